```python
import jax, jax.numpy as jnp
from jax import lax
import numpy as np

D_MODEL = 4096
BATCH = 4
SEQ = 2048
DEPTH = 1

N_META = 16
MIX_WIDTH = D_MODEL
LRU_WIDTH = MIX_WIDTH // 2
LRU_HEADS = 16
LRU_HEAD_DIM = LRU_WIDTH // LRU_HEADS
LRU_CONV_WIDTH = 4
LRU_C = 8.0
CONF_WIDTH = MIX_WIDTH - LRU_WIDTH
CONF_GROUPS = 16
CONF_GROUP_DIM = CONF_WIDTH // CONF_GROUPS
CONF_KERNEL = 31
N_EXPERTS = 32
TOP_K = 4
EXPERT_FF = D_MODEL // 2
SWIGLU_ALPHA = 1.702
SWIGLU_LIMIT = 7.0
MOE_BLOCK = 128
RMS_EPS = 1e-5
LN_EPS = 1e-5

kernel_name = "hybrid_rglru_conformer_moe_encoder"


def rmsnorm(x, g):
    x32 = x.astype(jnp.float32)
    y = x32 * lax.rsqrt(jnp.mean(x32 * x32, axis=-1, keepdims=True) + RMS_EPS)
    return (y * g.astype(jnp.float32)).astype(x.dtype)


def dwconv(x, w, b, pad):
    y = lax.conv_general_dilated(
        x, w[:, None, :].astype(x.dtype), window_strides=(1,), padding=[pad],
        dimension_numbers=("NWC", "WIO", "NWC"), feature_group_count=x.shape[-1])
    return y + b.astype(x.dtype)


def rglru_direction(u_in, conv_w, conv_b, w_a, b_a, w_i, b_i, lam):
    B, S, C = u_in.shape
    u = dwconv(u_in, conv_w, conv_b, (LRU_CONV_WIDTH - 1, 0))
    uh = u.reshape(B, S, LRU_HEADS, LRU_HEAD_DIM).astype(jnp.float32)
    r = jax.nn.sigmoid(jnp.einsum("bshd,hde->bshe", uh, w_a.astype(jnp.float32)) + b_a.astype(jnp.float32)).reshape(B, S, C)
    i = jax.nn.sigmoid(jnp.einsum("bshd,hde->bshe", uh, w_i.astype(jnp.float32)) + b_i.astype(jnp.float32)).reshape(B, S, C)
    log_a = -LRU_C * r * jax.nn.softplus(-lam.astype(jnp.float32))
    a = jnp.exp(log_a)
    bterm = jnp.sqrt(-jnp.expm1(2.0 * log_a)) * (i * u.astype(jnp.float32))

    def step(h, ab):
        a_t, b_t = ab
        h = a_t * h + b_t
        return h, h

    _, hs = lax.scan(step, jnp.zeros((B, C), jnp.float32),
                     (jnp.swapaxes(a, 0, 1), jnp.swapaxes(bterm, 0, 1)))
    return jnp.swapaxes(hs, 0, 1).astype(u_in.dtype)


def group_layernorm(x, g, b):
    B, S, C = x.shape
    x32 = x.astype(jnp.float32).reshape(B, S, CONF_GROUPS, CONF_GROUP_DIM)
    mu = jnp.mean(x32, axis=-1, keepdims=True)
    var = jnp.mean(jnp.square(x32 - mu), axis=-1, keepdims=True)
    y = ((x32 - mu) * lax.rsqrt(var + LN_EPS)).reshape(B, S, C)
    return (y * g.astype(jnp.float32) + b.astype(jnp.float32)).astype(x.dtype)


def moe(x2, w_router, b_router, w_gate, b_gate, w_up, b_up, w_down, b_down):
    T, D = x2.shape
    E, K, BLK = N_EXPERTS, TOP_K, MOE_BLOCK
    A = T * K
    logits = x2.astype(jnp.float32) @ w_router.astype(jnp.float32) + b_router.astype(jnp.float32)
    top_val, top_idx = lax.top_k(logits, K)
    gates = jax.nn.softmax(top_val, axis=-1)
    flat_e = top_idx.reshape(A).astype(jnp.int32)
    flat_tok = jnp.repeat(jnp.arange(T, dtype=jnp.int32), K)
    flat_g = gates.reshape(A)
    order = jnp.argsort(flat_e, stable=True)
    se, stok, sg = flat_e[order], flat_tok[order], flat_g[order]
    counts = jnp.bincount(flat_e, length=E).astype(jnp.int32)
    padded = (counts + BLK - 1) // BLK * BLK
    pend = jnp.cumsum(padded)
    pstart = pend - padded
    ustart = jnp.cumsum(counts) - counts
    dest = pstart[se] + jnp.arange(A, dtype=jnp.int32) - ustart[se]
    NB = -(-A // BLK) + E
    P = NB * BLK
    row_tok = jnp.full((P,), T, jnp.int32).at[dest].set(stok)
    row_g = jnp.zeros((P,), jnp.float32).at[dest].set(sg)
    block_e = jnp.minimum(jnp.searchsorted(pend, jnp.arange(NB, dtype=jnp.int32) * BLK, side="right"), E - 1)
    x_pad = jnp.concatenate([x2, jnp.zeros((1, D), x2.dtype)], axis=0)

    def block_fn(args):
        tok_b, e = args
        xb = x_pad[tok_b]
        hg = jnp.minimum(xb @ w_gate[e] + b_gate[e], SWIGLU_LIMIT)
        hu = jnp.clip(xb @ w_up[e] + b_up[e], -SWIGLU_LIMIT, SWIGLU_LIMIT)
        act = hg * jax.nn.sigmoid(SWIGLU_ALPHA * hg) * (hu + 1.0)
        return act @ w_down[e] + b_down[e]

    y_rows = lax.map(block_fn, (row_tok.reshape(NB, BLK), block_e)).reshape(P, D)
    y_rows = y_rows * row_g[:, None].astype(y_rows.dtype)
    return jax.ops.segment_sum(y_rows, row_tok, num_segments=T + 1)[:T]


def setup_inputs(seed: int = 0) -> dict:
    key = jax.random.key(seed)
    ks = jax.random.split(key, 32)
    L, D, F, E = DEPTH, D_MODEL, EXPERT_FF, N_EXPERTS
    n_in = 2 * LRU_WIDTH + 2 * CONF_WIDTH
    nrm = lambda k, shape, s: jax.random.normal(k, shape, jnp.float32) * s
    u = jax.random.uniform(ks[10], (L, 2, LRU_WIDTH), jnp.float32, 0.9, 0.999)
    s = u ** (1.0 / LRU_C)
    return {
        "x": nrm(ks[0], (BATCH, SEQ, D), 1.0),
        "meta_tokens": nrm(ks[1], (N_META, D), 1.0),
        "norm1_g": 1.0 + nrm(ks[2], (L, D), 0.02),
        "w_in": nrm(ks[3], (L, D, n_in), D ** -0.5),
        "lru_conv_w": nrm(ks[4], (L, 2, LRU_CONV_WIDTH, LRU_WIDTH), LRU_CONV_WIDTH ** -0.5),
        "lru_conv_b": nrm(ks[5], (L, 2, LRU_WIDTH), 0.01),
        "lru_w_a": nrm(ks[6], (L, 2, LRU_HEADS, LRU_HEAD_DIM, LRU_HEAD_DIM), LRU_HEAD_DIM ** -0.5),
        "lru_b_a": nrm(ks[7], (L, 2, LRU_HEADS, LRU_HEAD_DIM), 0.01),
        "lru_w_i": nrm(ks[8], (L, 2, LRU_HEADS, LRU_HEAD_DIM, LRU_HEAD_DIM), LRU_HEAD_DIM ** -0.5),
        "lru_b_i": nrm(ks[9], (L, 2, LRU_HEADS, LRU_HEAD_DIM), 0.01),
        "lru_lambda": jnp.log(s) - jnp.log1p(-s),
        "conf_conv_w": nrm(ks[11], (L, CONF_KERNEL, CONF_WIDTH), CONF_KERNEL ** -0.5),
        "conf_conv_b": nrm(ks[12], (L, CONF_WIDTH), 0.01),
        "conf_norm_g": 1.0 + nrm(ks[13], (L, CONF_WIDTH), 0.02),
        "conf_norm_b": nrm(ks[14], (L, CONF_WIDTH), 0.01),
        "w_out": nrm(ks[15], (L, MIX_WIDTH, D), MIX_WIDTH ** -0.5),
        "norm2_g": 1.0 + nrm(ks[16], (L, D), 0.02),
        "w_router": nrm(ks[17], (L, D, E), D ** -0.5),
        "b_router": nrm(ks[18], (L, E), 0.01),
        "w_gate": nrm(ks[19], (L, E, D, F), D ** -0.5),
        "b_gate": nrm(ks[20], (L, E, F), 0.01),
        "w_up": nrm(ks[21], (L, E, D, F), D ** -0.5),
        "b_up": nrm(ks[22], (L, E, F), 0.01),
        "w_down": nrm(ks[23], (L, E, F, D), F ** -0.5),
        "b_down": nrm(ks[24], (L, E, D), 0.01),
        "final_norm_g": 1.0 + nrm(ks[25], (D,), 0.02),
    }


def reference(x, meta_tokens, norm1_g, w_in, lru_conv_w, lru_conv_b, lru_w_a, lru_b_a,
              lru_w_i, lru_b_i, lru_lambda, conf_conv_w, conf_conv_b, conf_norm_g,
              conf_norm_b, w_out, norm2_g, w_router, b_router, w_gate, b_gate, w_up,
              b_up, w_down, b_down, final_norm_g):
    B, S, D = x.shape
    meta = jnp.broadcast_to(meta_tokens.astype(x.dtype)[None], (B, N_META, D))
    x = jnp.concatenate([meta, x], axis=1)
    St = S + N_META
    for l in range(DEPTH):
        h = rmsnorm(x, norm1_g[l])
        z = h @ w_in[l]
        lru_x, lru_gate, conf_a, conf_b = jnp.split(
            z, [LRU_WIDTH, 2 * LRU_WIDTH, 2 * LRU_WIDTH + CONF_WIDTH], axis=-1)
        h_fwd = rglru_direction(lru_x, lru_conv_w[l, 0], lru_conv_b[l, 0], lru_w_a[l, 0],
                                lru_b_a[l, 0], lru_w_i[l, 0], lru_b_i[l, 0], lru_lambda[l, 0])
        h_bwd = rglru_direction(lru_x[:, ::-1], lru_conv_w[l, 1], lru_conv_b[l, 1], lru_w_a[l, 1],
                                lru_b_a[l, 1], lru_w_i[l, 1], lru_b_i[l, 1], lru_lambda[l, 1])[:, ::-1]
        y_lru = (h_fwd + h_bwd) * jax.nn.gelu(lru_gate)
        c = conf_a * jax.nn.sigmoid(conf_b)
        c = dwconv(c, conf_conv_w[l], conf_conv_b[l], (CONF_KERNEL // 2, CONF_KERNEL // 2))
        c = jax.nn.silu(group_layernorm(c, conf_norm_g[l], conf_norm_b[l]))
        x = x + jnp.concatenate([y_lru, c], axis=-1) @ w_out[l]
        h2 = rmsnorm(x, norm2_g[l]).reshape(B * St, D)
        y_moe = moe(h2, w_router[l], b_router[l], w_gate[l], b_gate[l], w_up[l], b_up[l],
                    w_down[l], b_down[l])
        x = x + y_moe.reshape(B, St, D)
    y = rmsnorm(x, final_norm_g)
    return y[:, N_META:]
```

```python
import functools

import jax
import jax.numpy as jnp
from jax import lax
from jax.experimental import pallas as pl
from jax.experimental.pallas import tpu as pltpu

D_MODEL = 4096
N_META = 16
LRU_WIDTH = 2048
LRU_HEADS = 16
HEAD_DIM = 128
LRU_CONV_WIDTH = 4
LRU_C = 8.0
CONF_WIDTH = 2048
CONF_GROUPS = 16
CONF_KERNEL = 31
N_EXPERTS = 32
TOP_K = 4
EXPERT_FF = 2048
SWIGLU_ALPHA = 1.702
SWIGLU_LIMIT = 7.0
RMS_EPS = 1e-5
LN_EPS = 1e-5

SUBLANES = 8
VMEM_LIMIT = 56 * 1024 * 1024

SLOT_ROWS = 1152
BF16 = jnp.bfloat16
F32 = jnp.float32


def _params(*sem):
    return pltpu.CompilerParams(dimension_semantics=sem, vmem_limit_bytes=VMEM_LIMIT)


def _rmsnorm_kernel(x_ref, g_ref, o_ref):
    x = x_ref[...]
    ms = jnp.mean(x * x, axis=-1, keepdims=True)
    o_ref[...] = (x * lax.rsqrt(ms + RMS_EPS) * g_ref[...]).astype(o_ref.dtype)


def _rmsnorm(x, g, tm):
    t, d = x.shape
    return pl.pallas_call(
        _rmsnorm_kernel,
        grid=(t // tm,),
        in_specs=[pl.BlockSpec((tm, d), lambda i: (i, 0)),
                  pl.BlockSpec((1, d), lambda i: (0, 0))],
        out_specs=pl.BlockSpec((tm, d), lambda i: (i, 0)),
        out_shape=jax.ShapeDtypeStruct((t, d), BF16),
        compiler_params=_params("parallel"),
        name="rmsnorm1",
    )(x, g.reshape(1, d))


def _inproj_kernel(a_ref, w_ref, o_ref):
    o_ref[...] = jnp.dot(a_ref[...], w_ref[...].astype(BF16),
                         preferred_element_type=F32)


def _inproj(a, w, tm, tn):
    t, k = a.shape
    n = w.shape[1]
    return pl.pallas_call(
        _inproj_kernel,
        grid=(t // tm, n // tn),
        in_specs=[pl.BlockSpec((tm, k), lambda i, j: (i, 0)),
                  pl.BlockSpec((k, tn), lambda i, j: (0, j))],
        out_specs=pl.BlockSpec((tm, tn), lambda i, j: (i, j)),
        out_shape=jax.ShapeDtypeStruct((t, n), F32),
        compiler_params=_params("parallel", "parallel"),
        name="inproj",
    )(a, w)


def _tile_scan(a, b, reverse):
    n = a.shape[0]
    row = lax.broadcasted_iota(jnp.int32, a.shape, 0) & (SUBLANES - 1)
    for s in (1, 2, 4):
        if reverse:
            a_sh = pltpu.roll(a, n - s, 0)
            b_sh = pltpu.roll(b, n - s, 0)
            m = row < SUBLANES - s
        else:
            a_sh = pltpu.roll(a, s, 0)
            b_sh = pltpu.roll(b, s, 0)
            m = row >= s
        b = jnp.where(m, a * b_sh + b, b)
        a = jnp.where(m, a * a_sh, a)
    return a, b


def _gelu_tanh(x):
    return 0.5 * x * (1.0 + jnp.tanh(0.7978845608028654 * (x + 0.044715 * x * x * x)))


def _lru_kernel(xr_ref, gate_ref, xm_ref, cw_ref, cb_ref, wa_ref, ba_ref, wi_ref,
                bi_ref, lam_ref, o_ref, sf_ref, af_ref, bf_ref, ab_ref, bb_ref):
    s = xr_ref.shape[0]
    st = s + N_META
    pad = SUBLANES
    zeros8 = jnp.zeros((pad, HEAD_DIM), F32)
    sf_ref[0:pad, :] = zeros8
    sf_ref[pad:pad + N_META, :] = xm_ref[...]
    sf_ref[pad + N_META:pad + st, :] = xr_ref[...]
    sf_ref[pad + st:pad + st + pad, :] = zeros8

    def gates(u, d):
        ub = u.astype(BF16)
        r = jax.nn.sigmoid(jnp.dot(ub, wa_ref[d].astype(BF16), preferred_element_type=F32)
                           + ba_ref[d:d + 1, :])
        i = jax.nn.sigmoid(jnp.dot(ub, wi_ref[d].astype(BF16), preferred_element_type=F32)
                           + bi_ref[d:d + 1, :])
        lam = lam_ref[d:d + 1, :]
        softplus_neg = jnp.maximum(-lam, 0.0) + jnp.log1p(jnp.exp(-jnp.abs(lam)))
        log_a = (-LRU_C) * r * softplus_neg
        a = jnp.exp(log_a)
        b = jnp.sqrt(-jnp.tanh(log_a) * (a * a + 1.0)) * (i * u)
        return a, b

    uf = cb_ref[0:1, :] + cw_ref[0, 3:4, :] * sf_ref[pl.ds(pad, st), :]
    for j in range(1, LRU_CONV_WIDTH):
        uf = uf + cw_ref[0, 3 - j:4 - j, :] * sf_ref[pl.ds(pad - j, st), :]
    a, b = gates(uf, 0)
    a, b = _tile_scan(a, b, reverse=False)
    af_ref[...] = a
    bf_ref[...] = b

    base = pad + N_META
    ub_ = cb_ref[1:2, :] + cw_ref[1, 3:4, :] * sf_ref[pl.ds(base, s), :]
    for j in range(1, LRU_CONV_WIDTH):
        ub_ = ub_ + cw_ref[1, 3 - j:4 - j, :] * sf_ref[pl.ds(base + j, s), :]
    a, b = gates(ub_, 1)
    a, b = _tile_scan(a, b, reverse=True)
    ab_ref[...] = a
    bb_ref[...] = b

    def fwd_tile(j, c):
        sl = pl.ds(pl.multiple_of(j * SUBLANES, SUBLANES), SUBLANES)
        h = af_ref[sl, :] * c + bf_ref[sl, :]
        bf_ref[sl, :] = h
        return h[SUBLANES - 1:SUBLANES, :]

    c0 = jnp.zeros((1, HEAD_DIM), F32)
    cf = lax.fori_loop(0, N_META // SUBLANES, fwd_tile, c0)
    n_tiles = s // SUBLANES

    def both_tiles(j, carry):
        cf, cb = carry
        slf = pl.ds(pl.multiple_of(N_META + j * SUBLANES, SUBLANES), SUBLANES)
        hf = af_ref[slf, :] * cf + bf_ref[slf, :]
        bf_ref[slf, :] = hf
        slb = pl.ds(pl.multiple_of((n_tiles - 1 - j) * SUBLANES, SUBLANES), SUBLANES)
        hb = ab_ref[slb, :] * cb + bb_ref[slb, :]
        bb_ref[slb, :] = hb
        return hf[SUBLANES - 1:SUBLANES, :], hb[0:1, :]

    lax.fori_loop(0, n_tiles, both_tiles, (cf, c0))
    y = (bf_ref[pl.ds(N_META, s), :] + bb_ref[...]) * _gelu_tanh(gate_ref[...])
    o_ref[...] = y.astype(o_ref.dtype)


def _lru_mixer(z3, z_meta, cw, cb, wa, ba, wi, bi, lam):
    b, s, _ = z3.shape
    st = s + N_META
    hd = HEAD_DIM
    col = lambda off: (lambda bi_, h: (bi_, 0, off + h))
    return pl.pallas_call(
        _lru_kernel,
        grid=(b, LRU_HEADS),
        in_specs=[
            pl.BlockSpec((None, s, hd), col(0)),
            pl.BlockSpec((None, s, hd), col(LRU_HEADS)),
            pl.BlockSpec((N_META, hd), lambda bi_, h: (0, h)),
            pl.BlockSpec((2, LRU_CONV_WIDTH, hd), lambda bi_, h: (0, 0, h)),
            pl.BlockSpec((2, hd), lambda bi_, h: (0, h)),
            pl.BlockSpec((2, None, hd, hd), lambda bi_, h: (0, h, 0, 0)),
            pl.BlockSpec((2, hd), lambda bi_, h: (0, h)),
            pl.BlockSpec((2, None, hd, hd), lambda bi_, h: (0, h, 0, 0)),
            pl.BlockSpec((2, hd), lambda bi_, h: (0, h)),
            pl.BlockSpec((2, hd), lambda bi_, h: (0, h)),
        ],
        out_specs=pl.BlockSpec((None, s, hd), lambda bi_, h: (bi_, 0, h)),
        out_shape=jax.ShapeDtypeStruct((b, s, LRU_WIDTH), BF16),
        scratch_shapes=[
            pltpu.VMEM((st + 2 * SUBLANES, hd), F32),
            pltpu.VMEM((st, hd), F32),
            pltpu.VMEM((st, hd), F32),
            pltpu.VMEM((s, hd), F32),
            pltpu.VMEM((s, hd), F32),
        ],
        compiler_params=_params("parallel", "parallel"),
        name="lru_mixer",
    )(z3, z3, z_meta, cw, cb, wa, ba.reshape(2, LRU_WIDTH), wi, bi.reshape(2, LRU_WIDTH), lam)


CONF_CHUNK = 64


def _conf_kernel(a_ref, b_ref, am_ref, bm_ref, cw_ref, cb_ref, g_ref, be_ref, o_ref,
                 cs_ref, sh_ref):
    s = a_ref.shape[0]
    st = s + N_META
    n_sh = sh_ref.shape[1]
    cs_ref[0:N_META, :] = am_ref[...] * jax.nn.sigmoid(bm_ref[...])
    cs_ref[N_META:st, :] = a_ref[...] * jax.nn.sigmoid(b_ref[...])
    cs_ref[st:st + N_META, :] = jnp.zeros((N_META, HEAD_DIM), F32)
    for r in range(SUBLANES):
        sh_ref[r] = cs_ref[pl.ds(r, n_sh), :]

    def chunk(j, _):
        row0 = pl.multiple_of(j * CONF_CHUNK, CONF_CHUNK)
        acc = jnp.broadcast_to(cb_ref[...], (CONF_CHUNK, HEAD_DIM))
        for k in range(CONF_KERNEL):
            off = N_META - CONF_KERNEL // 2 + k
            r, q = off % SUBLANES, off // SUBLANES
            acc = acc + cw_ref[k:k + 1, :] * sh_ref[r, pl.ds(row0 + q * SUBLANES, CONF_CHUNK), :]
        mu = jnp.mean(acc, axis=-1, keepdims=True)
        xc = acc - mu
        var = jnp.mean(xc * xc, axis=-1, keepdims=True)
        y = xc * lax.rsqrt(var + LN_EPS) * g_ref[...] + be_ref[...]
        o_ref[pl.ds(row0, CONF_CHUNK), :] = (y * jax.nn.sigmoid(y)).astype(o_ref.dtype)
        return 0

    lax.fori_loop(0, s // CONF_CHUNK, chunk, 0)


def _conf_mixer(z3, z_meta, cw, cb, g, be):
    b, s, _ = z3.shape
    st = s + N_META
    hd = HEAD_DIM
    a_off = 2 * LRU_HEADS
    b_off = 2 * LRU_HEADS + CONF_GROUPS
    n_sh = st + N_META - SUBLANES
    vec = pl.BlockSpec((1, hd), lambda bi_, h: (0, h))
    return pl.pallas_call(
        _conf_kernel,
        grid=(b, CONF_GROUPS),
        in_specs=[
            pl.BlockSpec((None, s, hd), lambda bi_, h: (bi_, 0, a_off + h)),
            pl.BlockSpec((None, s, hd), lambda bi_, h: (bi_, 0, b_off + h)),
            pl.BlockSpec((N_META, hd), lambda bi_, h: (0, a_off + h)),
            pl.BlockSpec((N_META, hd), lambda bi_, h: (0, b_off + h)),
            pl.BlockSpec((CONF_KERNEL, hd), lambda bi_, h: (0, h)),
            vec, vec, vec,
        ],
        out_specs=pl.BlockSpec((None, s, hd), lambda bi_, h: (bi_, 0, h)),
        out_shape=jax.ShapeDtypeStruct((b, s, CONF_WIDTH), BF16),
        scratch_shapes=[
            pltpu.VMEM((st + N_META, hd), F32),
            pltpu.VMEM((SUBLANES, n_sh, hd), F32),
        ],
        compiler_params=_params("parallel", "parallel"),
        name="conf_mixer",
    )(z3, z3, z_meta, z_meta, cw, cb.reshape(1, -1), g.reshape(1, -1), be.reshape(1, -1))


def _outproj_kernel(ya_ref, yb_ref, wa_ref, wb_ref, x_ref, o_ref):
    acc = jnp.dot(ya_ref[...], wa_ref[...].astype(BF16), preferred_element_type=F32)
    acc = acc + jnp.dot(yb_ref[...], wb_ref[...].astype(BF16), preferred_element_type=F32)
    o_ref[...] = x_ref[...] + acc


def _outproj(ya, yb, w, x, tm, tn):
    t, k = ya.shape
    n = w.shape[1]
    return pl.pallas_call(
        _outproj_kernel,
        grid=(t // tm, n // tn),
        in_specs=[pl.BlockSpec((tm, k), lambda i, j: (i, 0)),
                  pl.BlockSpec((tm, k), lambda i, j: (i, 0)),
                  pl.BlockSpec((k, tn), lambda i, j: (0, j)),
                  pl.BlockSpec((k, tn), lambda i, j: (1, j)),
                  pl.BlockSpec((tm, tn), lambda i, j: (i, j))],
        out_specs=pl.BlockSpec((tm, tn), lambda i, j: (i, j)),
        out_shape=jax.ShapeDtypeStruct((t, n), F32),
        compiler_params=_params("parallel", "parallel"),
        name="outproj",
    )(ya, yb, w, w, x)


def _router_kernel(x_ref, g_ref, wr_ref, br_ref, hp_ref, idx_ref, gate_ref, rank_ref,
                   cnt_ref, carry_ref):
    tt = x_ref.shape[0]
    half = D_MODEL // 2

    @pl.when(pl.program_id(0) == 0)
    def _():
        carry_ref[...] = jnp.zeros_like(carry_ref)

    x = x_ref[...]
    ms = jnp.mean(x * x, axis=-1, keepdims=True)
    h = x * lax.rsqrt(ms + RMS_EPS) * g_ref[...]
    h_hi = h.astype(BF16)
    hi_bits = lax.bitcast_convert_type(h_hi[:, :half].astype(F32), jnp.uint32)
    lo_bits = lax.bitcast_convert_type(h_hi[:, half:].astype(F32), jnp.uint32)
    hp_ref[...] = hi_bits | lax.shift_right_logical(lo_bits, jnp.uint32(16))

    h_lo = (h - h_hi.astype(F32)).astype(BF16)
    w = wr_ref[...]
    w_hi = w.astype(BF16)
    w_lo = (w - w_hi.astype(F32)).astype(BF16)
    logits = (jnp.dot(h_hi, w_hi, preferred_element_type=F32)
              + jnp.dot(h_lo, w_hi, preferred_element_type=F32)
              + jnp.dot(h_hi, w_lo, preferred_element_type=F32)) + br_ref[...]

    lane = lax.broadcasted_iota(jnp.int32, (tt, N_EXPERTS), 1)
    lane_k = lax.broadcasted_iota(jnp.int32, (tt, TOP_K), 1)
    work = logits
    vals, sels = [], []
    idx_out = jnp.zeros((tt, TOP_K), jnp.int32)
    for k in range(TOP_K):
        m = jnp.max(work, axis=1, keepdims=True)
        am = jnp.min(jnp.where(work == m, lane, N_EXPERTS), axis=1, keepdims=True)
        sel = lane == am
        vals.append(m)
        sels.append(sel)
        idx_out = jnp.where(lane_k == k, am, idx_out)
        work = jnp.where(sel, -jnp.inf, work)
    idx_ref[...] = idx_out

    exps = [jnp.exp(v - vals[0]) for v in vals]
    denom = exps[0] + exps[1] + exps[2] + exps[3]
    gate_out = jnp.zeros((tt, TOP_K), F32)
    for k in range(TOP_K):
        gate_out = jnp.where(lane_k == k, exps[k] / denom, gate_out)
    gate_ref[...] = gate_out

    onehot = jnp.zeros((tt, N_EXPERTS), F32)
    for sel in sels:
        onehot = onehot + sel.astype(F32)
    r_i = lax.broadcasted_iota(jnp.int32, (tt, tt), 0)
    c_i = lax.broadcasted_iota(jnp.int32, (tt, tt), 1)
    tri = (c_i < r_i).astype(BF16)
    before = jnp.dot(tri, onehot.astype(BF16), preferred_element_type=F32) + carry_ref[...]
    rank_out = jnp.zeros((tt, TOP_K), jnp.int32)
    for k, sel in enumerate(sels):
        rk = jnp.sum(jnp.where(sel, before, 0.0), axis=1, keepdims=True).astype(jnp.int32)
        rank_out = jnp.where(lane_k == k, rk, rank_out)
    rank_ref[...] = rank_out
    carry_ref[...] = carry_ref[...] + jnp.sum(onehot, axis=0, keepdims=True)
    cnt_ref[...] = carry_ref[...].astype(jnp.int32)


def _router(x, g, wr, br, tt):
    t, d = x.shape
    small = lambda dt: jax.ShapeDtypeStruct((t, TOP_K), dt)
    return pl.pallas_call(
        _router_kernel,
        grid=(t // tt,),
        in_specs=[pl.BlockSpec((tt, d), lambda i: (i, 0)),
                  pl.BlockSpec((1, d), lambda i: (0, 0)),
                  pl.BlockSpec((d, N_EXPERTS), lambda i: (0, 0)),
                  pl.BlockSpec((1, N_EXPERTS), lambda i: (0, 0))],
        out_specs=[pl.BlockSpec((tt, d // 2), lambda i: (i, 0)),
                   pl.BlockSpec((tt, TOP_K), lambda i: (i, 0)),
                   pl.BlockSpec((tt, TOP_K), lambda i: (i, 0)),
                   pl.BlockSpec((tt, TOP_K), lambda i: (i, 0)),
                   pl.BlockSpec((1, N_EXPERTS), lambda i: (0, 0))],
        out_shape=[jax.ShapeDtypeStruct((t, d // 2), jnp.uint32),
                   small(jnp.int32), small(F32), small(jnp.int32),
                   jax.ShapeDtypeStruct((1, N_EXPERTS), jnp.int32)],
        scratch_shapes=[pltpu.VMEM((1, N_EXPERTS), F32)],
        compiler_params=_params("arbitrary"),
        name="router",
    )(x, g.reshape(1, d), wr, br.reshape(1, N_EXPERTS))


def _row_copy(src_hbm, row, dst, slot, sem):
    return pltpu.make_async_copy(src_hbm.at[pl.ds(row, 1)], dst.at[pl.ds(slot, 1)], sem)


def _moe_up_kernel(slot_e_ref, nused_ref, tok_ref, hp_hbm, wg_ref, wu_ref, bg_ref, bu_ref,
                   o_ref, gbuf_ref, xb_ref, sem):
    w = pl.program_id(0)
    f = pl.program_id(1)
    rows = gbuf_ref.shape[0]
    half = D_MODEL // 2
    valid = w < nused_ref[0]

    @pl.when(jnp.logical_and(valid, f == 0))
    def _():
        def issue(r, _):
            _row_copy(hp_hbm, tok_ref[w * rows + r], gbuf_ref, r, sem).start()
            return 0

        lax.fori_loop(0, rows, issue, 0)

        def drain(r, _):
            _row_copy(hp_hbm, 0, gbuf_ref, r, sem).wait()
            return 0

        lax.fori_loop(0, rows, drain, 0)
        u = gbuf_ref[...]
        hi = lax.bitcast_convert_type(u & jnp.uint32(0xFFFF0000), F32)
        lo = lax.bitcast_convert_type(lax.shift_left(u, jnp.uint32(16)), F32)
        xb_ref[:, :half] = hi.astype(BF16)
        xb_ref[:, half:] = lo.astype(BF16)

    @pl.when(valid)
    def _():
        x = xb_ref[...]
        hg = jnp.dot(x, wg_ref[...].astype(BF16), preferred_element_type=F32) + bg_ref[...]
        hu = jnp.dot(x, wu_ref[...].astype(BF16), preferred_element_type=F32) + bu_ref[...]
        hg = jnp.minimum(hg, SWIGLU_LIMIT)
        hu = jnp.clip(hu, -SWIGLU_LIMIT, SWIGLU_LIMIT)
        act = hg * jax.nn.sigmoid(SWIGLU_ALPHA * hg) * (hu + 1.0)
        o_ref[...] = act.astype(o_ref.dtype)


def _moe_up(slot_e, nused, tok_idx, hp, wg, wu, bg, bu, n_slots, tf):
    e, d, f = wg.shape
    nf = f // tf
    rows = SLOT_ROWS

    def w_map(w, j, se, nu, tok):
        return (se[w], 0, jnp.where(w < nu[0], j, nf - 1))

    def o_map(w, j, se, nu, tok):
        ok = w < nu[0]
        return (jnp.where(ok, w, nu[0] - 1), jnp.where(ok, j, nf - 1))

    grid_spec = pltpu.PrefetchScalarGridSpec(
        num_scalar_prefetch=3,
        grid=(n_slots, nf),
        in_specs=[pl.BlockSpec(memory_space=pl.ANY),
                  pl.BlockSpec((None, d, tf), w_map),
                  pl.BlockSpec((None, d, tf), w_map),
                  pl.BlockSpec((None, 1, tf), w_map),
                  pl.BlockSpec((None, 1, tf), w_map)],
        out_specs=pl.BlockSpec((rows, tf), o_map),
        scratch_shapes=[pltpu.VMEM((rows, d // 2), jnp.uint32),
                        pltpu.VMEM((rows, d), BF16),
                        pltpu.SemaphoreType.DMA(())],
    )
    return pl.pallas_call(
        _moe_up_kernel,
        grid_spec=grid_spec,
        out_shape=jax.ShapeDtypeStruct((n_slots * rows, f), BF16),
        compiler_params=_params("arbitrary", "arbitrary"),
        name="moe_up",
    )(slot_e, nused, tok_idx, hp, wg, wu, bg.reshape(e, 1, f), bu.reshape(e, 1, f))


def _moe_down_kernel(slot_e_ref, nused_ref, a_ref, wd_ref, bd_ref, o_ref):
    @pl.when(pl.program_id(0) < nused_ref[0])
    def _():
        o_ref[...] = jnp.dot(a_ref[...], wd_ref[...].astype(BF16),
                             preferred_element_type=F32) + bd_ref[...]


def _moe_down(slot_e, nused, act, wd, bd, n_slots, td):
    e, f, d = wd.shape
    nd = d // td
    rows = SLOT_ROWS

    def w_map(w, j, se, nu):
        return (se[w], 0, jnp.where(w < nu[0], j, nd - 1))

    def a_map(w, j, se, nu):
        return (jnp.where(w < nu[0], w, nu[0] - 1), 0)

    def o_map(w, j, se, nu):
        ok = w < nu[0]
        return (jnp.where(ok, w, nu[0] - 1), jnp.where(ok, j, nd - 1))

    grid_spec = pltpu.PrefetchScalarGridSpec(
        num_scalar_prefetch=2,
        grid=(n_slots, nd),
        in_specs=[pl.BlockSpec((rows, f), a_map),
                  pl.BlockSpec((None, f, td), w_map),
                  pl.BlockSpec((None, 1, td), w_map)],
        out_specs=pl.BlockSpec((rows, td), o_map),
    )
    return pl.pallas_call(
        _moe_down_kernel,
        grid_spec=grid_spec,
        out_shape=jax.ShapeDtypeStruct((n_slots * rows, d), F32),
        compiler_params=_params("arbitrary", "arbitrary"),
        name="moe_down",
    )(slot_e, nused, act, wd, bd.reshape(e, 1, d))


def _combine_kernel(pos_ref, y_hbm, x_ref, gate_ref, g_ref, o_ref, ybuf_ref, sem):
    i = pl.program_id(0)
    tt = x_ref.shape[0]

    def issue(n, _):
        t = n // TOP_K
        k = n % TOP_K
        pltpu.make_async_copy(y_hbm.at[pl.ds(pos_ref[i * tt * TOP_K + n], 1)],
                              ybuf_ref.at[k, pl.ds(t, 1)], sem).start()
        return 0

    lax.fori_loop(0, tt * TOP_K, issue, 0)

    def drain(n, _):
        pltpu.make_async_copy(y_hbm.at[pl.ds(0, 1)], ybuf_ref.at[0, pl.ds(0, 1)], sem).wait()
        return 0

    lax.fori_loop(0, tt * TOP_K, drain, 0)
    gates = gate_ref[...]
    acc = x_ref[...]
    for k in range(TOP_K):
        acc = acc + gates[:, k:k + 1] * ybuf_ref[k]
    ms = jnp.mean(acc * acc, axis=-1, keepdims=True)
    o_ref[...] = acc * lax.rsqrt(ms + RMS_EPS) * g_ref[...]


def _combine(pos_flat, y, x, gates, g, tt):
    t, d = x.shape
    grid_spec = pltpu.PrefetchScalarGridSpec(
        num_scalar_prefetch=1,
        grid=(t // tt,),
        in_specs=[pl.BlockSpec(memory_space=pl.ANY),
                  pl.BlockSpec((tt, d), lambda i, p: (i, 0)),
                  pl.BlockSpec((tt, TOP_K), lambda i, p: (i, 0)),
                  pl.BlockSpec((1, d), lambda i, p: (0, 0))],
        out_specs=pl.BlockSpec((tt, d), lambda i, p: (i, 0)),
        scratch_shapes=[pltpu.VMEM((TOP_K, tt, d), F32),
                        pltpu.SemaphoreType.DMA(())],
    )
    return pl.pallas_call(
        _combine_kernel,
        grid_spec=grid_spec,
        out_shape=jax.ShapeDtypeStruct((t, d), F32),
        compiler_params=_params("arbitrary"),
        name="combine",
    )(pos_flat, y, x, gates, g.reshape(1, d))


def _routing_tables(idx, rank, counts, n_slots):
    t = idx.shape[0]
    rows = SLOT_ROWS
    slots_per_e = (counts + rows - 1) // rows
    slot_end = jnp.cumsum(slots_per_e)
    slot_start = slot_end - slots_per_e
    nused = slot_end[-1]
    pos = (slot_start[idx] + rank // rows) * rows + rank % rows
    w_ids = jnp.arange(n_slots, dtype=jnp.int32)
    slot_e = jnp.searchsorted(slot_end, jnp.minimum(w_ids, nused - 1), side="right")
    slot_e = jnp.minimum(slot_e, N_EXPERTS - 1).astype(jnp.int32)
    tok = jnp.repeat(jnp.arange(t, dtype=jnp.int32), TOP_K)
    tok_idx = jnp.zeros((n_slots * rows,), jnp.int32).at[pos.reshape(-1)].set(tok)
    return slot_e, nused.reshape(1).astype(jnp.int32), tok_idx, pos.reshape(-1).astype(jnp.int32)


def kernel(x, meta_tokens, norm1_g, w_in, lru_conv_w, lru_conv_b, lru_w_a, lru_b_a, lru_w_i, lru_b_i, lru_lambda, conf_conv_w, conf_conv_b, conf_norm_g, conf_norm_b, w_out, norm2_g, w_router, b_router, w_gate, b_gate, w_up, b_up, w_down, b_down, final_norm_g):
    b, s, d = x.shape
    t = b * s
    x2 = x.reshape(t, d)
    n_slots = N_EXPERTS + -(-(t * TOP_K) // SLOT_ROWS)

    h = _rmsnorm(x2, norm1_g[0], tm=512)
    h_meta = _rmsnorm(meta_tokens.astype(x.dtype), norm1_g[0], tm=N_META)
    z = _inproj(h, w_in[0], tm=1024, tn=512)
    z_meta = _inproj(h_meta, w_in[0], tm=N_META, tn=512)
    z3 = z.reshape(b, s, -1)

    y_lru = _lru_mixer(z3, z_meta, lru_conv_w[0], lru_conv_b[0], lru_w_a[0], lru_b_a[0],
                       lru_w_i[0], lru_b_i[0], lru_lambda[0])
    y_conf = _conf_mixer(z3, z_meta, conf_conv_w[0], conf_conv_b[0], conf_norm_g[0],
                         conf_norm_b[0])
    x_mid = _outproj(y_lru.reshape(t, -1), y_conf.reshape(t, -1), w_out[0], x2,
                     tm=1024, tn=512)

    hp, idx, gates, rank, counts = _router(x_mid, norm2_g[0], w_router[0], b_router[0], tt=512)
    slot_e, nused, tok_idx, pos_flat = _routing_tables(idx, rank, counts[0], n_slots)
    act = _moe_up(slot_e, nused, tok_idx, hp, w_gate[0], w_up[0], b_gate[0], b_up[0],
                  n_slots, tf=256)
    y_rows = _moe_down(slot_e, nused, act, w_down[0], b_down[0], n_slots, td=512)
    out = _combine(pos_flat, y_rows, x_mid, gates, final_norm_g, tt=128)
    return out.reshape(b, s, d)
```

```python
import jax
import jax.numpy as jnp
from jax import lax
from jax.experimental import pallas as pl
from jax.experimental.pallas import tpu as pltpu

N_META = 16
LRU_WIDTH = 2048
LRU_HEADS = 16
HEAD_DIM = 128
LRU_CONV_WIDTH = 4
LRU_C = 8.0
CONF_WIDTH = 2048
CONF_GROUPS = 16
CONF_KERNEL = 31
N_EXPERTS = 32
TOP_K = 4
SWIGLU_ALPHA = 1.702
SWIGLU_LIMIT = 7.0
RMS_EPS = 1e-5
LN_EPS = 1e-5
SQRT_FLOOR = 1e-30

SUBLANES = 8
VMEM_LIMIT = 56 * 1024 * 1024

ROW_BLOCK = 256
PASS_ROWS = 1536
BF16 = jnp.bfloat16
F32 = jnp.float32
U32 = jnp.uint32


def _params(*sem):
    return pltpu.CompilerParams(dimension_semantics=sem, vmem_limit_bytes=VMEM_LIMIT)


def _pack_bf16_pair(hi, lo):
    hi_bits = lax.bitcast_convert_type(hi.astype(BF16).astype(F32), U32)
    lo_bits = lax.bitcast_convert_type(lo.astype(BF16).astype(F32), U32)
    return hi_bits | lax.shift_right_logical(lo_bits, jnp.uint32(16))


def _sigmoid(x):
    return 0.5 * jnp.tanh(0.5 * x) + 0.5


def _unpack_bf16_pair(u):
    hi = lax.bitcast_convert_type(u & jnp.uint32(0xFFFF0000), F32)
    lo = lax.bitcast_convert_type(lax.shift_left(u, jnp.uint32(16)), F32)
    return hi, lo


def _rmsnorm_kernel(x_ref, g_ref, o_ref):
    x = x_ref[...]
    ms = jnp.mean(x * x, axis=-1, keepdims=True)
    o_ref[...] = (x * lax.rsqrt(ms + RMS_EPS) * g_ref[...]).astype(o_ref.dtype)


def _rmsnorm(x, g, tm):
    t, d = x.shape
    return pl.pallas_call(
        _rmsnorm_kernel,
        grid=(t // tm,),
        in_specs=[pl.BlockSpec((tm, d), lambda i: (i, 0)),
                  pl.BlockSpec((1, d), lambda i: (0, 0))],
        out_specs=pl.BlockSpec((tm, d), lambda i: (i, 0)),
        out_shape=jax.ShapeDtypeStruct((t, d), BF16),
        compiler_params=_params("parallel"),
        name="rmsnorm1",
    )(x, g.reshape(1, d))


def _inproj_kernel(a_ref, w_ref, o_ref):
    o_ref[...] = jnp.dot(a_ref[...], w_ref[...].astype(BF16),
                         preferred_element_type=F32)


def _inproj(a, w, tm, tn):
    t, k = a.shape
    n = w.shape[1]
    return pl.pallas_call(
        _inproj_kernel,
        grid=(t // tm, n // tn),
        in_specs=[pl.BlockSpec((tm, k), lambda i, j: (i, 0)),
                  pl.BlockSpec((k, tn), lambda i, j: (0, j))],
        out_specs=pl.BlockSpec((tm, tn), lambda i, j: (i, j)),
        out_shape=jax.ShapeDtypeStruct((t, n), F32),
        compiler_params=_params("parallel", "parallel"),
        name="inproj",
    )(a, w)


def _tile_scan(a, b, reverse):
    n = a.shape[0]
    row = lax.broadcasted_iota(jnp.int32, a.shape, 0) & (SUBLANES - 1)
    for s in (1, 2, 4):
        if reverse:
            a_sh = pltpu.roll(a, n - s, 0)
            b_sh = pltpu.roll(b, n - s, 0)
            m = row < SUBLANES - s
        else:
            a_sh = pltpu.roll(a, s, 0)
            b_sh = pltpu.roll(b, s, 0)
            m = row >= s
        b = jnp.where(m, a * b_sh + b, b)
        a = jnp.where(m, a * a_sh, a)
    return a, b


def _gelu_tanh(x):
    return 0.5 * x * (1.0 + jnp.tanh(0.7978845608028654 * (x + 0.044715 * x * x * x)))


def _lru_kernel(xr_ref, gate_ref, xm_ref, cw_ref, cb_ref, wa_ref, ba_ref, wi_ref,
                bi_ref, lam_ref, o_ref, sf_ref, af_ref, bf_ref, ab_ref, bb_ref):
    s = xr_ref.shape[0]
    st = s + N_META
    pad = SUBLANES
    zeros8 = jnp.zeros((pad, HEAD_DIM), F32)
    sf_ref[0:pad, :] = zeros8
    sf_ref[pad:pad + N_META, :] = xm_ref[...]
    sf_ref[pad + N_META:pad + st, :] = xr_ref[...]
    sf_ref[pad + st:pad + st + pad, :] = zeros8

    def gates(u, d):
        ub = u.astype(BF16)
        r = _sigmoid(jnp.dot(ub, wa_ref[d].astype(BF16), preferred_element_type=F32)
                           + ba_ref[d:d + 1, :])
        i = _sigmoid(jnp.dot(ub, wi_ref[d].astype(BF16), preferred_element_type=F32)
                           + bi_ref[d:d + 1, :])
        lam = lam_ref[d:d + 1, :]
        softplus_neg = jnp.maximum(-lam, 0.0) + jnp.log1p(jnp.exp(-jnp.abs(lam)))
        log_a = (-LRU_C) * r * softplus_neg
        a = jnp.exp(log_a)
        one_minus_a2 = -jnp.tanh(log_a) * (a * a + 1.0)
        root = one_minus_a2 * lax.rsqrt(jnp.maximum(one_minus_a2, SQRT_FLOOR))
        b = root * (i * u)
        return a, b

    uf = cb_ref[0:1, :] + cw_ref[0, 3:4, :] * sf_ref[pl.ds(pad, st), :]
    for j in range(1, LRU_CONV_WIDTH):
        uf = uf + cw_ref[0, 3 - j:4 - j, :] * sf_ref[pl.ds(pad - j, st), :]
    a, b = gates(uf, 0)
    a, b = _tile_scan(a, b, reverse=False)
    af_ref[...] = a
    bf_ref[...] = b

    base = pad + N_META
    ub_ = cb_ref[1:2, :] + cw_ref[1, 3:4, :] * sf_ref[pl.ds(base, s), :]
    for j in range(1, LRU_CONV_WIDTH):
        ub_ = ub_ + cw_ref[1, 3 - j:4 - j, :] * sf_ref[pl.ds(base + j, s), :]
    a, b = gates(ub_, 1)
    a, b = _tile_scan(a, b, reverse=True)
    ab_ref[...] = a
    bb_ref[...] = b

    def fwd_tile(j, c):
        sl = pl.ds(pl.multiple_of(j * SUBLANES, SUBLANES), SUBLANES)
        h = af_ref[sl, :] * c + bf_ref[sl, :]
        bf_ref[sl, :] = h
        return h[SUBLANES - 1:SUBLANES, :]

    c0 = jnp.zeros((1, HEAD_DIM), F32)
    cf = lax.fori_loop(0, N_META // SUBLANES, fwd_tile, c0)
    n_tiles = s // SUBLANES

    def both_tiles(j, carry):
        cf, cb = carry
        slf = pl.ds(pl.multiple_of(N_META + j * SUBLANES, SUBLANES), SUBLANES)
        hf = af_ref[slf, :] * cf + bf_ref[slf, :]
        bf_ref[slf, :] = hf
        slb = pl.ds(pl.multiple_of((n_tiles - 1 - j) * SUBLANES, SUBLANES), SUBLANES)
        hb = ab_ref[slb, :] * cb + bb_ref[slb, :]
        bb_ref[slb, :] = hb
        return hf[SUBLANES - 1:SUBLANES, :], hb[0:1, :]

    lax.fori_loop(0, n_tiles, both_tiles, (cf, c0))
    y = (bf_ref[pl.ds(N_META, s), :] + bb_ref[...]) * _gelu_tanh(gate_ref[...])
    o_ref[...] = y.astype(o_ref.dtype)


def _lru_mixer(z3, z_meta, cw, cb, wa, ba, wi, bi, lam):
    b, s, _ = z3.shape
    st = s + N_META
    hd = HEAD_DIM
    col = lambda off: (lambda bi_, h: (bi_, 0, off + h))
    return pl.pallas_call(
        _lru_kernel,
        grid=(b, LRU_HEADS),
        in_specs=[
            pl.BlockSpec((None, s, hd), col(0)),
            pl.BlockSpec((None, s, hd), col(LRU_HEADS)),
            pl.BlockSpec((N_META, hd), lambda bi_, h: (0, h)),
            pl.BlockSpec((2, LRU_CONV_WIDTH, hd), lambda bi_, h: (0, 0, h)),
            pl.BlockSpec((2, hd), lambda bi_, h: (0, h)),
            pl.BlockSpec((2, None, hd, hd), lambda bi_, h: (0, h, 0, 0)),
            pl.BlockSpec((2, hd), lambda bi_, h: (0, h)),
            pl.BlockSpec((2, None, hd, hd), lambda bi_, h: (0, h, 0, 0)),
            pl.BlockSpec((2, hd), lambda bi_, h: (0, h)),
            pl.BlockSpec((2, hd), lambda bi_, h: (0, h)),
        ],
        out_specs=pl.BlockSpec((None, s, hd), lambda bi_, h: (bi_, 0, h)),
        out_shape=jax.ShapeDtypeStruct((b, s, LRU_WIDTH), BF16),
        scratch_shapes=[
            pltpu.VMEM((st + 2 * SUBLANES, hd), F32),
            pltpu.VMEM((st, hd), F32),
            pltpu.VMEM((st, hd), F32),
            pltpu.VMEM((s, hd), F32),
            pltpu.VMEM((s, hd), F32),
        ],
        compiler_params=_params("parallel", "parallel"),
        name="lru_mixer",
    )(z3, z3, z_meta, cw, cb, wa, ba.reshape(2, LRU_WIDTH), wi, bi.reshape(2, LRU_WIDTH), lam)


CONF_CHUNK = 64
CONF_PARTIAL_SUMS = 2


def _conf_kernel(a_ref, b_ref, am_ref, bm_ref, cw_ref, cb_ref, g_ref, be_ref, o_ref,
                 cs_ref, sh_ref):
    s = a_ref.shape[0]
    st = s + N_META
    n_sh = sh_ref.shape[1]
    cs_ref[0:N_META, :] = am_ref[...] * _sigmoid(bm_ref[...])
    cs_ref[N_META:st, :] = a_ref[...] * _sigmoid(b_ref[...])
    cs_ref[st:st + N_META, :] = jnp.zeros((N_META, HEAD_DIM), F32)
    for r in range(1, SUBLANES):
        sh_ref[r - 1] = cs_ref[pl.ds(r, n_sh), :]

    n_chunks = s // CONF_CHUNK

    def conv_chunk(row0):
        parts = [None] * CONF_PARTIAL_SUMS
        for k in range(CONF_KERNEL):
            off = N_META - CONF_KERNEL // 2 + k
            r, q = off % SUBLANES, off // SUBLANES
            rows = pl.ds(row0 + q * SUBLANES, CONF_CHUNK)
            term = cw_ref[k:k + 1, :] * (cs_ref[rows, :] if r == 0 else sh_ref[r - 1, rows, :])
            p = k % CONF_PARTIAL_SUMS
            parts[p] = term if parts[p] is None else parts[p] + term
        return sum(parts[1:], parts[0]) + cb_ref[...]

    def step(j, carry):
        acc_prev, xc_prev = carry
        mean = jnp.mean(acc_prev, axis=-1, keepdims=True)
        var = jnp.mean(xc_prev * xc_prev, axis=-1, keepdims=True)
        acc = conv_chunk(pl.multiple_of(jnp.minimum(j, n_chunks - 1) * CONF_CHUNK, CONF_CHUNK))
        y = xc_prev * lax.rsqrt(var + LN_EPS) * g_ref[...] + be_ref[...]
        out_row = pl.multiple_of(jnp.maximum(j - 2, 0) * CONF_CHUNK, CONF_CHUNK)
        o_ref[pl.ds(out_row, CONF_CHUNK), :] = (y * _sigmoid(y)).astype(o_ref.dtype)
        return acc, acc_prev - mean

    warmup = cs_ref[pl.ds(0, CONF_CHUNK), :]
    lax.fori_loop(0, n_chunks + 2, step, (warmup, warmup))


def _conf_mixer(z3, z_meta, cw, cb, g, be):
    b, s, _ = z3.shape
    st = s + N_META
    hd = HEAD_DIM
    a_off = 2 * LRU_HEADS
    b_off = 2 * LRU_HEADS + CONF_GROUPS
    n_sh = st + N_META - SUBLANES
    vec = pl.BlockSpec((1, hd), lambda bi_, h: (0, h))
    return pl.pallas_call(
        _conf_kernel,
        grid=(b, CONF_GROUPS),
        in_specs=[
            pl.BlockSpec((None, s, hd), lambda bi_, h: (bi_, 0, a_off + h)),
            pl.BlockSpec((None, s, hd), lambda bi_, h: (bi_, 0, b_off + h)),
            pl.BlockSpec((N_META, hd), lambda bi_, h: (0, a_off + h)),
            pl.BlockSpec((N_META, hd), lambda bi_, h: (0, b_off + h)),
            pl.BlockSpec((CONF_KERNEL, hd), lambda bi_, h: (0, h)),
            vec, vec, vec,
        ],
        out_specs=pl.BlockSpec((None, s, hd), lambda bi_, h: (bi_, 0, h)),
        out_shape=jax.ShapeDtypeStruct((b, s, CONF_WIDTH), BF16),
        scratch_shapes=[
            pltpu.VMEM((st + N_META, hd), F32),
            pltpu.VMEM((SUBLANES - 1, n_sh, hd), F32),
        ],
        compiler_params=_params("parallel", "parallel"),
        name="conf_mixer",
    )(z3, z3, z_meta, z_meta, cw, cb.reshape(1, -1), g.reshape(1, -1), be.reshape(1, -1))


def _outproj_kernel(ya_ref, yb_ref, wa_ref, wb_ref, x_ref, o_ref):
    acc = jnp.dot(ya_ref[...], wa_ref[...].astype(BF16), preferred_element_type=F32)
    acc = acc + jnp.dot(yb_ref[...], wb_ref[...].astype(BF16), preferred_element_type=F32)
    o_ref[...] = x_ref[...] + acc


def _outproj(ya, yb, w, x, tm, tn):
    t, k = ya.shape
    n = w.shape[1]
    return pl.pallas_call(
        _outproj_kernel,
        grid=(t // tm, n // tn),
        in_specs=[pl.BlockSpec((tm, k), lambda i, j: (i, 0)),
                  pl.BlockSpec((tm, k), lambda i, j: (i, 0)),
                  pl.BlockSpec((k, tn), lambda i, j: (0, j)),
                  pl.BlockSpec((k, tn), lambda i, j: (1, j)),
                  pl.BlockSpec((tm, tn), lambda i, j: (i, j))],
        out_specs=pl.BlockSpec((tm, tn), lambda i, j: (i, j)),
        out_shape=jax.ShapeDtypeStruct((t, n), F32),
        compiler_params=_params("parallel", "parallel"),
        name="outproj",
    )(ya, yb, w, w, x)


def _router_kernel(x_ref, g_ref, wr_ref, br_ref, hp_ref, idx_ref, gate_ref, rank_ref,
                   cnt_ref, carry_ref):
    tt, d = x_ref.shape
    half = d // 2

    @pl.when(pl.program_id(0) == 0)
    def _():
        carry_ref[...] = jnp.zeros_like(carry_ref)

    x = x_ref[...]
    ms = jnp.mean(x * x, axis=-1, keepdims=True)
    h = x * lax.rsqrt(ms + RMS_EPS) * g_ref[...]
    hp_ref[...] = _pack_bf16_pair(h[:, :half], h[:, half:])

    h_hi = h.astype(BF16)
    h_lo = (h - h_hi.astype(F32)).astype(BF16)
    w = wr_ref[...]
    w_hi = w.astype(BF16)
    w_lo = (w - w_hi.astype(F32)).astype(BF16)
    logits = (jnp.dot(h_hi, w_hi, preferred_element_type=F32)
              + jnp.dot(h_lo, w_hi, preferred_element_type=F32)
              + jnp.dot(h_hi, w_lo, preferred_element_type=F32)) + br_ref[...]

    lane = lax.broadcasted_iota(jnp.int32, (tt, N_EXPERTS), 1)
    lane_k = lax.broadcasted_iota(jnp.int32, (tt, TOP_K), 1)
    work = logits
    vals, sels = [], []
    idx_out = jnp.zeros((tt, TOP_K), jnp.int32)
    for k in range(TOP_K):
        m = jnp.max(work, axis=1, keepdims=True)
        am = jnp.min(jnp.where(work == m, lane, N_EXPERTS), axis=1, keepdims=True)
        sel = lane == am
        vals.append(m)
        sels.append(sel)
        idx_out = jnp.where(lane_k == k, am, idx_out)
        work = jnp.where(sel, -jnp.inf, work)
    idx_ref[...] = idx_out

    exps = [jnp.exp(v - vals[0]) for v in vals]
    denom = exps[0] + exps[1] + exps[2] + exps[3]
    gate_out = jnp.zeros((tt, TOP_K), F32)
    for k in range(TOP_K):
        gate_out = jnp.where(lane_k == k, exps[k] / denom, gate_out)
    gate_ref[...] = gate_out

    onehot = jnp.zeros((tt, N_EXPERTS), F32)
    for sel in sels:
        onehot = onehot + sel.astype(F32)
    r_i = lax.broadcasted_iota(jnp.int32, (tt, tt), 0)
    c_i = lax.broadcasted_iota(jnp.int32, (tt, tt), 1)
    tri = (c_i < r_i).astype(BF16)
    before = jnp.dot(tri, onehot.astype(BF16), preferred_element_type=F32) + carry_ref[...]
    rank_out = jnp.zeros((tt, TOP_K), jnp.int32)
    for k, sel in enumerate(sels):
        rk = jnp.sum(jnp.where(sel, before, 0.0), axis=1, keepdims=True).astype(jnp.int32)
        rank_out = jnp.where(lane_k == k, rk, rank_out)
    rank_ref[...] = rank_out
    carry_ref[...] = carry_ref[...] + jnp.sum(onehot, axis=0, keepdims=True)
    cnt_ref[...] = carry_ref[...].astype(jnp.int32)


def _router(x, g, wr, br, tt):
    t, d = x.shape
    small = lambda dt: jax.ShapeDtypeStruct((t, TOP_K), dt)
    return pl.pallas_call(
        _router_kernel,
        grid=(t // tt,),
        in_specs=[pl.BlockSpec((tt, d), lambda i: (i, 0)),
                  pl.BlockSpec((1, d), lambda i: (0, 0)),
                  pl.BlockSpec((d, N_EXPERTS), lambda i: (0, 0)),
                  pl.BlockSpec((1, N_EXPERTS), lambda i: (0, 0))],
        out_specs=[pl.BlockSpec((tt, d // 2), lambda i: (i, 0)),
                   pl.BlockSpec((tt, TOP_K), lambda i: (i, 0)),
                   pl.BlockSpec((tt, TOP_K), lambda i: (i, 0)),
                   pl.BlockSpec((tt, TOP_K), lambda i: (i, 0)),
                   pl.BlockSpec((1, N_EXPERTS), lambda i: (0, 0))],
        out_shape=[jax.ShapeDtypeStruct((t, d // 2), U32),
                   small(jnp.int32), small(F32), small(jnp.int32),
                   jax.ShapeDtypeStruct((1, N_EXPERTS), jnp.int32)],
        scratch_shapes=[pltpu.VMEM((1, N_EXPERTS), F32)],
        compiler_params=_params("arbitrary"),
        name="router",
    )(x, g.reshape(1, d), wr, br.reshape(1, N_EXPERTS))


def _num_passes(n_assign):
    return N_EXPERTS + n_assign // PASS_ROWS


def _routing_tables(idx, rank, counts, n_pass_max):
    t = idx.shape[0]
    padded = (counts + ROW_BLOCK - 1) // ROW_BLOCK * ROW_BLOCK
    passes_e = (padded + PASS_ROWS - 1) // PASS_ROWS
    pass_end = jnp.cumsum(passes_e)
    pass_start = pass_end - passes_e
    n_pass = pass_end[-1]
    pos = (pass_start[idx] + rank // PASS_ROWS) * PASS_ROWS + rank % PASS_ROWS
    p_ids = jnp.arange(n_pass_max, dtype=jnp.int32)
    live = p_ids < n_pass
    pe = jnp.searchsorted(pass_end, jnp.minimum(p_ids, n_pass - 1), side="right")
    pe = jnp.minimum(pe, N_EXPERTS - 1).astype(jnp.int32)
    rows = jnp.clip(padded[pe] - (p_ids - pass_start[pe]) * PASS_ROWS, 0, PASS_ROWS)
    pass_nb = jnp.where(live, rows // ROW_BLOCK, 0).astype(jnp.int32)
    tok = jnp.repeat(jnp.arange(t, dtype=jnp.int32), TOP_K)
    tok_idx = jnp.zeros((n_pass_max * PASS_ROWS,), jnp.int32).at[pos.reshape(-1)].set(tok)
    return (pe, pass_nb, n_pass.reshape(1).astype(jnp.int32), tok_idx,
            pos.reshape(-1).astype(jnp.int32))


GATHER_UNROLL = 8


def _gather_rows(src_hbm, idx_ref, idx_base, n_rows, dst_ref, sem):
    def issue(g, _):
        for u in range(GATHER_UNROLL):
            r = g * GATHER_UNROLL + u
            pltpu.make_async_copy(src_hbm.at[pl.ds(idx_ref[idx_base + r], 1)],
                                  dst_ref.at[pl.ds(r, 1)], sem).start()
        return 0

    lax.fori_loop(0, n_rows // GATHER_UNROLL, issue, 0)


def _wait_row_blocks(src_hbm, dst_ref, n_blocks, rows, sem):
    def drain(i, _):
        pltpu.make_async_copy(src_hbm.at[pl.ds(0, rows)], dst_ref.at[pl.ds(0, rows)], sem).wait()
        return 0

    lax.fori_loop(0, n_blocks, drain, 0)


def _moe_up_kernel(pe_ref, nb_ref, npass_ref, tok_ref, hp_hbm, wg_ref, wu_ref, bg_ref, bu_ref,
                   o_ref, gbuf_ref, xb_ref, sem):
    p = pl.program_id(0)
    f = pl.program_id(1)
    cap, half = gbuf_ref.shape
    n_pass = npass_ref[0]
    live = p < n_pass
    nb = nb_ref[p]

    @pl.when(jnp.logical_and(live, f == 0))
    def _():
        @pl.when(p == 0)
        def _():
            _gather_rows(hp_hbm, tok_ref, 0, nb * ROW_BLOCK, gbuf_ref, sem)

        _wait_row_blocks(hp_hbm, gbuf_ref, nb, ROW_BLOCK, sem)

        def unpack(rb, _):
            rows = pl.ds(pl.multiple_of(rb * ROW_BLOCK, ROW_BLOCK), ROW_BLOCK)
            hi, lo = _unpack_bf16_pair(gbuf_ref[rows, :])
            xb_ref[rows, :half] = hi.astype(BF16)
            xb_ref[rows, half:] = lo.astype(BF16)
            return 0

        lax.fori_loop(0, nb, unpack, 0)

        @pl.when(p + 1 < n_pass)
        def _():
            _gather_rows(hp_hbm, tok_ref, (p + 1) * cap, nb_ref[p + 1] * ROW_BLOCK, gbuf_ref, sem)

    @pl.when(live)
    def _():
        def block(rb, _):
            rows = pl.ds(pl.multiple_of(rb * ROW_BLOCK, ROW_BLOCK), ROW_BLOCK)
            x = xb_ref[rows, :]
            hg = jnp.dot(x, wg_ref[...].astype(BF16), preferred_element_type=F32) + bg_ref[...]
            hu = jnp.dot(x, wu_ref[...].astype(BF16), preferred_element_type=F32) + bu_ref[...]
            hg = jnp.minimum(hg, SWIGLU_LIMIT)
            hu = jnp.clip(hu, -SWIGLU_LIMIT, SWIGLU_LIMIT)
            act = hg * _sigmoid(SWIGLU_ALPHA * hg) * (hu + 1.0)
            o_ref[rows, :] = act.astype(o_ref.dtype)
            return 0

        lax.fori_loop(0, nb, block, 0)


def _moe_up(pe, pass_nb, n_pass, tok_idx, hp, wg, wu, bg, bu, n_pass_max, tf):
    e, d, f = wg.shape
    nf = f // tf
    cap = PASS_ROWS

    def w_map(p, j, pe_, nb_, np_, tok):
        return (pe_[p], 0, jnp.where(p < np_[0], j, nf - 1))

    def o_map(p, j, pe_, nb_, np_, tok):
        ok = p < np_[0]
        return (jnp.where(ok, p, np_[0] - 1), jnp.where(ok, j, nf - 1))

    grid_spec = pltpu.PrefetchScalarGridSpec(
        num_scalar_prefetch=4,
        grid=(n_pass_max, nf),
        in_specs=[pl.BlockSpec(memory_space=pl.ANY),
                  pl.BlockSpec((None, d, tf), w_map),
                  pl.BlockSpec((None, d, tf), w_map),
                  pl.BlockSpec((None, 1, tf), w_map),
                  pl.BlockSpec((None, 1, tf), w_map)],
        out_specs=pl.BlockSpec((cap, tf), o_map),
        scratch_shapes=[pltpu.VMEM((cap, d // 2), U32),
                        pltpu.VMEM((cap, d), BF16),
                        pltpu.SemaphoreType.DMA(())],
    )
    return pl.pallas_call(
        _moe_up_kernel,
        grid_spec=grid_spec,
        out_shape=jax.ShapeDtypeStruct((n_pass_max * cap, f), BF16),
        compiler_params=_params("arbitrary", "arbitrary"),
        name="moe_up",
    )(pe, pass_nb, n_pass, tok_idx, hp, wg, wu, bg.reshape(e, 1, f), bu.reshape(e, 1, f))


def _moe_down_kernel(pe_ref, nb_ref, npass_ref, a_ref, wh_ref, wl_ref, bh_ref, bl_ref, o_ref):
    p = pl.program_id(0)

    @pl.when(p < npass_ref[0])
    def _():
        def block(rb, _):
            rows = pl.ds(pl.multiple_of(rb * ROW_BLOCK, ROW_BLOCK), ROW_BLOCK)
            a = a_ref[rows, :]
            hi = jnp.dot(a, wh_ref[...].astype(BF16), preferred_element_type=F32) + bh_ref[...]
            lo = jnp.dot(a, wl_ref[...].astype(BF16), preferred_element_type=F32) + bl_ref[...]
            o_ref[rows, :] = _pack_bf16_pair(hi, lo)
            return 0

        lax.fori_loop(0, nb_ref[p], block, 0)


def _moe_down(pe, pass_nb, n_pass, act, wd, bd, n_pass_max, td):
    e, f, d = wd.shape
    nd = d // 2 // td
    cap = PASS_ROWS

    def col(p, j, np_):
        return jnp.where(p < np_[0], j, nd - 1)

    def row(p, np_):
        return jnp.where(p < np_[0], p, np_[0] - 1)

    grid_spec = pltpu.PrefetchScalarGridSpec(
        num_scalar_prefetch=3,
        grid=(n_pass_max, nd),
        in_specs=[pl.BlockSpec((cap, f), lambda p, j, pe_, nb_, np_: (row(p, np_), 0)),
                  pl.BlockSpec((None, f, td), lambda p, j, pe_, nb_, np_: (pe_[p], 0, col(p, j, np_))),
                  pl.BlockSpec((None, f, td), lambda p, j, pe_, nb_, np_: (pe_[p], 0, nd + col(p, j, np_))),
                  pl.BlockSpec((None, 1, td), lambda p, j, pe_, nb_, np_: (pe_[p], 0, col(p, j, np_))),
                  pl.BlockSpec((None, 1, td), lambda p, j, pe_, nb_, np_: (pe_[p], 0, nd + col(p, j, np_)))],
        out_specs=pl.BlockSpec((cap, td), lambda p, j, pe_, nb_, np_: (row(p, np_), col(p, j, np_))),
    )
    bd3 = bd.reshape(e, 1, d)
    return pl.pallas_call(
        _moe_down_kernel,
        grid_spec=grid_spec,
        out_shape=jax.ShapeDtypeStruct((n_pass_max * cap, d // 2), U32),
        compiler_params=_params("arbitrary", "arbitrary"),
        name="moe_down",
    )(pe, pass_nb, n_pass, act, wd, wd, bd3, bd3)


def _combine_kernel(pos_ref, y_hbm, x_ref, gate_ref, g_ref, o_ref, ybuf_ref, sems):
    i = pl.program_id(0)
    n = pl.num_programs(0)
    tt, d = x_ref.shape
    half = d // 2
    slot = i % 2

    def start_tile(step, slot_):
        for k in range(TOP_K):
            def issue(g, _):
                for u in range(GATHER_UNROLL):
                    r = g * GATHER_UNROLL + u
                    pltpu.make_async_copy(
                        y_hbm.at[pl.ds(pos_ref[(step * tt + r) * TOP_K + k], 1)],
                        ybuf_ref.at[slot_, k, pl.ds(r, 1)], sems.at[slot_]).start()
                return 0

            lax.fori_loop(0, tt // GATHER_UNROLL, issue, 0)

    @pl.when(i == 0)
    def _():
        start_tile(0, 0)

    @pl.when(i + 1 < n)
    def _():
        start_tile(i + 1, 1 - slot)

    for k in range(TOP_K):
        pltpu.make_async_copy(y_hbm.at[pl.ds(0, tt)], ybuf_ref.at[slot, k], sems.at[slot]).wait()

    gates = gate_ref[...]
    x = x_ref[...]
    acc_hi = x[:, :half]
    acc_lo = x[:, half:]
    for k in range(TOP_K):
        hi, lo = _unpack_bf16_pair(ybuf_ref[slot, k])
        acc_hi = acc_hi + gates[:, k:k + 1] * hi
        acc_lo = acc_lo + gates[:, k:k + 1] * lo
    ms = (jnp.sum(acc_hi * acc_hi, axis=-1, keepdims=True)
          + jnp.sum(acc_lo * acc_lo, axis=-1, keepdims=True)) * (1.0 / d)
    scale = lax.rsqrt(ms + RMS_EPS)
    o_ref[:, :half] = acc_hi * scale * g_ref[:, :half]
    o_ref[:, half:] = acc_lo * scale * g_ref[:, half:]


def _combine(pos_flat, y, x, gates, g, tt):
    t, d = x.shape
    grid_spec = pltpu.PrefetchScalarGridSpec(
        num_scalar_prefetch=1,
        grid=(t // tt,),
        in_specs=[pl.BlockSpec(memory_space=pl.ANY),
                  pl.BlockSpec((tt, d), lambda i, p: (i, 0)),
                  pl.BlockSpec((tt, TOP_K), lambda i, p: (i, 0)),
                  pl.BlockSpec((1, d), lambda i, p: (0, 0))],
        out_specs=pl.BlockSpec((tt, d), lambda i, p: (i, 0)),
        scratch_shapes=[pltpu.VMEM((2, TOP_K, tt, d // 2), U32),
                        pltpu.SemaphoreType.DMA((2,))],
    )
    return pl.pallas_call(
        _combine_kernel,
        grid_spec=grid_spec,
        out_shape=jax.ShapeDtypeStruct((t, d), F32),
        compiler_params=_params("arbitrary"),
        name="combine",
    )(pos_flat, y, x, gates, g.reshape(1, d))


def _moe_and_final_norm(x_mid, norm2_g, w_router, b_router, w_gate, b_gate, w_up, b_up,
                        w_down, b_down, final_norm_g, tt_router, tf, td, tt_combine):
    t = x_mid.shape[0]
    n_pass_max = _num_passes(t * TOP_K)
    hp, idx, gates, rank, counts = _router(x_mid, norm2_g, w_router, b_router, tt=tt_router)
    pe, pass_nb, n_pass, tok_idx, pos_flat = _routing_tables(idx, rank, counts[0], n_pass_max)
    act = _moe_up(pe, pass_nb, n_pass, tok_idx, hp, w_gate, w_up, b_gate, b_up, n_pass_max, tf=tf)
    y_rows = _moe_down(pe, pass_nb, n_pass, act, w_down, b_down, n_pass_max, td=td)
    return _combine(pos_flat, y_rows, x_mid, gates, final_norm_g, tt=tt_combine)


def kernel(x, meta_tokens, norm1_g, w_in, lru_conv_w, lru_conv_b, lru_w_a, lru_b_a, lru_w_i, lru_b_i, lru_lambda, conf_conv_w, conf_conv_b, conf_norm_g, conf_norm_b, w_out, norm2_g, w_router, b_router, w_gate, b_gate, w_up, b_up, w_down, b_down, final_norm_g):
    b, s, d = x.shape
    t = b * s
    x2 = x.reshape(t, d)

    h = _rmsnorm(x2, norm1_g[0], tm=512)
    h_meta = _rmsnorm(meta_tokens.astype(x.dtype), norm1_g[0], tm=N_META)
    z = _inproj(h, w_in[0], tm=1024, tn=512)
    z_meta = _inproj(h_meta, w_in[0], tm=N_META, tn=512)
    z3 = z.reshape(b, s, -1)

    y_lru = _lru_mixer(z3, z_meta, lru_conv_w[0], lru_conv_b[0], lru_w_a[0], lru_b_a[0],
                       lru_w_i[0], lru_b_i[0], lru_lambda[0])
    y_conf = _conf_mixer(z3, z_meta, conf_conv_w[0], conf_conv_b[0], conf_norm_g[0],
                         conf_norm_b[0])
    x_mid = _outproj(y_lru.reshape(t, -1), y_conf.reshape(t, -1), w_out[0], x2,
                     tm=1024, tn=512)

    out = _moe_and_final_norm(x_mid, norm2_g[0], w_router[0], b_router[0], w_gate[0], b_gate[0],
                              w_up[0], b_up[0], w_down[0], b_down[0], final_norm_g,
                              tt_router=512, tf=256, td=256, tt_combine=256)
    return out.reshape(b, s, d)
```

```python
import jax
import jax.numpy as jnp
from jax import lax
from jax.experimental import pallas as pl
from jax.experimental.pallas import tpu as pltpu

N_META = 16
LRU_WIDTH = 2048
LRU_HEADS = 16
HEAD_DIM = 128
LRU_CONV_WIDTH = 4
LRU_C = 8.0
CONF_WIDTH = 2048
CONF_GROUPS = 16
CONF_KERNEL = 31
N_EXPERTS = 32
TOP_K = 4
SWIGLU_ALPHA = 1.702
SWIGLU_LIMIT = 7.0
RMS_EPS = 1e-5
LN_EPS = 1e-5
SQRT_FLOOR = 1e-30

SUBLANES = 8
VMEM_LIMIT = 56 * 1024 * 1024
VMEM_LIMIT_LARGE = 61 * 1024 * 1024

ROW_BLOCK = 128
MATMUL_ROWS = 512
PASS_ROWS = 1536
BF16 = jnp.bfloat16
F32 = jnp.float32
U32 = jnp.uint32


def _params(*sem, vmem_limit=VMEM_LIMIT):
    return pltpu.CompilerParams(dimension_semantics=sem, vmem_limit_bytes=vmem_limit)


def _pack_bf16_pair(hi, lo):
    hi_bits = lax.bitcast_convert_type(hi.astype(BF16).astype(F32), U32)
    lo_bits = lax.bitcast_convert_type(lo.astype(BF16).astype(F32), U32)
    return hi_bits | lax.shift_right_logical(lo_bits, jnp.uint32(16))


def _sigmoid(x):
    return 0.5 * jnp.tanh(0.5 * x) + 0.5


def _unpack_bf16_pair(u):
    hi = lax.bitcast_convert_type(u & jnp.uint32(0xFFFF0000), F32)
    lo = lax.bitcast_convert_type(lax.shift_left(u, jnp.uint32(16)), F32)
    return hi, lo


def _rmsnorm_kernel(x_ref, g_ref, o_ref):
    x = x_ref[...]
    ms = jnp.mean(x * x, axis=-1, keepdims=True)
    o_ref[...] = (x * lax.rsqrt(ms + RMS_EPS) * g_ref[...]).astype(o_ref.dtype)


def _rmsnorm(x, g, tm):
    t, d = x.shape
    return pl.pallas_call(
        _rmsnorm_kernel,
        grid=(t // tm,),
        in_specs=[pl.BlockSpec((tm, d), lambda i: (i, 0)),
                  pl.BlockSpec((1, d), lambda i: (0, 0))],
        out_specs=pl.BlockSpec((tm, d), lambda i: (i, 0)),
        out_shape=jax.ShapeDtypeStruct((t, d), BF16),
        compiler_params=_params("parallel"),
        name="rmsnorm1",
    )(x, g.reshape(1, d))


def _inproj_kernel(a_ref, am_ref, w_ref, o_ref, om_ref, wb_ref):
    @pl.when(pl.program_id(1) == 0)
    def _():
        wb_ref[...] = w_ref[...].astype(BF16)
        om_ref[...] = jnp.dot(am_ref[...], wb_ref[...], preferred_element_type=F32)

    o_ref[...] = jnp.dot(a_ref[...], wb_ref[...], preferred_element_type=F32)


def _inproj(a, a_meta, w, tm, tn):
    t, k = a.shape
    n = w.shape[1]
    return pl.pallas_call(
        _inproj_kernel,
        grid=(n // tn, t // tm),
        in_specs=[pl.BlockSpec((tm, k), lambda j, i: (i, 0)),
                  pl.BlockSpec((N_META, k), lambda j, i: (0, 0)),
                  pl.BlockSpec((k, tn), lambda j, i: (0, j))],
        out_specs=[pl.BlockSpec((tm, tn), lambda j, i: (i, j)),
                   pl.BlockSpec((N_META, tn), lambda j, i: (0, j))],
        out_shape=[jax.ShapeDtypeStruct((t, n), F32),
                   jax.ShapeDtypeStruct((N_META, n), F32)],
        scratch_shapes=[pltpu.VMEM((k, tn), BF16)],
        compiler_params=_params("parallel", "arbitrary"),
        name="inproj",
    )(a, a_meta, w)


def _tile_scan(a, b, reverse):
    n = a.shape[0]
    row = lax.broadcasted_iota(jnp.int32, a.shape, 0) & (SUBLANES - 1)
    for s in (1, 2, 4):
        if reverse:
            a_sh = pltpu.roll(a, n - s, 0)
            b_sh = pltpu.roll(b, n - s, 0)
            m = row < SUBLANES - s
        else:
            a_sh = pltpu.roll(a, s, 0)
            b_sh = pltpu.roll(b, s, 0)
            m = row >= s
        b = jnp.where(m, a * b_sh + b, b)
        a = jnp.where(m, a * a_sh, a)
    return a, b


def _gelu_tanh(x):
    return 0.5 * x * (1.0 + jnp.tanh(0.7978845608028654 * (x + 0.044715 * x * x * x)))


def _lru_kernel(xr_ref, gate_ref, xm_ref, cw_ref, cb_ref, wa_ref, ba_ref, wi_ref,
                bi_ref, lam_ref, o_ref, sf_ref, af_ref, bf_ref, ab_ref, bb_ref):
    s = xr_ref.shape[0]
    st = s + N_META
    pad = SUBLANES
    zeros8 = jnp.zeros((pad, HEAD_DIM), F32)
    sf_ref[0:pad, :] = zeros8
    sf_ref[pad:pad + N_META, :] = xm_ref[...]
    sf_ref[pad + N_META:pad + st, :] = xr_ref[...]
    sf_ref[pad + st:pad + st + pad, :] = zeros8

    def gates(u, d):
        ub = u.astype(BF16)
        r = _sigmoid(jnp.dot(ub, wa_ref[d].astype(BF16), preferred_element_type=F32)
                           + ba_ref[d:d + 1, :])
        i = _sigmoid(jnp.dot(ub, wi_ref[d].astype(BF16), preferred_element_type=F32)
                           + bi_ref[d:d + 1, :])
        lam = lam_ref[d:d + 1, :]
        softplus_neg = jnp.maximum(-lam, 0.0) + jnp.log1p(jnp.exp(-jnp.abs(lam)))
        log_a = (-LRU_C) * r * softplus_neg
        a = jnp.exp(log_a)
        one_minus_a2 = -jnp.tanh(log_a) * (a * a + 1.0)
        root = one_minus_a2 * lax.rsqrt(jnp.maximum(one_minus_a2, SQRT_FLOOR))
        b = root * (i * u)
        return a, b

    uf = cb_ref[0:1, :] + cw_ref[0, 3:4, :] * sf_ref[pl.ds(pad, st), :]
    for j in range(1, LRU_CONV_WIDTH):
        uf = uf + cw_ref[0, 3 - j:4 - j, :] * sf_ref[pl.ds(pad - j, st), :]
    a, b = gates(uf, 0)
    a, b = _tile_scan(a, b, reverse=False)
    af_ref[...] = a
    bf_ref[...] = b

    base = pad + N_META
    ub_ = cb_ref[1:2, :] + cw_ref[1, 3:4, :] * sf_ref[pl.ds(base, s), :]
    for j in range(1, LRU_CONV_WIDTH):
        ub_ = ub_ + cw_ref[1, 3 - j:4 - j, :] * sf_ref[pl.ds(base + j, s), :]
    a, b = gates(ub_, 1)
    a, b = _tile_scan(a, b, reverse=True)
    ab_ref[...] = a
    bb_ref[...] = b

    def fwd_tile(j, c):
        sl = pl.ds(pl.multiple_of(j * SUBLANES, SUBLANES), SUBLANES)
        h = af_ref[sl, :] * c + bf_ref[sl, :]
        bf_ref[sl, :] = h
        return h[SUBLANES - 1:SUBLANES, :]

    c0 = jnp.zeros((1, HEAD_DIM), F32)
    cf = lax.fori_loop(0, N_META // SUBLANES, fwd_tile, c0)
    n_tiles = s // SUBLANES

    def both_tiles(j, carry):
        cf, cb = carry
        slf = pl.ds(pl.multiple_of(N_META + j * SUBLANES, SUBLANES), SUBLANES)
        hf = af_ref[slf, :] * cf + bf_ref[slf, :]
        bf_ref[slf, :] = hf
        slb = pl.ds(pl.multiple_of((n_tiles - 1 - j) * SUBLANES, SUBLANES), SUBLANES)
        hb = ab_ref[slb, :] * cb + bb_ref[slb, :]
        bb_ref[slb, :] = hb
        return hf[SUBLANES - 1:SUBLANES, :], hb[0:1, :]

    lax.fori_loop(0, n_tiles, both_tiles, (cf, c0), unroll=4)
    y = (bf_ref[pl.ds(N_META, s), :] + bb_ref[...]) * _gelu_tanh(gate_ref[...])
    o_ref[...] = y.astype(o_ref.dtype)


def _lru_mixer(z3, z_meta, cw, cb, wa, ba, wi, bi, lam):
    b, s, _ = z3.shape
    st = s + N_META
    hd = HEAD_DIM
    col = lambda off: (lambda bi_, h: (bi_, 0, off + h))
    return pl.pallas_call(
        _lru_kernel,
        grid=(b, LRU_HEADS),
        in_specs=[
            pl.BlockSpec((None, s, hd), col(0)),
            pl.BlockSpec((None, s, hd), col(LRU_HEADS)),
            pl.BlockSpec((N_META, hd), lambda bi_, h: (0, h)),
            pl.BlockSpec((2, LRU_CONV_WIDTH, hd), lambda bi_, h: (0, 0, h)),
            pl.BlockSpec((2, hd), lambda bi_, h: (0, h)),
            pl.BlockSpec((2, None, hd, hd), lambda bi_, h: (0, h, 0, 0)),
            pl.BlockSpec((2, hd), lambda bi_, h: (0, h)),
            pl.BlockSpec((2, None, hd, hd), lambda bi_, h: (0, h, 0, 0)),
            pl.BlockSpec((2, hd), lambda bi_, h: (0, h)),
            pl.BlockSpec((2, hd), lambda bi_, h: (0, h)),
        ],
        out_specs=pl.BlockSpec((None, s, hd), lambda bi_, h: (bi_, 0, h)),
        out_shape=jax.ShapeDtypeStruct((b, s, LRU_WIDTH), BF16),
        scratch_shapes=[
            pltpu.VMEM((st + 2 * SUBLANES, hd), F32),
            pltpu.VMEM((st, hd), F32),
            pltpu.VMEM((st, hd), F32),
            pltpu.VMEM((s, hd), F32),
            pltpu.VMEM((s, hd), F32),
        ],
        compiler_params=_params("parallel", "parallel"),
        name="lru_mixer",
    )(z3, z3, z_meta, cw, cb, wa, ba.reshape(2, LRU_WIDTH), wi, bi.reshape(2, LRU_WIDTH), lam)


CONF_CHUNK = 64
CONF_PARTIAL_SUMS = 2


def _conf_kernel(a_ref, b_ref, am_ref, bm_ref, cw_ref, cb_ref, g_ref, be_ref, o_ref,
                 cs_ref, sh_ref):
    s = a_ref.shape[0]
    st = s + N_META
    n_sh = sh_ref.shape[1]
    cs_ref[0:N_META, :] = am_ref[...] * _sigmoid(bm_ref[...])
    cs_ref[N_META:st, :] = a_ref[...] * _sigmoid(b_ref[...])
    cs_ref[st:st + N_META, :] = jnp.zeros((N_META, HEAD_DIM), F32)
    for r in range(1, SUBLANES):
        sh_ref[r - 1] = cs_ref[pl.ds(r, n_sh), :]

    n_chunks = s // CONF_CHUNK

    def conv_chunk(row0):
        parts = [None] * CONF_PARTIAL_SUMS
        for k in range(CONF_KERNEL):
            off = N_META - CONF_KERNEL // 2 + k
            r, q = off % SUBLANES, off // SUBLANES
            rows = pl.ds(row0 + q * SUBLANES, CONF_CHUNK)
            term = cw_ref[k:k + 1, :] * (cs_ref[rows, :] if r == 0 else sh_ref[r - 1, rows, :])
            p = k % CONF_PARTIAL_SUMS
            parts[p] = term if parts[p] is None else parts[p] + term
        return sum(parts[1:], parts[0]) + cb_ref[...]

    def step(j, carry):
        acc_prev, xc_prev = carry
        mean = jnp.mean(acc_prev, axis=-1, keepdims=True)
        var = jnp.mean(xc_prev * xc_prev, axis=-1, keepdims=True)
        acc = conv_chunk(pl.multiple_of(jnp.minimum(j, n_chunks - 1) * CONF_CHUNK, CONF_CHUNK))
        y = xc_prev * lax.rsqrt(var + LN_EPS) * g_ref[...] + be_ref[...]
        out_row = pl.multiple_of(jnp.maximum(j - 2, 0) * CONF_CHUNK, CONF_CHUNK)
        o_ref[pl.ds(out_row, CONF_CHUNK), :] = (y * _sigmoid(y)).astype(o_ref.dtype)
        return acc, acc_prev - mean

    warmup = cs_ref[pl.ds(0, CONF_CHUNK), :]
    lax.fori_loop(0, n_chunks + 2, step, (warmup, warmup))


def _conf_mixer(z3, z_meta, cw, cb, g, be):
    b, s, _ = z3.shape
    st = s + N_META
    hd = HEAD_DIM
    a_off = 2 * LRU_HEADS
    b_off = 2 * LRU_HEADS + CONF_GROUPS
    n_sh = st + N_META - SUBLANES
    vec = pl.BlockSpec((1, hd), lambda bi_, h: (0, h))
    return pl.pallas_call(
        _conf_kernel,
        grid=(b, CONF_GROUPS),
        in_specs=[
            pl.BlockSpec((None, s, hd), lambda bi_, h: (bi_, 0, a_off + h)),
            pl.BlockSpec((None, s, hd), lambda bi_, h: (bi_, 0, b_off + h)),
            pl.BlockSpec((N_META, hd), lambda bi_, h: (0, a_off + h)),
            pl.BlockSpec((N_META, hd), lambda bi_, h: (0, b_off + h)),
            pl.BlockSpec((CONF_KERNEL, hd), lambda bi_, h: (0, h)),
            vec, vec, vec,
        ],
        out_specs=pl.BlockSpec((None, s, hd), lambda bi_, h: (bi_, 0, h)),
        out_shape=jax.ShapeDtypeStruct((b, s, CONF_WIDTH), BF16),
        scratch_shapes=[
            pltpu.VMEM((st + N_META, hd), F32),
            pltpu.VMEM((SUBLANES - 1, n_sh, hd), F32),
        ],
        compiler_params=_params("parallel", "parallel"),
        name="conf_mixer",
    )(z3, z3, z_meta, z_meta, cw, cb.reshape(1, -1), g.reshape(1, -1), be.reshape(1, -1))


def _outproj_kernel(ya_ref, yb_ref, wa_ref, wb_ref, x_ref, o_ref):
    acc = jnp.dot(ya_ref[...], wa_ref[...].astype(BF16), preferred_element_type=F32)
    acc = acc + jnp.dot(yb_ref[...], wb_ref[...].astype(BF16), preferred_element_type=F32)
    o_ref[...] = x_ref[...] + acc


def _outproj(ya, yb, w, x, tm, tn):
    t, k = ya.shape
    n = w.shape[1]
    return pl.pallas_call(
        _outproj_kernel,
        grid=(t // tm, n // tn),
        in_specs=[pl.BlockSpec((tm, k), lambda i, j: (i, 0)),
                  pl.BlockSpec((tm, k), lambda i, j: (i, 0)),
                  pl.BlockSpec((k, tn), lambda i, j: (0, j)),
                  pl.BlockSpec((k, tn), lambda i, j: (1, j)),
                  pl.BlockSpec((tm, tn), lambda i, j: (i, j))],
        out_specs=pl.BlockSpec((tm, tn), lambda i, j: (i, j)),
        out_shape=jax.ShapeDtypeStruct((t, n), F32),
        compiler_params=_params("parallel", "parallel", vmem_limit=VMEM_LIMIT_LARGE),
        name="outproj",
    )(ya, yb, w, w, x)


def _router_kernel(x_ref, g_ref, wr_ref, br_ref, hp_ref, idx_ref, gate_ref, rank_ref,
                   cnt_ref, carry_ref):
    tt, d = x_ref.shape
    half = d // 2

    @pl.when(pl.program_id(0) == 0)
    def _():
        carry_ref[...] = jnp.zeros_like(carry_ref)

    x = x_ref[...]
    ms = jnp.mean(x * x, axis=-1, keepdims=True)
    h = x * lax.rsqrt(ms + RMS_EPS) * g_ref[...]
    hp_ref[...] = _pack_bf16_pair(h[:, :half], h[:, half:])

    h_hi = h.astype(BF16)
    h_lo = (h - h_hi.astype(F32)).astype(BF16)
    w = wr_ref[...]
    w_hi = w.astype(BF16)
    w_lo = (w - w_hi.astype(F32)).astype(BF16)
    logits = (jnp.dot(h_hi, w_hi, preferred_element_type=F32)
              + jnp.dot(h_lo, w_hi, preferred_element_type=F32)
              + jnp.dot(h_hi, w_lo, preferred_element_type=F32)) + br_ref[...]

    lane = lax.broadcasted_iota(jnp.int32, (tt, N_EXPERTS), 1)
    lane_k = lax.broadcasted_iota(jnp.int32, (tt, TOP_K), 1)
    work = logits
    vals, sels = [], []
    idx_out = jnp.zeros((tt, TOP_K), jnp.int32)
    for k in range(TOP_K):
        m = jnp.max(work, axis=1, keepdims=True)
        am = jnp.min(jnp.where(work == m, lane, N_EXPERTS), axis=1, keepdims=True)
        sel = lane == am
        vals.append(m)
        sels.append(sel)
        idx_out = jnp.where(lane_k == k, am, idx_out)
        work = jnp.where(sel, -jnp.inf, work)
    idx_ref[...] = idx_out

    exps = [jnp.exp(v - vals[0]) for v in vals]
    denom = exps[0] + exps[1] + exps[2] + exps[3]
    gate_out = jnp.zeros((tt, TOP_K), F32)
    for k in range(TOP_K):
        gate_out = jnp.where(lane_k == k, exps[k] / denom, gate_out)
    gate_ref[...] = gate_out

    onehot = jnp.zeros((tt, N_EXPERTS), F32)
    for sel in sels:
        onehot = onehot + sel.astype(F32)
    r_i = lax.broadcasted_iota(jnp.int32, (tt, tt), 0)
    c_i = lax.broadcasted_iota(jnp.int32, (tt, tt), 1)
    tri = (c_i < r_i).astype(BF16)
    before = jnp.dot(tri, onehot.astype(BF16), preferred_element_type=F32) + carry_ref[...]
    rank_out = jnp.zeros((tt, TOP_K), jnp.int32)
    for k, sel in enumerate(sels):
        rk = jnp.sum(jnp.where(sel, before, 0.0), axis=1, keepdims=True).astype(jnp.int32)
        rank_out = jnp.where(lane_k == k, rk, rank_out)
    rank_ref[...] = rank_out
    carry_ref[...] = carry_ref[...] + jnp.sum(onehot, axis=0, keepdims=True)
    cnt_ref[...] = carry_ref[...].astype(jnp.int32)


def _router(x, g, wr, br, tt):
    t, d = x.shape
    small = lambda dt: jax.ShapeDtypeStruct((t, TOP_K), dt)
    return pl.pallas_call(
        _router_kernel,
        grid=(t // tt,),
        in_specs=[pl.BlockSpec((tt, d), lambda i: (i, 0)),
                  pl.BlockSpec((1, d), lambda i: (0, 0)),
                  pl.BlockSpec((d, N_EXPERTS), lambda i: (0, 0)),
                  pl.BlockSpec((1, N_EXPERTS), lambda i: (0, 0))],
        out_specs=[pl.BlockSpec((tt, d // 2), lambda i: (i, 0)),
                   pl.BlockSpec((tt, TOP_K), lambda i: (i, 0)),
                   pl.BlockSpec((tt, TOP_K), lambda i: (i, 0)),
                   pl.BlockSpec((tt, TOP_K), lambda i: (i, 0)),
                   pl.BlockSpec((1, N_EXPERTS), lambda i: (0, 0))],
        out_shape=[jax.ShapeDtypeStruct((t, d // 2), U32),
                   small(jnp.int32), small(F32), small(jnp.int32),
                   jax.ShapeDtypeStruct((1, N_EXPERTS), jnp.int32)],
        scratch_shapes=[pltpu.VMEM((1, N_EXPERTS), F32)],
        compiler_params=_params("arbitrary"),
        name="router",
    )(x, g.reshape(1, d), wr, br.reshape(1, N_EXPERTS))


def _num_passes(n_assign):
    return N_EXPERTS + n_assign // PASS_ROWS


def _routing_tables(idx, rank, counts, n_pass_max):
    t = idx.shape[0]
    padded = (counts + ROW_BLOCK - 1) // ROW_BLOCK * ROW_BLOCK
    passes_e = (padded + PASS_ROWS - 1) // PASS_ROWS
    pass_end = jnp.cumsum(passes_e)
    pass_start = pass_end - passes_e
    n_pass = pass_end[-1]
    pos = (pass_start[idx] + rank // PASS_ROWS) * PASS_ROWS + rank % PASS_ROWS
    p_ids = jnp.arange(n_pass_max, dtype=jnp.int32)
    live = p_ids < n_pass
    pe = jnp.searchsorted(pass_end, jnp.minimum(p_ids, n_pass - 1), side="right")
    pe = jnp.minimum(pe, N_EXPERTS - 1).astype(jnp.int32)
    rows = jnp.clip(padded[pe] - (p_ids - pass_start[pe]) * PASS_ROWS, 0, PASS_ROWS)
    pass_nb = jnp.where(live, rows // ROW_BLOCK, 0).astype(jnp.int32)
    tok = jnp.repeat(jnp.arange(t, dtype=jnp.int32), TOP_K)
    tok_idx = jnp.zeros((n_pass_max * PASS_ROWS,), jnp.int32).at[pos.reshape(-1)].set(
        tok, unique_indices=True, mode="promise_in_bounds")
    return (pe, pass_nb, n_pass.reshape(1).astype(jnp.int32), tok_idx,
            pos.reshape(-1).astype(jnp.int32))


GATHER_UNROLL = 8


def _gather_rows(src_hbm, idx_ref, idx_base, n_rows, dst_ref, sem):
    def issue(g, _):
        for u in range(GATHER_UNROLL):
            r = g * GATHER_UNROLL + u
            pltpu.make_async_copy(src_hbm.at[pl.ds(idx_ref[idx_base + r], 1)],
                                  dst_ref.at[pl.ds(r, 1)], sem).start()
        return 0

    lax.fori_loop(0, n_rows // GATHER_UNROLL, issue, 0)


def _for_row_blocks(n_blocks, body):
    per_big = MATMUL_ROWS // ROW_BLOCK
    n_big = n_blocks // per_big

    def big(i, _):
        body(pl.multiple_of(i * MATMUL_ROWS, MATMUL_ROWS), MATMUL_ROWS)
        return 0

    lax.fori_loop(0, n_big, big, 0)
    done = n_big * per_big
    m = per_big // 2
    while m >= 1:
        tail_here = ((n_blocks - done) // m) % 2 == 1
        start = done + ((n_blocks - done) // (2 * m)) * (2 * m)

        @pl.when(tail_here)
        def _(start=start, m=m):
            body(pl.multiple_of(start * ROW_BLOCK, ROW_BLOCK), m * ROW_BLOCK)

        m //= 2


def _wait_row_blocks(src_hbm, dst_ref, n_blocks, rows, sem):
    def drain(i, _):
        pltpu.make_async_copy(src_hbm.at[pl.ds(0, rows)], dst_ref.at[pl.ds(0, rows)], sem).wait()
        return 0

    lax.fori_loop(0, n_blocks, drain, 0)


def _moe_up_kernel(pe_ref, nb_ref, npass_ref, tok_ref, hp_hbm, wg_ref, wu_ref, bg_ref, bu_ref,
                   o_ref, gbuf_ref, xb_ref, sem):
    p = pl.program_id(0)
    f = pl.program_id(1)
    cap, half = gbuf_ref.shape
    n_pass = npass_ref[0]
    live = p < n_pass
    nb = nb_ref[p]

    @pl.when(jnp.logical_and(live, f == 0))
    def _():
        @pl.when(p == 0)
        def _():
            _gather_rows(hp_hbm, tok_ref, 0, nb * ROW_BLOCK, gbuf_ref, sem)

        _wait_row_blocks(hp_hbm, gbuf_ref, nb, ROW_BLOCK, sem)

        def unpack(rb, _):
            rows = pl.ds(pl.multiple_of(rb * ROW_BLOCK, ROW_BLOCK), ROW_BLOCK)
            hi, lo = _unpack_bf16_pair(gbuf_ref[rows, :])
            xb_ref[rows, :half] = hi.astype(BF16)
            xb_ref[rows, half:] = lo.astype(BF16)
            return 0

        lax.fori_loop(0, nb, unpack, 0)

        @pl.when(p + 1 < n_pass)
        def _():
            _gather_rows(hp_hbm, tok_ref, (p + 1) * cap, nb_ref[p + 1] * ROW_BLOCK, gbuf_ref, sem)

    @pl.when(live)
    def _():
        def block(row0, m):
            rows = pl.ds(row0, m)
            x = xb_ref[rows, :]
            hg = jnp.dot(x, wg_ref[...].astype(BF16), preferred_element_type=F32) + bg_ref[...]
            hu = jnp.dot(x, wu_ref[...].astype(BF16), preferred_element_type=F32) + bu_ref[...]
            hg = jnp.minimum(hg, SWIGLU_LIMIT)
            hu = jnp.clip(hu, -SWIGLU_LIMIT, SWIGLU_LIMIT)
            act = hg * _sigmoid(SWIGLU_ALPHA * hg) * (hu + 1.0)
            o_ref[rows, :] = act.astype(o_ref.dtype)

        _for_row_blocks(nb, block)


def _moe_up(pe, pass_nb, n_pass, tok_idx, hp, wg, wu, bg, bu, n_pass_max, tf):
    e, d, f = wg.shape
    nf = f // tf
    cap = PASS_ROWS

    def w_map(p, j, pe_, nb_, np_, tok):
        return (pe_[p], 0, jnp.where(p < np_[0], j, nf - 1))

    def o_map(p, j, pe_, nb_, np_, tok):
        ok = p < np_[0]
        return (jnp.where(ok, p, np_[0] - 1), jnp.where(ok, j, nf - 1))

    grid_spec = pltpu.PrefetchScalarGridSpec(
        num_scalar_prefetch=4,
        grid=(n_pass_max, nf),
        in_specs=[pl.BlockSpec(memory_space=pl.ANY),
                  pl.BlockSpec((None, d, tf), w_map),
                  pl.BlockSpec((None, d, tf), w_map),
                  pl.BlockSpec((None, 1, tf), w_map),
                  pl.BlockSpec((None, 1, tf), w_map)],
        out_specs=pl.BlockSpec((cap, tf), o_map),
        scratch_shapes=[pltpu.VMEM((cap, d // 2), U32),
                        pltpu.VMEM((cap, d), BF16),
                        pltpu.SemaphoreType.DMA(())],
    )
    return pl.pallas_call(
        _moe_up_kernel,
        grid_spec=grid_spec,
        out_shape=jax.ShapeDtypeStruct((n_pass_max * cap, f), BF16),
        compiler_params=_params("arbitrary", "arbitrary"),
        name="moe_up",
    )(pe, pass_nb, n_pass, tok_idx, hp, wg, wu, bg.reshape(e, 1, f), bu.reshape(e, 1, f))


def _moe_down_kernel(pe_ref, nb_ref, npass_ref, a_ref, wh_ref, wl_ref, bh_ref, bl_ref, o_ref):
    p = pl.program_id(0)

    @pl.when(p < npass_ref[0])
    def _():
        def block(row0, m):
            rows = pl.ds(row0, m)
            a = a_ref[rows, :]
            hi = jnp.dot(a, wh_ref[...].astype(BF16), preferred_element_type=F32) + bh_ref[...]
            lo = jnp.dot(a, wl_ref[...].astype(BF16), preferred_element_type=F32) + bl_ref[...]
            o_ref[rows, :] = _pack_bf16_pair(hi, lo)

        _for_row_blocks(nb_ref[p], block)


def _moe_down(pe, pass_nb, n_pass, act, wd, bd, n_pass_max, td):
    e, f, d = wd.shape
    nd = d // 2 // td
    cap = PASS_ROWS

    def col(p, j, np_):
        return jnp.where(p < np_[0], j, nd - 1)

    def row(p, np_):
        return jnp.where(p < np_[0], p, np_[0] - 1)

    grid_spec = pltpu.PrefetchScalarGridSpec(
        num_scalar_prefetch=3,
        grid=(n_pass_max, nd),
        in_specs=[pl.BlockSpec((cap, f), lambda p, j, pe_, nb_, np_: (row(p, np_), 0)),
                  pl.BlockSpec((None, f, td), lambda p, j, pe_, nb_, np_: (pe_[p], 0, col(p, j, np_))),
                  pl.BlockSpec((None, f, td), lambda p, j, pe_, nb_, np_: (pe_[p], 0, nd + col(p, j, np_))),
                  pl.BlockSpec((None, 1, td), lambda p, j, pe_, nb_, np_: (pe_[p], 0, col(p, j, np_))),
                  pl.BlockSpec((None, 1, td), lambda p, j, pe_, nb_, np_: (pe_[p], 0, nd + col(p, j, np_)))],
        out_specs=pl.BlockSpec((cap, td), lambda p, j, pe_, nb_, np_: (row(p, np_), col(p, j, np_))),
    )
    bd3 = bd.reshape(e, 1, d)
    return pl.pallas_call(
        _moe_down_kernel,
        grid_spec=grid_spec,
        out_shape=jax.ShapeDtypeStruct((n_pass_max * cap, d // 2), U32),
        compiler_params=_params("arbitrary", "arbitrary"),
        name="moe_down",
    )(pe, pass_nb, n_pass, act, wd, wd, bd3, bd3)


def _combine_kernel(pos_ref, y_hbm, x_ref, gate_ref, g_ref, o_ref, ybuf_ref, sems):
    i = pl.program_id(0)
    n = pl.num_programs(0)
    tt, d = x_ref.shape
    half = d // 2
    slot = i % 2

    def start_tile(step, slot_):
        for k in range(TOP_K):
            def issue(g, _):
                for u in range(GATHER_UNROLL):
                    r = g * GATHER_UNROLL + u
                    pltpu.make_async_copy(
                        y_hbm.at[pl.ds(pos_ref[(step * tt + r) * TOP_K + k], 1)],
                        ybuf_ref.at[slot_, k, pl.ds(r, 1)], sems.at[slot_]).start()
                return 0

            lax.fori_loop(0, tt // GATHER_UNROLL, issue, 0)

    @pl.when(i == 0)
    def _():
        start_tile(0, 0)

    @pl.when(i + 1 < n)
    def _():
        start_tile(i + 1, 1 - slot)

    for k in range(TOP_K):
        pltpu.make_async_copy(y_hbm.at[pl.ds(0, tt)], ybuf_ref.at[slot, k], sems.at[slot]).wait()

    gates = gate_ref[...]
    x = x_ref[...]
    acc_hi = x[:, :half]
    acc_lo = x[:, half:]
    for k in range(TOP_K):
        hi, lo = _unpack_bf16_pair(ybuf_ref[slot, k])
        acc_hi = acc_hi + gates[:, k:k + 1] * hi
        acc_lo = acc_lo + gates[:, k:k + 1] * lo
    ms = (jnp.sum(acc_hi * acc_hi, axis=-1, keepdims=True)
          + jnp.sum(acc_lo * acc_lo, axis=-1, keepdims=True)) * (1.0 / d)
    scale = lax.rsqrt(ms + RMS_EPS)
    o_ref[:, :half] = acc_hi * scale * g_ref[:, :half]
    o_ref[:, half:] = acc_lo * scale * g_ref[:, half:]


def _combine(pos_flat, y, x, gates, g, tt):
    t, d = x.shape
    grid_spec = pltpu.PrefetchScalarGridSpec(
        num_scalar_prefetch=1,
        grid=(t // tt,),
        in_specs=[pl.BlockSpec(memory_space=pl.ANY),
                  pl.BlockSpec((tt, d), lambda i, p: (i, 0)),
                  pl.BlockSpec((tt, TOP_K), lambda i, p: (i, 0)),
                  pl.BlockSpec((1, d), lambda i, p: (0, 0))],
        out_specs=pl.BlockSpec((tt, d), lambda i, p: (i, 0)),
        scratch_shapes=[pltpu.VMEM((2, TOP_K, tt, d // 2), U32),
                        pltpu.SemaphoreType.DMA((2,))],
    )
    return pl.pallas_call(
        _combine_kernel,
        grid_spec=grid_spec,
        out_shape=jax.ShapeDtypeStruct((t, d), F32),
        compiler_params=_params("arbitrary"),
        name="combine",
    )(pos_flat, y, x, gates, g.reshape(1, d))


def _moe_and_final_norm(x_mid, norm2_g, w_router, b_router, w_gate, b_gate, w_up, b_up,
                        w_down, b_down, final_norm_g, tt_router, tf, td, tt_combine):
    t = x_mid.shape[0]
    n_pass_max = _num_passes(t * TOP_K)
    hp, idx, gates, rank, counts = _router(x_mid, norm2_g, w_router, b_router, tt=tt_router)
    pe, pass_nb, n_pass, tok_idx, pos_flat = _routing_tables(idx, rank, counts[0], n_pass_max)
    act = _moe_up(pe, pass_nb, n_pass, tok_idx, hp, w_gate, w_up, b_gate, b_up, n_pass_max, tf=tf)
    y_rows = _moe_down(pe, pass_nb, n_pass, act, w_down, b_down, n_pass_max, td=td)
    return _combine(pos_flat, y_rows, x_mid, gates, final_norm_g, tt=tt_combine)


def kernel(x, meta_tokens, norm1_g, w_in, lru_conv_w, lru_conv_b, lru_w_a, lru_b_a, lru_w_i, lru_b_i, lru_lambda, conf_conv_w, conf_conv_b, conf_norm_g, conf_norm_b, w_out, norm2_g, w_router, b_router, w_gate, b_gate, w_up, b_up, w_down, b_down, final_norm_g):
    b, s, d = x.shape
    t = b * s
    x2 = x.reshape(t, d)

    h = _rmsnorm(x2, norm1_g[0], tm=512)
    h_meta = _rmsnorm(meta_tokens.astype(x.dtype), norm1_g[0], tm=N_META)
    z, z_meta = _inproj(h, h_meta, w_in[0], tm=1024, tn=512)
    z3 = z.reshape(b, s, -1)

    y_lru = _lru_mixer(z3, z_meta, lru_conv_w[0], lru_conv_b[0], lru_w_a[0], lru_b_a[0],
                       lru_w_i[0], lru_b_i[0], lru_lambda[0])
    y_conf = _conf_mixer(z3, z_meta, conf_conv_w[0], conf_conv_b[0], conf_norm_g[0],
                         conf_norm_b[0])
    x_mid = _outproj(y_lru.reshape(t, -1), y_conf.reshape(t, -1), w_out[0], x2,
                     tm=2048, tn=256)

    out = _moe_and_final_norm(x_mid, norm2_g[0], w_router[0], b_router[0], w_gate[0], b_gate[0],
                              w_up[0], b_up[0], w_down[0], b_down[0], final_norm_g,
                              tt_router=512, tf=256, td=512, tt_combine=256)
    return out.reshape(b, s, d)
```

```python
import jax
import jax.numpy as jnp
from jax import lax
from jax.experimental import pallas as pl
from jax.experimental.pallas import tpu as pltpu

N_META = 16
LRU_WIDTH = 2048
LRU_HEADS = 16
HEAD_DIM = 128
LRU_CONV_WIDTH = 4
LRU_C = 8.0
CONF_WIDTH = 2048
CONF_GROUPS = 16
CONF_KERNEL = 31
N_EXPERTS = 32
TOP_K = 4
SWIGLU_ALPHA = 1.702
SWIGLU_LIMIT = 7.0
RMS_EPS = 1e-5
LN_EPS = 1e-5
SQRT_FLOOR = 1e-30

SUBLANES = 8
VMEM_LIMIT = 56 * 1024 * 1024
VMEM_LIMIT_LARGE = 61 * 1024 * 1024

ROW_BLOCK = 128
MATMUL_ROWS = 1024
PASS_ROWS = 1536
BF16 = jnp.bfloat16
F32 = jnp.float32
U32 = jnp.uint32


def _params(*sem, vmem_limit=VMEM_LIMIT):
    return pltpu.CompilerParams(dimension_semantics=sem, vmem_limit_bytes=vmem_limit)


def _pack_bf16_pair(hi, lo):
    hi_bits = lax.bitcast_convert_type(hi.astype(BF16).astype(F32), U32)
    lo_bits = lax.bitcast_convert_type(lo.astype(BF16).astype(F32), U32)
    return hi_bits | lax.shift_right_logical(lo_bits, jnp.uint32(16))


def _sigmoid(x):
    return 0.5 * jnp.tanh(0.5 * x) + 0.5


def _unpack_bf16_pair(u):
    hi = lax.bitcast_convert_type(u & jnp.uint32(0xFFFF0000), F32)
    lo = lax.bitcast_convert_type(lax.shift_left(u, jnp.uint32(16)), F32)
    return hi, lo


def _rmsnorm_kernel(x_ref, g_ref, o_ref):
    x = x_ref[...]
    ms = jnp.mean(x * x, axis=-1, keepdims=True)
    o_ref[...] = (x * lax.rsqrt(ms + RMS_EPS) * g_ref[...]).astype(o_ref.dtype)


def _rmsnorm(x, g, tm):
    t, d = x.shape
    return pl.pallas_call(
        _rmsnorm_kernel,
        grid=(t // tm,),
        in_specs=[pl.BlockSpec((tm, d), lambda i: (i, 0)),
                  pl.BlockSpec((1, d), lambda i: (0, 0))],
        out_specs=pl.BlockSpec((tm, d), lambda i: (i, 0)),
        out_shape=jax.ShapeDtypeStruct((t, d), BF16),
        compiler_params=_params("parallel"),
        name="rmsnorm1",
    )(x, g.reshape(1, d))


def _inproj_kernel(a_ref, am_ref, w_ref, o_ref, om_ref, wb_ref):
    @pl.when(pl.program_id(1) == 0)
    def _():
        wb_ref[...] = w_ref[...].astype(BF16)
        om_ref[...] = jnp.dot(am_ref[...], wb_ref[...], preferred_element_type=F32)

    o_ref[...] = jnp.dot(a_ref[...], wb_ref[...], preferred_element_type=F32)


def _inproj(a, a_meta, w, tm, tn):
    t, k = a.shape
    n = w.shape[1]
    return pl.pallas_call(
        _inproj_kernel,
        grid=(n // tn, t // tm),
        in_specs=[pl.BlockSpec((tm, k), lambda j, i: (i, 0)),
                  pl.BlockSpec((N_META, k), lambda j, i: (0, 0)),
                  pl.BlockSpec((k, tn), lambda j, i: (0, j))],
        out_specs=[pl.BlockSpec((tm, tn), lambda j, i: (i, j)),
                   pl.BlockSpec((N_META, tn), lambda j, i: (0, j))],
        out_shape=[jax.ShapeDtypeStruct((t, n), F32),
                   jax.ShapeDtypeStruct((N_META, n), F32)],
        scratch_shapes=[pltpu.VMEM((k, tn), BF16)],
        compiler_params=_params("parallel", "arbitrary"),
        name="inproj",
    )(a, a_meta, w)


LRU_CHUNKS = SUBLANES
LRU_CHUNK_PAD = SUBLANES
LRU_SCAN_UNROLL = 8


def _gelu_tanh(x):
    return 0.5 * x * (1.0 + jnp.tanh(0.7978845608028654 * (x + 0.044715 * x * x * x)))


def _lru_kernel(xr_ref, gate_ref, xm_ref, cw_ref, cb_ref, wa_ref, ba_ref, wi_ref,
                bi_ref, lam_ref, o_ref, sf_ref, af_ref, bf_ref, ab_ref, bb_ref, hf_ref, hb_ref):
    s = xr_ref.shape[0]
    st = s + N_META
    pad = SUBLANES
    clen = s // LRU_CHUNKS
    pitch = clen + LRU_CHUNK_PAD
    zeros8 = jnp.zeros((pad, HEAD_DIM), F32)
    sf_ref[0:pad, :] = zeros8
    sf_ref[pad:pad + N_META, :] = xm_ref[...]
    sf_ref[pad + N_META:pad + st, :] = xr_ref[...]
    sf_ref[pad + st:pad + st + pad, :] = zeros8

    def gates(u, d):
        ub = u.astype(BF16)
        r = _sigmoid(jnp.dot(ub, wa_ref[d].astype(BF16), preferred_element_type=F32)
                           + ba_ref[d:d + 1, :])
        i = _sigmoid(jnp.dot(ub, wi_ref[d].astype(BF16), preferred_element_type=F32)
                           + bi_ref[d:d + 1, :])
        lam = lam_ref[d:d + 1, :]
        softplus_neg = jnp.maximum(-lam, 0.0) + jnp.log1p(jnp.exp(-jnp.abs(lam)))
        log_a = (-LRU_C) * r * softplus_neg
        a = jnp.exp(log_a)
        one_minus_a2 = -jnp.tanh(log_a) * (a * a + 1.0)
        root = one_minus_a2 * lax.rsqrt(jnp.maximum(one_minus_a2, SQRT_FLOOR))
        b = root * (i * u)
        return a, b

    uf = cb_ref[0:1, :] + cw_ref[0, 3:4, :] * sf_ref[pl.ds(pad, st), :]
    for j in range(1, LRU_CONV_WIDTH):
        uf = uf + cw_ref[0, 3 - j:4 - j, :] * sf_ref[pl.ds(pad - j, st), :]
    a, b = gates(uf, 0)
    h_meta = b[0:1, :]
    for r in range(1, N_META):
        h_meta = a[r:r + 1, :] * h_meta + b[r:r + 1, :]
    for c in range(LRU_CHUNKS):
        af_ref[c * pitch:c * pitch + clen, :] = a[N_META + c * clen:N_META + (c + 1) * clen, :]
        bf_ref[c * pitch:c * pitch + clen, :] = b[N_META + c * clen:N_META + (c + 1) * clen, :]

    base = pad + N_META
    ub_ = cb_ref[1:2, :] + cw_ref[1, 3:4, :] * sf_ref[pl.ds(base, s), :]
    for j in range(1, LRU_CONV_WIDTH):
        ub_ = ub_ + cw_ref[1, 3 - j:4 - j, :] * sf_ref[pl.ds(base + j, s), :]
    a, b = gates(ub_, 1)
    for c in range(LRU_CHUNKS):
        ab_ref[c * pitch:c * pitch + clen, :] = a[c * clen:(c + 1) * clen, :]
        bb_ref[c * pitch:c * pitch + clen, :] = b[c * clen:(c + 1) * clen, :]

    def step_rows(t):
        return (pl.ds(t, LRU_CHUNKS, stride=pitch),
                pl.ds(clen - 1 - t, LRU_CHUNKS, stride=pitch))

    def local_scan(t, carry):
        hf, pf, hb, pb = carry
        rf, rb = step_rows(t)
        a_f = af_ref[rf, :]
        a_b = ab_ref[rb, :]
        return (a_f * hf + bf_ref[rf, :], a_f * pf, a_b * hb + bb_ref[rb, :], a_b * pb)

    zero = sf_ref[0:SUBLANES, :]
    one = zero + 1.0
    hf, pf, hb, pb = lax.fori_loop(0, clen, local_scan, (zero, one, zero, one),
                                   unroll=LRU_SCAN_UNROLL)

    row = lax.broadcasted_iota(jnp.int32, (LRU_CHUNKS, HEAD_DIM), 0)
    c = h_meta
    start_f = jnp.where(row == 0, c, zero)
    for k in range(1, LRU_CHUNKS):
        c = hf[k - 1:k, :] + pf[k - 1:k, :] * c
        start_f = jnp.where(row == k, c, start_f)
    c = zero[0:1, :]
    start_b = zero
    for k in range(LRU_CHUNKS - 2, -1, -1):
        c = hb[k + 1:k + 2, :] + pb[k + 1:k + 2, :] * c
        start_b = jnp.where(row == k, c, start_b)

    def final_scan(t, carry):
        hf, hb = carry
        rf, rb = step_rows(t)
        hf = af_ref[rf, :] * hf + bf_ref[rf, :]
        hb = ab_ref[rb, :] * hb + bb_ref[rb, :]
        hf_ref[rf, :] = hf
        hb_ref[rb, :] = hb
        return hf, hb

    lax.fori_loop(0, clen, final_scan, (start_f, start_b), unroll=LRU_SCAN_UNROLL)
    for c in range(LRU_CHUNKS):
        hsum = hf_ref[c * pitch:c * pitch + clen, :] + hb_ref[c * pitch:c * pitch + clen, :]
        y = hsum * _gelu_tanh(gate_ref[c * clen:(c + 1) * clen, :])
        o_ref[c * clen:(c + 1) * clen, :] = y.astype(o_ref.dtype)


def _lru_mixer(z3, z_meta, cw, cb, wa, ba, wi, bi, lam):
    b, s, _ = z3.shape
    st = s + N_META
    hd = HEAD_DIM
    col = lambda off: (lambda bi_, h: (bi_, 0, off + h))
    return pl.pallas_call(
        _lru_kernel,
        grid=(b, LRU_HEADS),
        in_specs=[
            pl.BlockSpec((None, s, hd), col(0)),
            pl.BlockSpec((None, s, hd), col(LRU_HEADS)),
            pl.BlockSpec((N_META, hd), lambda bi_, h: (0, h)),
            pl.BlockSpec((2, LRU_CONV_WIDTH, hd), lambda bi_, h: (0, 0, h)),
            pl.BlockSpec((2, hd), lambda bi_, h: (0, h)),
            pl.BlockSpec((2, None, hd, hd), lambda bi_, h: (0, h, 0, 0)),
            pl.BlockSpec((2, hd), lambda bi_, h: (0, h)),
            pl.BlockSpec((2, None, hd, hd), lambda bi_, h: (0, h, 0, 0)),
            pl.BlockSpec((2, hd), lambda bi_, h: (0, h)),
            pl.BlockSpec((2, hd), lambda bi_, h: (0, h)),
        ],
        out_specs=pl.BlockSpec((None, s, hd), lambda bi_, h: (bi_, 0, h)),
        out_shape=jax.ShapeDtypeStruct((b, s, LRU_WIDTH), BF16),
        scratch_shapes=[
            pltpu.VMEM((st + 2 * SUBLANES, hd), F32),
        ] + [pltpu.VMEM((LRU_CHUNKS * (s // LRU_CHUNKS + LRU_CHUNK_PAD), hd), F32)] * 6,
        compiler_params=_params("parallel", "parallel"),
        name="lru_mixer",
    )(z3, z3, z_meta, cw, cb, wa, ba.reshape(2, LRU_WIDTH), wi, bi.reshape(2, LRU_WIDTH), lam)


CONF_CHUNK = 64
CONF_PARTIAL_SUMS = 2


def _conf_kernel(a_ref, b_ref, am_ref, bm_ref, cw_ref, cb_ref, g_ref, be_ref, o_ref,
                 cs_ref, sh_ref):
    s = a_ref.shape[0]
    st = s + N_META
    n_sh = sh_ref.shape[1]
    cs_ref[0:N_META, :] = am_ref[...] * _sigmoid(bm_ref[...])
    cs_ref[N_META:st, :] = a_ref[...] * _sigmoid(b_ref[...])
    cs_ref[st:st + N_META, :] = jnp.zeros((N_META, HEAD_DIM), F32)
    for r in range(1, SUBLANES):
        sh_ref[r - 1] = cs_ref[pl.ds(r, n_sh), :]

    n_chunks = s // CONF_CHUNK

    def conv_chunk(row0):
        parts = [None] * CONF_PARTIAL_SUMS
        for k in range(CONF_KERNEL):
            off = N_META - CONF_KERNEL // 2 + k
            r, q = off % SUBLANES, off // SUBLANES
            rows = pl.ds(row0 + q * SUBLANES, CONF_CHUNK)
            term = cw_ref[k:k + 1, :] * (cs_ref[rows, :] if r == 0 else sh_ref[r - 1, rows, :])
            p = k % CONF_PARTIAL_SUMS
            parts[p] = term if parts[p] is None else parts[p] + term
        return sum(parts[1:], parts[0]) + cb_ref[...]

    def step(j, carry):
        acc_prev, xc_prev = carry
        mean = jnp.mean(acc_prev, axis=-1, keepdims=True)
        var = jnp.mean(xc_prev * xc_prev, axis=-1, keepdims=True)
        acc = conv_chunk(pl.multiple_of(jnp.minimum(j, n_chunks - 1) * CONF_CHUNK, CONF_CHUNK))
        y = xc_prev * lax.rsqrt(var + LN_EPS) * g_ref[...] + be_ref[...]
        out_row = pl.multiple_of(jnp.maximum(j - 2, 0) * CONF_CHUNK, CONF_CHUNK)
        o_ref[pl.ds(out_row, CONF_CHUNK), :] = (y * _sigmoid(y)).astype(o_ref.dtype)
        return acc, acc_prev - mean

    warmup = cs_ref[pl.ds(0, CONF_CHUNK), :]
    lax.fori_loop(0, n_chunks + 2, step, (warmup, warmup))


def _conf_mixer(z3, z_meta, cw, cb, g, be):
    b, s, _ = z3.shape
    st = s + N_META
    hd = HEAD_DIM
    a_off = 2 * LRU_HEADS
    b_off = 2 * LRU_HEADS + CONF_GROUPS
    n_sh = st + N_META - SUBLANES
    vec = pl.BlockSpec((1, hd), lambda bi_, h: (0, h))
    return pl.pallas_call(
        _conf_kernel,
        grid=(b, CONF_GROUPS),
        in_specs=[
            pl.BlockSpec((None, s, hd), lambda bi_, h: (bi_, 0, a_off + h)),
            pl.BlockSpec((None, s, hd), lambda bi_, h: (bi_, 0, b_off + h)),
            pl.BlockSpec((N_META, hd), lambda bi_, h: (0, a_off + h)),
            pl.BlockSpec((N_META, hd), lambda bi_, h: (0, b_off + h)),
            pl.BlockSpec((CONF_KERNEL, hd), lambda bi_, h: (0, h)),
            vec, vec, vec,
        ],
        out_specs=pl.BlockSpec((None, s, hd), lambda bi_, h: (bi_, 0, h)),
        out_shape=jax.ShapeDtypeStruct((b, s, CONF_WIDTH), BF16),
        scratch_shapes=[
            pltpu.VMEM((st + N_META, hd), F32),
            pltpu.VMEM((SUBLANES - 1, n_sh, hd), F32),
        ],
        compiler_params=_params("parallel", "parallel"),
        name="conf_mixer",
    )(z3, z3, z_meta, z_meta, cw, cb.reshape(1, -1), g.reshape(1, -1), be.reshape(1, -1))


def _outproj_kernel(ya_ref, yb_ref, wa_ref, wb_ref, x_ref, o_ref):
    acc = jnp.dot(ya_ref[...], wa_ref[...].astype(BF16), preferred_element_type=F32)
    acc = acc + jnp.dot(yb_ref[...], wb_ref[...].astype(BF16), preferred_element_type=F32)
    o_ref[...] = x_ref[...] + acc


def _outproj(ya, yb, w, x, tm, tn):
    t, k = ya.shape
    n = w.shape[1]
    return pl.pallas_call(
        _outproj_kernel,
        grid=(t // tm, n // tn),
        in_specs=[pl.BlockSpec((tm, k), lambda i, j: (i, 0)),
                  pl.BlockSpec((tm, k), lambda i, j: (i, 0)),
                  pl.BlockSpec((k, tn), lambda i, j: (0, j)),
                  pl.BlockSpec((k, tn), lambda i, j: (1, j)),
                  pl.BlockSpec((tm, tn), lambda i, j: (i, j))],
        out_specs=pl.BlockSpec((tm, tn), lambda i, j: (i, j)),
        out_shape=jax.ShapeDtypeStruct((t, n), F32),
        compiler_params=_params("parallel", "parallel", vmem_limit=VMEM_LIMIT_LARGE),
        name="outproj",
    )(ya, yb, w, w, x)


def _router_kernel(x_ref, g_ref, wr_ref, br_ref, hp_ref, idx_ref, gate_ref, rank_ref,
                   cnt_ref, carry_ref):
    tt, d = x_ref.shape
    half = d // 2

    @pl.when(pl.program_id(0) == 0)
    def _():
        carry_ref[...] = jnp.zeros_like(carry_ref)

    x = x_ref[...]
    ms = jnp.mean(x * x, axis=-1, keepdims=True)
    h = x * lax.rsqrt(ms + RMS_EPS) * g_ref[...]
    hp_ref[...] = _pack_bf16_pair(h[:, :half], h[:, half:])

    h_hi = h.astype(BF16)
    h_lo = (h - h_hi.astype(F32)).astype(BF16)
    w = wr_ref[...]
    w_hi = w.astype(BF16)
    w_lo = (w - w_hi.astype(F32)).astype(BF16)
    logits = (jnp.dot(h_hi, w_hi, preferred_element_type=F32)
              + jnp.dot(h_lo, w_hi, preferred_element_type=F32)
              + jnp.dot(h_hi, w_lo, preferred_element_type=F32)) + br_ref[...]

    lane = lax.broadcasted_iota(jnp.int32, (tt, N_EXPERTS), 1)
    lane_k = lax.broadcasted_iota(jnp.int32, (tt, TOP_K), 1)
    work = logits
    vals, sels = [], []
    idx_out = jnp.zeros((tt, TOP_K), jnp.int32)
    for k in range(TOP_K):
        m = jnp.max(work, axis=1, keepdims=True)
        am = jnp.min(jnp.where(work == m, lane, N_EXPERTS), axis=1, keepdims=True)
        sel = lane == am
        vals.append(m)
        sels.append(sel)
        idx_out = jnp.where(lane_k == k, am, idx_out)
        work = jnp.where(sel, -jnp.inf, work)
    idx_ref[...] = idx_out

    exps = [jnp.exp(v - vals[0]) for v in vals]
    denom = exps[0] + exps[1] + exps[2] + exps[3]
    gate_out = jnp.zeros((tt, TOP_K), F32)
    for k in range(TOP_K):
        gate_out = jnp.where(lane_k == k, exps[k] / denom, gate_out)
    gate_ref[...] = gate_out

    onehot = jnp.zeros((tt, N_EXPERTS), F32)
    for sel in sels:
        onehot = onehot + sel.astype(F32)
    r_i = lax.broadcasted_iota(jnp.int32, (tt, tt), 0)
    c_i = lax.broadcasted_iota(jnp.int32, (tt, tt), 1)
    tri = (c_i < r_i).astype(BF16)
    before = jnp.dot(tri, onehot.astype(BF16), preferred_element_type=F32) + carry_ref[...]
    rank_out = jnp.zeros((tt, TOP_K), jnp.int32)
    for k, sel in enumerate(sels):
        rk = jnp.sum(jnp.where(sel, before, 0.0), axis=1, keepdims=True).astype(jnp.int32)
        rank_out = jnp.where(lane_k == k, rk, rank_out)
    rank_ref[...] = rank_out
    carry_ref[...] = carry_ref[...] + jnp.sum(onehot, axis=0, keepdims=True)
    cnt_ref[...] = carry_ref[...].astype(jnp.int32)


def _router(x, g, wr, br, tt):
    t, d = x.shape
    small = lambda dt: jax.ShapeDtypeStruct((t, TOP_K), dt)
    return pl.pallas_call(
        _router_kernel,
        grid=(t // tt,),
        in_specs=[pl.BlockSpec((tt, d), lambda i: (i, 0)),
                  pl.BlockSpec((1, d), lambda i: (0, 0)),
                  pl.BlockSpec((d, N_EXPERTS), lambda i: (0, 0)),
                  pl.BlockSpec((1, N_EXPERTS), lambda i: (0, 0))],
        out_specs=[pl.BlockSpec((tt, d // 2), lambda i: (i, 0)),
                   pl.BlockSpec((tt, TOP_K), lambda i: (i, 0)),
                   pl.BlockSpec((tt, TOP_K), lambda i: (i, 0)),
                   pl.BlockSpec((tt, TOP_K), lambda i: (i, 0)),
                   pl.BlockSpec((1, N_EXPERTS), lambda i: (0, 0))],
        out_shape=[jax.ShapeDtypeStruct((t, d // 2), U32),
                   small(jnp.int32), small(F32), small(jnp.int32),
                   jax.ShapeDtypeStruct((1, N_EXPERTS), jnp.int32)],
        scratch_shapes=[pltpu.VMEM((1, N_EXPERTS), F32)],
        compiler_params=_params("arbitrary"),
        name="router",
    )(x, g.reshape(1, d), wr, br.reshape(1, N_EXPERTS))


def _num_passes(n_assign):
    return N_EXPERTS + n_assign // PASS_ROWS


def _routing_tables(idx, rank, counts, n_pass_max):
    t = idx.shape[0]
    padded = (counts + ROW_BLOCK - 1) // ROW_BLOCK * ROW_BLOCK
    passes_e = (padded + PASS_ROWS - 1) // PASS_ROWS
    pass_end = jnp.cumsum(passes_e)
    pass_start = pass_end - passes_e
    n_pass = pass_end[-1]
    pos = (pass_start[idx] + rank // PASS_ROWS) * PASS_ROWS + rank % PASS_ROWS
    p_ids = jnp.arange(n_pass_max, dtype=jnp.int32)
    live = p_ids < n_pass
    pe = jnp.searchsorted(pass_end, jnp.minimum(p_ids, n_pass - 1), side="right")
    pe = jnp.minimum(pe, N_EXPERTS - 1).astype(jnp.int32)
    rows = jnp.clip(padded[pe] - (p_ids - pass_start[pe]) * PASS_ROWS, 0, PASS_ROWS)
    pass_nb = jnp.where(live, rows // ROW_BLOCK, 0).astype(jnp.int32)
    tok = jnp.repeat(jnp.arange(t, dtype=jnp.int32), TOP_K)
    tok_idx = jnp.zeros((n_pass_max * PASS_ROWS,), jnp.int32).at[pos.reshape(-1)].set(
        tok, unique_indices=True, mode="promise_in_bounds")
    return (pe, pass_nb, n_pass.reshape(1).astype(jnp.int32), tok_idx,
            pos.T.reshape(-1).astype(jnp.int32))


GATHER_UNROLL = 8


def _gather_rows(src_hbm, idx_ref, idx_base, n_rows, dst_ref, sem):
    def issue(g, _):
        r0 = pl.multiple_of(g * GATHER_UNROLL, GATHER_UNROLL)
        dst_tile = dst_ref.at[pl.ds(r0, GATHER_UNROLL)]
        for u in range(GATHER_UNROLL):
            pltpu.make_async_copy(src_hbm.at[pl.ds(idx_ref[idx_base + r0 + u], 1)],
                                  dst_tile.at[pl.ds(u, 1)], sem).start()
        return 0

    lax.fori_loop(0, n_rows // GATHER_UNROLL, issue, 0)


def _for_row_blocks(n_blocks, body):
    per_big = MATMUL_ROWS // ROW_BLOCK
    n_big = n_blocks // per_big

    def big(i, _):
        body(pl.multiple_of(i * MATMUL_ROWS, MATMUL_ROWS), MATMUL_ROWS)
        return 0

    lax.fori_loop(0, n_big, big, 0)
    done = n_big * per_big
    m = per_big // 2
    while m >= 1:
        tail_here = ((n_blocks - done) // m) % 2 == 1
        start = done + ((n_blocks - done) // (2 * m)) * (2 * m)

        @pl.when(tail_here)
        def _(start=start, m=m):
            body(pl.multiple_of(start * ROW_BLOCK, ROW_BLOCK), m * ROW_BLOCK)

        m //= 2


def _wait_row_blocks(src_hbm, dst_ref, n_blocks, rows, sem):
    def drain(i, _):
        pltpu.make_async_copy(src_hbm.at[pl.ds(0, rows)], dst_ref.at[pl.ds(0, rows)], sem).wait()
        return 0

    lax.fori_loop(0, n_blocks, drain, 0)


def _moe_up_kernel(pe_ref, nb_ref, npass_ref, tok_ref, hp_hbm, wg_ref, wu_ref, bg_ref, bu_ref,
                   o_ref, gbuf_ref, xb_ref, sem):
    p = pl.program_id(0)
    f = pl.program_id(1)
    cap, half = gbuf_ref.shape
    n_pass = npass_ref[0]
    live = p < n_pass
    nb = nb_ref[p]

    @pl.when(jnp.logical_and(live, f == 0))
    def _():
        @pl.when(p == 0)
        def _():
            _gather_rows(hp_hbm, tok_ref, 0, nb * ROW_BLOCK, gbuf_ref, sem)

        _wait_row_blocks(hp_hbm, gbuf_ref, nb, ROW_BLOCK, sem)

        def unpack(rb, _):
            rows = pl.ds(pl.multiple_of(rb * ROW_BLOCK, ROW_BLOCK), ROW_BLOCK)
            hi, lo = _unpack_bf16_pair(gbuf_ref[rows, :])
            xb_ref[rows, :half] = hi.astype(BF16)
            xb_ref[rows, half:] = lo.astype(BF16)
            return 0

        lax.fori_loop(0, nb, unpack, 0)

        @pl.when(p + 1 < n_pass)
        def _():
            _gather_rows(hp_hbm, tok_ref, (p + 1) * cap, nb_ref[p + 1] * ROW_BLOCK, gbuf_ref, sem)

    @pl.when(live)
    def _():
        def block(row0, m):
            rows = pl.ds(row0, m)
            x = xb_ref[rows, :]
            hg = jnp.dot(x, wg_ref[...].astype(BF16), preferred_element_type=F32) + bg_ref[...]
            hu = jnp.dot(x, wu_ref[...].astype(BF16), preferred_element_type=F32) + bu_ref[...]
            hg = jnp.minimum(hg, SWIGLU_LIMIT)
            hu = jnp.clip(hu, -SWIGLU_LIMIT, SWIGLU_LIMIT)
            act = hg * _sigmoid(SWIGLU_ALPHA * hg) * (hu + 1.0)
            o_ref[rows, :] = act.astype(o_ref.dtype)

        _for_row_blocks(nb, block)


def _moe_up(pe, pass_nb, n_pass, tok_idx, hp, wg, wu, bg, bu, n_pass_max, tf):
    e, d, f = wg.shape
    nf = f // tf
    cap = PASS_ROWS

    def w_map(p, j, pe_, nb_, np_, tok):
        return (pe_[p], 0, jnp.where(p < np_[0], j, nf - 1))

    def o_map(p, j, pe_, nb_, np_, tok):
        ok = p < np_[0]
        return (jnp.where(ok, p, np_[0] - 1), jnp.where(ok, j, nf - 1))

    grid_spec = pltpu.PrefetchScalarGridSpec(
        num_scalar_prefetch=4,
        grid=(n_pass_max, nf),
        in_specs=[pl.BlockSpec(memory_space=pl.ANY),
                  pl.BlockSpec((None, d, tf), w_map),
                  pl.BlockSpec((None, d, tf), w_map),
                  pl.BlockSpec((None, 1, tf), w_map),
                  pl.BlockSpec((None, 1, tf), w_map)],
        out_specs=pl.BlockSpec((cap, tf), o_map),
        scratch_shapes=[pltpu.VMEM((cap, d // 2), U32),
                        pltpu.VMEM((cap, d), BF16),
                        pltpu.SemaphoreType.DMA(())],
    )
    return pl.pallas_call(
        _moe_up_kernel,
        grid_spec=grid_spec,
        out_shape=jax.ShapeDtypeStruct((n_pass_max * cap, f), BF16),
        compiler_params=_params("arbitrary", "arbitrary"),
        name="moe_up",
    )(pe, pass_nb, n_pass, tok_idx, hp, wg, wu, bg.reshape(e, 1, f), bu.reshape(e, 1, f))


def _moe_down_kernel(pe_ref, nb_ref, npass_ref, a_ref, wh_ref, wl_ref, bh_ref, bl_ref, o_ref):
    p = pl.program_id(0)

    @pl.when(p < npass_ref[0])
    def _():
        def block(row0, m):
            rows = pl.ds(row0, m)
            a = a_ref[rows, :]
            hi = jnp.dot(a, wh_ref[...].astype(BF16), preferred_element_type=F32) + bh_ref[...]
            lo = jnp.dot(a, wl_ref[...].astype(BF16), preferred_element_type=F32) + bl_ref[...]
            o_ref[rows, :] = _pack_bf16_pair(hi, lo)

        _for_row_blocks(nb_ref[p], block)


def _moe_down(pe, pass_nb, n_pass, act, wd, bd, n_pass_max, td):
    e, f, d = wd.shape
    nd = d // 2 // td
    cap = PASS_ROWS

    def col(p, j, np_):
        return jnp.where(p < np_[0], j, nd - 1)

    def row(p, np_):
        return jnp.where(p < np_[0], p, np_[0] - 1)

    grid_spec = pltpu.PrefetchScalarGridSpec(
        num_scalar_prefetch=3,
        grid=(n_pass_max, nd),
        in_specs=[pl.BlockSpec((cap, f), lambda p, j, pe_, nb_, np_: (row(p, np_), 0)),
                  pl.BlockSpec((None, f, td), lambda p, j, pe_, nb_, np_: (pe_[p], 0, col(p, j, np_))),
                  pl.BlockSpec((None, f, td), lambda p, j, pe_, nb_, np_: (pe_[p], 0, nd + col(p, j, np_))),
                  pl.BlockSpec((None, 1, td), lambda p, j, pe_, nb_, np_: (pe_[p], 0, col(p, j, np_))),
                  pl.BlockSpec((None, 1, td), lambda p, j, pe_, nb_, np_: (pe_[p], 0, nd + col(p, j, np_)))],
        out_specs=pl.BlockSpec((cap, td), lambda p, j, pe_, nb_, np_: (row(p, np_), col(p, j, np_))),
    )
    bd3 = bd.reshape(e, 1, d)
    return pl.pallas_call(
        _moe_down_kernel,
        grid_spec=grid_spec,
        out_shape=jax.ShapeDtypeStruct((n_pass_max * cap, d // 2), U32),
        compiler_params=_params("arbitrary", "arbitrary"),
        name="moe_down",
    )(pe, pass_nb, n_pass, act, wd, wd, bd3, bd3)


def _combine_kernel(pos_ref, y_hbm, x_ref, gate_ref, g_ref, o_ref, ybuf_ref, sems):
    i = pl.program_id(0)
    n = pl.num_programs(0)
    tt, d = x_ref.shape
    half = d // 2
    slot = i % 2

    n_tok = n * tt

    def start_tile(step, slot_):
        for k in range(TOP_K):
            _gather_rows(y_hbm, pos_ref, k * n_tok + step * tt, tt, ybuf_ref.at[slot_, k],
                         sems.at[slot_])

    @pl.when(i == 0)
    def _():
        start_tile(0, 0)

    @pl.when(i + 1 < n)
    def _():
        start_tile(i + 1, 1 - slot)

    for k in range(TOP_K):
        pltpu.make_async_copy(y_hbm.at[pl.ds(0, tt)], ybuf_ref.at[slot, k], sems.at[slot]).wait()

    gates = gate_ref[...]
    x = x_ref[...]
    acc_hi = x[:, :half]
    acc_lo = x[:, half:]
    for k in range(TOP_K):
        hi, lo = _unpack_bf16_pair(ybuf_ref[slot, k])
        acc_hi = acc_hi + gates[:, k:k + 1] * hi
        acc_lo = acc_lo + gates[:, k:k + 1] * lo
    ms = (jnp.sum(acc_hi * acc_hi, axis=-1, keepdims=True)
          + jnp.sum(acc_lo * acc_lo, axis=-1, keepdims=True)) * (1.0 / d)
    scale = lax.rsqrt(ms + RMS_EPS)
    o_ref[:, :half] = acc_hi * scale * g_ref[:, :half]
    o_ref[:, half:] = acc_lo * scale * g_ref[:, half:]


def _combine(pos_flat, y, x, gates, g, tt):
    t, d = x.shape
    grid_spec = pltpu.PrefetchScalarGridSpec(
        num_scalar_prefetch=1,
        grid=(t // tt,),
        in_specs=[pl.BlockSpec(memory_space=pl.ANY),
                  pl.BlockSpec((tt, d), lambda i, p: (i, 0)),
                  pl.BlockSpec((tt, TOP_K), lambda i, p: (i, 0)),
                  pl.BlockSpec((1, d), lambda i, p: (0, 0))],
        out_specs=pl.BlockSpec((tt, d), lambda i, p: (i, 0)),
        scratch_shapes=[pltpu.VMEM((2, TOP_K, tt, d // 2), U32),
                        pltpu.SemaphoreType.DMA((2,))],
    )
    return pl.pallas_call(
        _combine_kernel,
        grid_spec=grid_spec,
        out_shape=jax.ShapeDtypeStruct((t, d), F32),
        compiler_params=_params("arbitrary"),
        name="combine",
    )(pos_flat, y, x, gates, g.reshape(1, d))


def _moe_and_final_norm(x_mid, norm2_g, w_router, b_router, w_gate, b_gate, w_up, b_up,
                        w_down, b_down, final_norm_g, tt_router, tf, td, tt_combine):
    t = x_mid.shape[0]
    n_pass_max = _num_passes(t * TOP_K)
    hp, idx, gates, rank, counts = _router(x_mid, norm2_g, w_router, b_router, tt=tt_router)
    pe, pass_nb, n_pass, tok_idx, pos_flat = _routing_tables(idx, rank, counts[0], n_pass_max)
    act = _moe_up(pe, pass_nb, n_pass, tok_idx, hp, w_gate, w_up, b_gate, b_up, n_pass_max, tf=tf)
    y_rows = _moe_down(pe, pass_nb, n_pass, act, w_down, b_down, n_pass_max, td=td)
    return _combine(pos_flat, y_rows, x_mid, gates, final_norm_g, tt=tt_combine)


def kernel(x, meta_tokens, norm1_g, w_in, lru_conv_w, lru_conv_b, lru_w_a, lru_b_a, lru_w_i, lru_b_i, lru_lambda, conf_conv_w, conf_conv_b, conf_norm_g, conf_norm_b, w_out, norm2_g, w_router, b_router, w_gate, b_gate, w_up, b_up, w_down, b_down, final_norm_g):
    b, s, d = x.shape
    t = b * s
    x2 = x.reshape(t, d)

    h = _rmsnorm(x2, norm1_g[0], tm=512)
    h_meta = _rmsnorm(meta_tokens.astype(x.dtype), norm1_g[0], tm=N_META)
    z, z_meta = _inproj(h, h_meta, w_in[0], tm=1024, tn=512)
    z3 = z.reshape(b, s, -1)

    y_lru = _lru_mixer(z3, z_meta, lru_conv_w[0], lru_conv_b[0], lru_w_a[0], lru_b_a[0],
                       lru_w_i[0], lru_b_i[0], lru_lambda[0])
    y_conf = _conf_mixer(z3, z_meta, conf_conv_w[0], conf_conv_b[0], conf_norm_g[0],
                         conf_norm_b[0])
    x_mid = _outproj(y_lru.reshape(t, -1), y_conf.reshape(t, -1), w_out[0], x2,
                     tm=2048, tn=256)

    out = _moe_and_final_norm(x_mid, norm2_g[0], w_router[0], b_router[0], w_gate[0], b_gate[0],
                              w_up[0], b_up[0], w_down[0], b_down[0], final_norm_g,
                              tt_router=512, tf=256, td=512, tt_combine=256)
    return out.reshape(b, s, d)
```

```python
import jax
import jax.numpy as jnp
from jax import lax
from jax.experimental import pallas as pl
from jax.experimental.pallas import tpu as pltpu

N_META = 16
LRU_WIDTH = 2048
LRU_HEADS = 16
HEAD_DIM = 128
LRU_CONV_WIDTH = 4
LRU_C = 8.0
CONF_WIDTH = 2048
CONF_GROUPS = 16
CONF_KERNEL = 31
N_EXPERTS = 32
TOP_K = 4
SWIGLU_ALPHA = 1.702
SWIGLU_LIMIT = 7.0
RMS_EPS = 1e-5
LN_EPS = 1e-5
SQRT_FLOOR = 1e-30

SUBLANES = 8
VMEM_LIMIT = 56 * 1024 * 1024
VMEM_LIMIT_LARGE = 61 * 1024 * 1024

ROW_BLOCK = 128
MATMUL_ROWS = 1024
PASS_ROWS = 1536
BF16 = jnp.bfloat16
F32 = jnp.float32
U32 = jnp.uint32


def _params(*sem, vmem_limit=VMEM_LIMIT):
    return pltpu.CompilerParams(dimension_semantics=sem, vmem_limit_bytes=vmem_limit)


def _pack_bf16_pair(hi, lo):
    hi_bits = lax.bitcast_convert_type(hi.astype(BF16).astype(F32), U32)
    lo_bits = lax.bitcast_convert_type(lo.astype(BF16).astype(F32), U32)
    return hi_bits | lax.shift_right_logical(lo_bits, jnp.uint32(16))


def _sigmoid(x):
    return 0.5 * jnp.tanh(0.5 * x) + 0.5


def _unpack_bf16_pair(u):
    hi = lax.bitcast_convert_type(u & jnp.uint32(0xFFFF0000), F32)
    lo = lax.bitcast_convert_type(lax.shift_left(u, jnp.uint32(16)), F32)
    return hi, lo


def _rmsnorm_kernel(x_ref, g_ref, o_ref):
    x = x_ref[...]
    ms = jnp.mean(x * x, axis=-1, keepdims=True)
    o_ref[...] = (x * lax.rsqrt(ms + RMS_EPS) * g_ref[...]).astype(o_ref.dtype)


def _rmsnorm(x, g, tm):
    t, d = x.shape
    return pl.pallas_call(
        _rmsnorm_kernel,
        grid=(t // tm,),
        in_specs=[pl.BlockSpec((tm, d), lambda i: (i, 0)),
                  pl.BlockSpec((1, d), lambda i: (0, 0))],
        out_specs=pl.BlockSpec((tm, d), lambda i: (i, 0)),
        out_shape=jax.ShapeDtypeStruct((t, d), BF16),
        compiler_params=_params("parallel"),
        name="rmsnorm1",
    )(x, g.reshape(1, d))


def _inproj_kernel(a_ref, am_ref, w_ref, o_ref, om_ref, wb_ref):
    @pl.when(pl.program_id(1) == 0)
    def _():
        wb_ref[...] = w_ref[...].astype(BF16)
        om_ref[...] = jnp.dot(am_ref[...], wb_ref[...], preferred_element_type=F32)

    o_ref[...] = jnp.dot(a_ref[...], wb_ref[...], preferred_element_type=F32)


def _inproj(a, a_meta, w, tm, tn):
    t, k = a.shape
    n = w.shape[1]
    return pl.pallas_call(
        _inproj_kernel,
        grid=(n // tn, t // tm),
        in_specs=[pl.BlockSpec((tm, k), lambda j, i: (i, 0)),
                  pl.BlockSpec((N_META, k), lambda j, i: (0, 0)),
                  pl.BlockSpec((k, tn), lambda j, i: (0, j))],
        out_specs=[pl.BlockSpec((tm, tn), lambda j, i: (i, j)),
                   pl.BlockSpec((N_META, tn), lambda j, i: (0, j))],
        out_shape=[jax.ShapeDtypeStruct((t, n), F32),
                   jax.ShapeDtypeStruct((N_META, n), F32)],
        scratch_shapes=[pltpu.VMEM((k, tn), BF16)],
        compiler_params=_params("parallel", "arbitrary"),
        name="inproj",
    )(a, a_meta, w)


LRU_CHUNKS = SUBLANES
LRU_CHUNK_PAD = SUBLANES
LRU_SCAN_UNROLL = 8


def _gelu_tanh(x):
    return 0.5 * x * (1.0 + jnp.tanh(0.7978845608028654 * (x + 0.044715 * x * x * x)))


def _lru_kernel(xr_ref, gate_ref, xm_ref, cw_ref, cb_ref, wa_ref, ba_ref, wi_ref,
                bi_ref, lam_ref, o_ref, sf_ref, af_ref, bf_ref, ab_ref, bb_ref, hf_ref, hb_ref):
    s = xr_ref.shape[0]
    st = s + N_META
    pad = SUBLANES
    clen = s // LRU_CHUNKS
    pitch = clen + LRU_CHUNK_PAD
    zeros8 = jnp.zeros((pad, HEAD_DIM), F32)
    sf_ref[0:pad, :] = zeros8
    sf_ref[pad:pad + N_META, :] = xm_ref[...]
    sf_ref[pad + N_META:pad + st, :] = xr_ref[...]
    sf_ref[pad + st:pad + st + pad, :] = zeros8

    def gates(u, d):
        ub = u.astype(BF16)
        r = _sigmoid(jnp.dot(ub, wa_ref[d].astype(BF16), preferred_element_type=F32)
                           + ba_ref[d:d + 1, :])
        i = _sigmoid(jnp.dot(ub, wi_ref[d].astype(BF16), preferred_element_type=F32)
                           + bi_ref[d:d + 1, :])
        lam = lam_ref[d:d + 1, :]
        softplus_neg = jnp.maximum(-lam, 0.0) + jnp.log1p(jnp.exp(-jnp.abs(lam)))
        log_a = (-LRU_C) * r * softplus_neg
        a = jnp.exp(log_a)
        one_minus_a2 = -jnp.tanh(log_a) * (a * a + 1.0)
        root = one_minus_a2 * lax.rsqrt(jnp.maximum(one_minus_a2, SQRT_FLOOR))
        b = root * (i * u)
        return a, b

    uf = cb_ref[0:1, :] + cw_ref[0, 3:4, :] * sf_ref[pl.ds(pad, st), :]
    for j in range(1, LRU_CONV_WIDTH):
        uf = uf + cw_ref[0, 3 - j:4 - j, :] * sf_ref[pl.ds(pad - j, st), :]
    a, b = gates(uf, 0)
    h_meta = b[0:1, :]
    for r in range(1, N_META):
        h_meta = a[r:r + 1, :] * h_meta + b[r:r + 1, :]
    for c in range(LRU_CHUNKS):
        af_ref[c * pitch:c * pitch + clen, :] = a[N_META + c * clen:N_META + (c + 1) * clen, :]
        bf_ref[c * pitch:c * pitch + clen, :] = b[N_META + c * clen:N_META + (c + 1) * clen, :]

    base = pad + N_META
    ub_ = cb_ref[1:2, :] + cw_ref[1, 3:4, :] * sf_ref[pl.ds(base, s), :]
    for j in range(1, LRU_CONV_WIDTH):
        ub_ = ub_ + cw_ref[1, 3 - j:4 - j, :] * sf_ref[pl.ds(base + j, s), :]
    a, b = gates(ub_, 1)
    for c in range(LRU_CHUNKS):
        ab_ref[c * pitch:c * pitch + clen, :] = a[c * clen:(c + 1) * clen, :]
        bb_ref[c * pitch:c * pitch + clen, :] = b[c * clen:(c + 1) * clen, :]

    def step_rows(t):
        return (pl.ds(t, LRU_CHUNKS, stride=pitch),
                pl.ds(clen - 1 - t, LRU_CHUNKS, stride=pitch))

    def local_scan(t, carry):
        hf, pf, hb, pb = carry
        rf, rb = step_rows(t)
        a_f = af_ref[rf, :]
        a_b = ab_ref[rb, :]
        return (a_f * hf + bf_ref[rf, :], a_f * pf, a_b * hb + bb_ref[rb, :], a_b * pb)

    zero = sf_ref[0:SUBLANES, :]
    one = zero + 1.0
    hf, pf, hb, pb = lax.fori_loop(0, clen, local_scan, (zero, one, zero, one),
                                   unroll=LRU_SCAN_UNROLL)

    row = lax.broadcasted_iota(jnp.int32, (LRU_CHUNKS, HEAD_DIM), 0)
    c = h_meta
    start_f = jnp.where(row == 0, c, zero)
    for k in range(1, LRU_CHUNKS):
        c = hf[k - 1:k, :] + pf[k - 1:k, :] * c
        start_f = jnp.where(row == k, c, start_f)
    c = zero[0:1, :]
    start_b = zero
    for k in range(LRU_CHUNKS - 2, -1, -1):
        c = hb[k + 1:k + 2, :] + pb[k + 1:k + 2, :] * c
        start_b = jnp.where(row == k, c, start_b)

    def final_scan(t, carry):
        hf, hb = carry
        rf, rb = step_rows(t)
        hf = af_ref[rf, :] * hf + bf_ref[rf, :]
        hb = ab_ref[rb, :] * hb + bb_ref[rb, :]
        hf_ref[rf, :] = hf
        hb_ref[rb, :] = hb
        return hf, hb

    lax.fori_loop(0, clen, final_scan, (start_f, start_b), unroll=LRU_SCAN_UNROLL)
    for c in range(LRU_CHUNKS):
        hsum = hf_ref[c * pitch:c * pitch + clen, :] + hb_ref[c * pitch:c * pitch + clen, :]
        y = hsum * _gelu_tanh(gate_ref[c * clen:(c + 1) * clen, :])
        o_ref[c * clen:(c + 1) * clen, :] = y.astype(o_ref.dtype)


def _lru_mixer(z3, z_meta, cw, cb, wa, ba, wi, bi, lam):
    b, s, _ = z3.shape
    st = s + N_META
    hd = HEAD_DIM
    col = lambda off: (lambda bi_, h: (bi_, 0, off + h))
    return pl.pallas_call(
        _lru_kernel,
        grid=(b, LRU_HEADS),
        in_specs=[
            pl.BlockSpec((None, s, hd), col(0)),
            pl.BlockSpec((None, s, hd), col(LRU_HEADS)),
            pl.BlockSpec((N_META, hd), lambda bi_, h: (0, h)),
            pl.BlockSpec((2, LRU_CONV_WIDTH, hd), lambda bi_, h: (0, 0, h)),
            pl.BlockSpec((2, hd), lambda bi_, h: (0, h)),
            pl.BlockSpec((2, None, hd, hd), lambda bi_, h: (0, h, 0, 0)),
            pl.BlockSpec((2, hd), lambda bi_, h: (0, h)),
            pl.BlockSpec((2, None, hd, hd), lambda bi_, h: (0, h, 0, 0)),
            pl.BlockSpec((2, hd), lambda bi_, h: (0, h)),
            pl.BlockSpec((2, hd), lambda bi_, h: (0, h)),
        ],
        out_specs=pl.BlockSpec((None, s, hd), lambda bi_, h: (bi_, 0, h)),
        out_shape=jax.ShapeDtypeStruct((b, s, LRU_WIDTH), BF16),
        scratch_shapes=[
            pltpu.VMEM((st + 2 * SUBLANES, hd), F32),
        ] + [pltpu.VMEM((LRU_CHUNKS * (s // LRU_CHUNKS + LRU_CHUNK_PAD), hd), F32)] * 6,
        compiler_params=_params("parallel", "parallel"),
        name="lru_mixer",
    )(z3, z3, z_meta, cw, cb, wa, ba.reshape(2, LRU_WIDTH), wi, bi.reshape(2, LRU_WIDTH), lam)


CONF_CHUNK = 64
CONF_PARTIAL_SUMS = 2


def _conf_kernel(a_ref, b_ref, am_ref, bm_ref, cw_ref, cb_ref, g_ref, be_ref, o_ref,
                 cs_ref, sh_ref):
    s = a_ref.shape[0]
    st = s + N_META
    n_sh = sh_ref.shape[1]
    cs_ref[0:N_META, :] = am_ref[...] * _sigmoid(bm_ref[...])
    cs_ref[N_META:st, :] = a_ref[...] * _sigmoid(b_ref[...])
    cs_ref[st:st + N_META, :] = jnp.zeros((N_META, HEAD_DIM), F32)
    for r in range(1, SUBLANES):
        sh_ref[r - 1] = cs_ref[pl.ds(r, n_sh), :]

    n_chunks = s // CONF_CHUNK

    def conv_chunk(row0):
        parts = [None] * CONF_PARTIAL_SUMS
        for k in range(CONF_KERNEL):
            off = N_META - CONF_KERNEL // 2 + k
            r, q = off % SUBLANES, off // SUBLANES
            rows = pl.ds(row0 + q * SUBLANES, CONF_CHUNK)
            term = cw_ref[k:k + 1, :] * (cs_ref[rows, :] if r == 0 else sh_ref[r - 1, rows, :])
            p = k % CONF_PARTIAL_SUMS
            parts[p] = term if parts[p] is None else parts[p] + term
        return sum(parts[1:], parts[0]) + cb_ref[...]

    def step(j, carry):
        acc_prev, xc_prev = carry
        mean = jnp.mean(acc_prev, axis=-1, keepdims=True)
        var = jnp.mean(xc_prev * xc_prev, axis=-1, keepdims=True)
        acc = conv_chunk(pl.multiple_of(jnp.minimum(j, n_chunks - 1) * CONF_CHUNK, CONF_CHUNK))
        y = xc_prev * lax.rsqrt(var + LN_EPS) * g_ref[...] + be_ref[...]
        out_row = pl.multiple_of(jnp.maximum(j - 2, 0) * CONF_CHUNK, CONF_CHUNK)
        o_ref[pl.ds(out_row, CONF_CHUNK), :] = (y * _sigmoid(y)).astype(o_ref.dtype)
        return acc, acc_prev - mean

    warmup = cs_ref[pl.ds(0, CONF_CHUNK), :]
    lax.fori_loop(0, n_chunks + 2, step, (warmup, warmup))


def _conf_mixer(z3, z_meta, cw, cb, g, be):
    b, s, _ = z3.shape
    st = s + N_META
    hd = HEAD_DIM
    a_off = 2 * LRU_HEADS
    b_off = 2 * LRU_HEADS + CONF_GROUPS
    n_sh = st + N_META - SUBLANES
    vec = pl.BlockSpec((1, hd), lambda bi_, h: (0, h))
    return pl.pallas_call(
        _conf_kernel,
        grid=(b, CONF_GROUPS),
        in_specs=[
            pl.BlockSpec((None, s, hd), lambda bi_, h: (bi_, 0, a_off + h)),
            pl.BlockSpec((None, s, hd), lambda bi_, h: (bi_, 0, b_off + h)),
            pl.BlockSpec((N_META, hd), lambda bi_, h: (0, a_off + h)),
            pl.BlockSpec((N_META, hd), lambda bi_, h: (0, b_off + h)),
            pl.BlockSpec((CONF_KERNEL, hd), lambda bi_, h: (0, h)),
            vec, vec, vec,
        ],
        out_specs=pl.BlockSpec((None, s, hd), lambda bi_, h: (bi_, 0, h)),
        out_shape=jax.ShapeDtypeStruct((b, s, CONF_WIDTH), BF16),
        scratch_shapes=[
            pltpu.VMEM((st + N_META, hd), F32),
            pltpu.VMEM((SUBLANES - 1, n_sh, hd), F32),
        ],
        compiler_params=_params("parallel", "parallel"),
        name="conf_mixer",
    )(z3, z3, z_meta, z_meta, cw, cb.reshape(1, -1), g.reshape(1, -1), be.reshape(1, -1))


def _outproj_kernel(ya_ref, yb_ref, wa_ref, wb_ref, x_ref, o_ref):
    acc = jnp.dot(ya_ref[...], wa_ref[...].astype(BF16), preferred_element_type=F32)
    acc = acc + jnp.dot(yb_ref[...], wb_ref[...].astype(BF16), preferred_element_type=F32)
    o_ref[...] = x_ref[...] + acc


def _outproj(ya, yb, w, x, tm, tn):
    t, k = ya.shape
    n = w.shape[1]
    return pl.pallas_call(
        _outproj_kernel,
        grid=(t // tm, n // tn),
        in_specs=[pl.BlockSpec((tm, k), lambda i, j: (i, 0)),
                  pl.BlockSpec((tm, k), lambda i, j: (i, 0)),
                  pl.BlockSpec((k, tn), lambda i, j: (0, j)),
                  pl.BlockSpec((k, tn), lambda i, j: (1, j)),
                  pl.BlockSpec((tm, tn), lambda i, j: (i, j))],
        out_specs=pl.BlockSpec((tm, tn), lambda i, j: (i, j)),
        out_shape=jax.ShapeDtypeStruct((t, n), F32),
        compiler_params=_params("parallel", "parallel", vmem_limit=VMEM_LIMIT_LARGE),
        name="outproj",
    )(ya, yb, w, w, x)


def _router_kernel(x_ref, g_ref, wr_ref, br_ref, hp_ref, idx_ref, gate_ref, rank_ref,
                   cnt_ref, carry_ref):
    tt, d = x_ref.shape
    half = d // 2

    @pl.when(pl.program_id(0) == 0)
    def _():
        carry_ref[...] = jnp.zeros_like(carry_ref)

    x = x_ref[...]
    ms = jnp.mean(x * x, axis=-1, keepdims=True)
    h = x * lax.rsqrt(ms + RMS_EPS) * g_ref[...]
    hp_ref[...] = _pack_bf16_pair(h[:, :half], h[:, half:])

    h_hi = h.astype(BF16)
    h_lo = (h - h_hi.astype(F32)).astype(BF16)
    w = wr_ref[...]
    w_hi = w.astype(BF16)
    w_lo = (w - w_hi.astype(F32)).astype(BF16)
    hi_terms = jnp.dot(h_hi, jnp.concatenate([w_hi, w_lo], axis=1), preferred_element_type=F32)
    logits = (hi_terms[:, :N_EXPERTS] + hi_terms[:, N_EXPERTS:]
              + jnp.dot(h_lo, w_hi, preferred_element_type=F32)) + br_ref[...]

    lane = lax.broadcasted_iota(jnp.int32, (tt, N_EXPERTS), 1)
    lane_k = lax.broadcasted_iota(jnp.int32, (tt, TOP_K), 1)
    work = logits
    vals, sels = [], []
    idx_out = jnp.zeros((tt, TOP_K), jnp.int32)
    for k in range(TOP_K):
        m = jnp.max(work, axis=1, keepdims=True)
        am = jnp.min(jnp.where(work == m, lane, N_EXPERTS), axis=1, keepdims=True)
        sel = lane == am
        vals.append(m)
        sels.append(sel)
        idx_out = jnp.where(lane_k == k, am, idx_out)
        work = jnp.where(sel, -jnp.inf, work)
    idx_ref[...] = idx_out

    exps = [jnp.exp(v - vals[0]) for v in vals]
    denom = exps[0] + exps[1] + exps[2] + exps[3]
    gate_out = jnp.zeros((tt, TOP_K), F32)
    for k in range(TOP_K):
        gate_out = jnp.where(lane_k == k, exps[k] / denom, gate_out)
    gate_ref[...] = gate_out

    onehot = jnp.zeros((tt, N_EXPERTS), F32)
    for sel in sels:
        onehot = onehot + sel.astype(F32)
    r_i = lax.broadcasted_iota(jnp.int32, (tt, tt), 0)
    c_i = lax.broadcasted_iota(jnp.int32, (tt, tt), 1)
    tri = (c_i < r_i).astype(BF16)
    before = jnp.dot(tri, onehot.astype(BF16), preferred_element_type=F32) + carry_ref[...]
    rank_out = jnp.zeros((tt, TOP_K), jnp.int32)
    for k, sel in enumerate(sels):
        rk = jnp.sum(jnp.where(sel, before, 0.0), axis=1, keepdims=True).astype(jnp.int32)
        rank_out = jnp.where(lane_k == k, rk, rank_out)
    rank_ref[...] = rank_out
    carry_ref[...] = carry_ref[...] + jnp.sum(onehot, axis=0, keepdims=True)
    cnt_ref[...] = carry_ref[...].astype(jnp.int32)


def _router(x, g, wr, br, tt):
    t, d = x.shape
    small = lambda dt: jax.ShapeDtypeStruct((t, TOP_K), dt)
    return pl.pallas_call(
        _router_kernel,
        grid=(t // tt,),
        in_specs=[pl.BlockSpec((tt, d), lambda i: (i, 0)),
                  pl.BlockSpec((1, d), lambda i: (0, 0)),
                  pl.BlockSpec((d, N_EXPERTS), lambda i: (0, 0)),
                  pl.BlockSpec((1, N_EXPERTS), lambda i: (0, 0))],
        out_specs=[pl.BlockSpec((tt, d // 2), lambda i: (i, 0)),
                   pl.BlockSpec((tt, TOP_K), lambda i: (i, 0)),
                   pl.BlockSpec((tt, TOP_K), lambda i: (i, 0)),
                   pl.BlockSpec((tt, TOP_K), lambda i: (i, 0)),
                   pl.BlockSpec((1, N_EXPERTS), lambda i: (0, 0))],
        out_shape=[jax.ShapeDtypeStruct((t, d // 2), U32),
                   small(jnp.int32), small(F32), small(jnp.int32),
                   jax.ShapeDtypeStruct((1, N_EXPERTS), jnp.int32)],
        scratch_shapes=[pltpu.VMEM((1, N_EXPERTS), F32)],
        compiler_params=_params("arbitrary"),
        name="router",
    )(x, g.reshape(1, d), wr, br.reshape(1, N_EXPERTS))


def _num_passes(n_assign):
    return N_EXPERTS + n_assign // PASS_ROWS


def _routing_tables(idx, rank, counts, n_pass_max):
    t = idx.shape[0]
    padded = (counts + ROW_BLOCK - 1) // ROW_BLOCK * ROW_BLOCK
    passes_e = (padded + PASS_ROWS - 1) // PASS_ROWS
    pass_end = jnp.cumsum(passes_e)
    pass_start = pass_end - passes_e
    n_pass = pass_end[-1]
    pos = (pass_start[idx] + rank // PASS_ROWS) * PASS_ROWS + rank % PASS_ROWS
    p_ids = jnp.arange(n_pass_max, dtype=jnp.int32)
    live = p_ids < n_pass
    pe = jnp.searchsorted(pass_end, jnp.minimum(p_ids, n_pass - 1), side="right")
    pe = jnp.minimum(pe, N_EXPERTS - 1).astype(jnp.int32)
    done = (p_ids - pass_start[pe]) * PASS_ROWS
    rows = jnp.clip(padded[pe] - done, 0, PASS_ROWS)
    pass_nb = jnp.where(live, rows // ROW_BLOCK, 0).astype(jnp.int32)
    pass_valid = jnp.where(live, jnp.clip(counts[pe] - done, 0, PASS_ROWS), 0).astype(jnp.int32)
    return (pe, pass_nb, pass_valid, n_pass.reshape(1).astype(jnp.int32),
            pos.T.reshape(-1).astype(jnp.int32))


GATHER_UNROLL = 8


def _gather_rows(src_hbm, idx_ref, idx_base, n_rows, dst_ref, sem):
    def issue(g, _):
        r0 = pl.multiple_of(g * GATHER_UNROLL, GATHER_UNROLL)
        dst_tile = dst_ref.at[pl.ds(r0, GATHER_UNROLL)]
        for u in range(GATHER_UNROLL):
            pltpu.make_async_copy(src_hbm.at[pl.ds(idx_ref[idx_base + r0 + u], 1)],
                                  dst_tile.at[pl.ds(u, 1)], sem).start()
        return 0

    lax.fori_loop(0, n_rows // GATHER_UNROLL, issue, 0)


def _for_row_blocks(n_blocks, body):
    per_big = MATMUL_ROWS // ROW_BLOCK
    n_big = n_blocks // per_big

    def big(i, _):
        body(pl.multiple_of(i * MATMUL_ROWS, MATMUL_ROWS), MATMUL_ROWS)
        return 0

    lax.fori_loop(0, n_big, big, 0)
    done = n_big * per_big
    m = per_big // 2
    while m >= 1:
        tail_here = ((n_blocks - done) // m) % 2 == 1
        start = done + ((n_blocks - done) // (2 * m)) * (2 * m)

        @pl.when(tail_here)
        def _(start=start, m=m):
            body(pl.multiple_of(start * ROW_BLOCK, ROW_BLOCK), m * ROW_BLOCK)

        m //= 2


def _wait_row_blocks(src_hbm, dst_ref, n_blocks, rows, sem):
    def drain(i, _):
        pltpu.make_async_copy(src_hbm.at[pl.ds(0, rows)], dst_ref.at[pl.ds(0, rows)], sem).wait()
        return 0

    lax.fori_loop(0, n_blocks, drain, 0)


def _build_token_table(pos_ref, nb_ref, valid_ref, n_pass, cap, tok_ref):
    def pad_pass(q, _):
        def pad_row(r, _):
            tok_ref[q * cap + r] = 0
            return 0

        lax.fori_loop(valid_ref[q], nb_ref[q] * ROW_BLOCK, pad_row, 0)
        return 0

    lax.fori_loop(0, n_pass, pad_pass, 0)
    n_tok = pos_ref.shape[0] // TOP_K
    for k in range(TOP_K):
        def place(g, _, k=k):
            for u in range(GATHER_UNROLL):
                t = g * GATHER_UNROLL + u
                tok_ref[pos_ref[k * n_tok + t]] = t
            return 0

        lax.fori_loop(0, n_tok // GATHER_UNROLL, place, 0)


def _moe_up_kernel(pe_ref, nb_ref, valid_ref, npass_ref, pos_ref, hp_hbm, wg_ref, wu_ref, bg_ref,
                   bu_ref, o_ref, gbuf_ref, xb_ref, tok_ref, sem):
    p = pl.program_id(0)
    f = pl.program_id(1)
    cap, half = gbuf_ref.shape
    n_pass = npass_ref[0]
    live = p < n_pass
    nb = nb_ref[p]

    @pl.when(jnp.logical_and(live, f == 0))
    def _():
        @pl.when(p == 0)
        def _():
            _build_token_table(pos_ref, nb_ref, valid_ref, n_pass, cap, tok_ref)
            _gather_rows(hp_hbm, tok_ref, 0, nb * ROW_BLOCK, gbuf_ref, sem)

        _wait_row_blocks(hp_hbm, gbuf_ref, nb, ROW_BLOCK, sem)

        def unpack(rb, _):
            rows = pl.ds(pl.multiple_of(rb * ROW_BLOCK, ROW_BLOCK), ROW_BLOCK)
            hi, lo = _unpack_bf16_pair(gbuf_ref[rows, :])
            xb_ref[rows, :half] = hi.astype(BF16)
            xb_ref[rows, half:] = lo.astype(BF16)
            return 0

        lax.fori_loop(0, nb, unpack, 0)

        @pl.when(p + 1 < n_pass)
        def _():
            _gather_rows(hp_hbm, tok_ref, (p + 1) * cap, nb_ref[p + 1] * ROW_BLOCK, gbuf_ref, sem)

    @pl.when(live)
    def _():
        def block(row0, m):
            rows = pl.ds(row0, m)
            x = xb_ref[rows, :]
            hg = jnp.dot(x, wg_ref[...].astype(BF16), preferred_element_type=F32) + bg_ref[...]
            hu = jnp.dot(x, wu_ref[...].astype(BF16), preferred_element_type=F32) + bu_ref[...]
            hg = jnp.minimum(hg, SWIGLU_LIMIT)
            hu = jnp.clip(hu, -SWIGLU_LIMIT, SWIGLU_LIMIT)
            act = hg * _sigmoid(SWIGLU_ALPHA * hg) * (hu + 1.0)
            o_ref[rows, :] = act.astype(o_ref.dtype)

        _for_row_blocks(nb, block)


def _moe_up(pe, pass_nb, pass_valid, n_pass, pos_flat, hp, wg, wu, bg, bu, n_pass_max, tf):
    e, d, f = wg.shape
    nf = f // tf
    cap = PASS_ROWS

    def w_map(p, j, pe_, nb_, valid_, np_, pos_):
        return (pe_[p], 0, jnp.where(p < np_[0], j, nf - 1))

    def o_map(p, j, pe_, nb_, valid_, np_, pos_):
        ok = p < np_[0]
        return (jnp.where(ok, p, np_[0] - 1), jnp.where(ok, j, nf - 1))

    grid_spec = pltpu.PrefetchScalarGridSpec(
        num_scalar_prefetch=5,
        grid=(n_pass_max, nf),
        in_specs=[pl.BlockSpec(memory_space=pl.ANY),
                  pl.BlockSpec((None, d, tf), w_map),
                  pl.BlockSpec((None, d, tf), w_map),
                  pl.BlockSpec((None, 1, tf), w_map),
                  pl.BlockSpec((None, 1, tf), w_map)],
        out_specs=pl.BlockSpec((cap, tf), o_map),
        scratch_shapes=[pltpu.VMEM((cap, d // 2), U32),
                        pltpu.VMEM((cap, d), BF16),
                        pltpu.SMEM((n_pass_max * cap,), jnp.int32),
                        pltpu.SemaphoreType.DMA(())],
    )
    return pl.pallas_call(
        _moe_up_kernel,
        grid_spec=grid_spec,
        out_shape=jax.ShapeDtypeStruct((n_pass_max * cap, f), BF16),
        compiler_params=_params("arbitrary", "arbitrary"),
        name="moe_up",
    )(pe, pass_nb, pass_valid, n_pass, pos_flat, hp, wg, wu, bg.reshape(e, 1, f),
      bu.reshape(e, 1, f))


def _moe_down_kernel(pe_ref, nb_ref, npass_ref, a_ref, wh_ref, wl_ref, bh_ref, bl_ref, o_ref):
    p = pl.program_id(0)

    @pl.when(p < npass_ref[0])
    def _():
        def block(row0, m):
            rows = pl.ds(row0, m)
            a = a_ref[rows, :]
            hi = jnp.dot(a, wh_ref[...].astype(BF16), preferred_element_type=F32) + bh_ref[...]
            lo = jnp.dot(a, wl_ref[...].astype(BF16), preferred_element_type=F32) + bl_ref[...]
            o_ref[rows, :] = _pack_bf16_pair(hi, lo)

        _for_row_blocks(nb_ref[p], block)


def _moe_down(pe, pass_nb, n_pass, act, wd, bd, n_pass_max, td):
    e, f, d = wd.shape
    nd = d // 2 // td
    cap = PASS_ROWS

    def col(p, j, np_):
        return jnp.where(p < np_[0], j, nd - 1)

    def row(p, np_):
        return jnp.where(p < np_[0], p, np_[0] - 1)

    grid_spec = pltpu.PrefetchScalarGridSpec(
        num_scalar_prefetch=3,
        grid=(n_pass_max, nd),
        in_specs=[pl.BlockSpec((cap, f), lambda p, j, pe_, nb_, np_: (row(p, np_), 0)),
                  pl.BlockSpec((None, f, td), lambda p, j, pe_, nb_, np_: (pe_[p], 0, col(p, j, np_))),
                  pl.BlockSpec((None, f, td), lambda p, j, pe_, nb_, np_: (pe_[p], 0, nd + col(p, j, np_))),
                  pl.BlockSpec((None, 1, td), lambda p, j, pe_, nb_, np_: (pe_[p], 0, col(p, j, np_))),
                  pl.BlockSpec((None, 1, td), lambda p, j, pe_, nb_, np_: (pe_[p], 0, nd + col(p, j, np_)))],
        out_specs=pl.BlockSpec((cap, td), lambda p, j, pe_, nb_, np_: (row(p, np_), col(p, j, np_))),
    )
    bd3 = bd.reshape(e, 1, d)
    return pl.pallas_call(
        _moe_down_kernel,
        grid_spec=grid_spec,
        out_shape=jax.ShapeDtypeStruct((n_pass_max * cap, d // 2), U32),
        compiler_params=_params("arbitrary", "arbitrary"),
        name="moe_down",
    )(pe, pass_nb, n_pass, act, wd, wd, bd3, bd3)


def _combine_kernel(pos_ref, y_hbm, x_ref, gate_ref, g_ref, o_ref, ybuf_ref, sems):
    i = pl.program_id(0)
    n = pl.num_programs(0)
    tt, d = x_ref.shape
    half = d // 2
    slot = i % 2

    n_tok = n * tt

    def start_tile(step, slot_):
        for k in range(TOP_K):
            _gather_rows(y_hbm, pos_ref, k * n_tok + step * tt, tt, ybuf_ref.at[slot_, k],
                         sems.at[slot_])

    @pl.when(i == 0)
    def _():
        start_tile(0, 0)

    @pl.when(i + 1 < n)
    def _():
        start_tile(i + 1, 1 - slot)

    for k in range(TOP_K):
        pltpu.make_async_copy(y_hbm.at[pl.ds(0, tt)], ybuf_ref.at[slot, k], sems.at[slot]).wait()

    gates = gate_ref[...]
    x = x_ref[...]
    acc_hi = x[:, :half]
    acc_lo = x[:, half:]
    for k in range(TOP_K):
        hi, lo = _unpack_bf16_pair(ybuf_ref[slot, k])
        acc_hi = acc_hi + gates[:, k:k + 1] * hi
        acc_lo = acc_lo + gates[:, k:k + 1] * lo
    ms = (jnp.sum(acc_hi * acc_hi, axis=-1, keepdims=True)
          + jnp.sum(acc_lo * acc_lo, axis=-1, keepdims=True)) * (1.0 / d)
    scale = lax.rsqrt(ms + RMS_EPS)
    o_ref[:, :half] = acc_hi * scale * g_ref[:, :half]
    o_ref[:, half:] = acc_lo * scale * g_ref[:, half:]


def _combine(pos_flat, y, x, gates, g, tt):
    t, d = x.shape
    grid_spec = pltpu.PrefetchScalarGridSpec(
        num_scalar_prefetch=1,
        grid=(t // tt,),
        in_specs=[pl.BlockSpec(memory_space=pl.ANY),
                  pl.BlockSpec((tt, d), lambda i, p: (i, 0)),
                  pl.BlockSpec((tt, TOP_K), lambda i, p: (i, 0)),
                  pl.BlockSpec((1, d), lambda i, p: (0, 0))],
        out_specs=pl.BlockSpec((tt, d), lambda i, p: (i, 0)),
        scratch_shapes=[pltpu.VMEM((2, TOP_K, tt, d // 2), U32),
                        pltpu.SemaphoreType.DMA((2,))],
    )
    return pl.pallas_call(
        _combine_kernel,
        grid_spec=grid_spec,
        out_shape=jax.ShapeDtypeStruct((t, d), F32),
        compiler_params=_params("arbitrary"),
        name="combine",
    )(pos_flat, y, x, gates, g.reshape(1, d))


def _moe_and_final_norm(x_mid, norm2_g, w_router, b_router, w_gate, b_gate, w_up, b_up,
                        w_down, b_down, final_norm_g, tt_router, tf, td, tt_combine):
    t = x_mid.shape[0]
    n_pass_max = _num_passes(t * TOP_K)
    hp, idx, gates, rank, counts = _router(x_mid, norm2_g, w_router, b_router, tt=tt_router)
    pe, pass_nb, pass_valid, n_pass, pos_flat = _routing_tables(idx, rank, counts[0], n_pass_max)
    act = _moe_up(pe, pass_nb, pass_valid, n_pass, pos_flat, hp, w_gate, w_up, b_gate, b_up,
                  n_pass_max, tf=tf)
    y_rows = _moe_down(pe, pass_nb, n_pass, act, w_down, b_down, n_pass_max, td=td)
    return _combine(pos_flat, y_rows, x_mid, gates, final_norm_g, tt=tt_combine)


def kernel(x, meta_tokens, norm1_g, w_in, lru_conv_w, lru_conv_b, lru_w_a, lru_b_a, lru_w_i, lru_b_i, lru_lambda, conf_conv_w, conf_conv_b, conf_norm_g, conf_norm_b, w_out, norm2_g, w_router, b_router, w_gate, b_gate, w_up, b_up, w_down, b_down, final_norm_g):
    b, s, d = x.shape
    t = b * s
    x2 = x.reshape(t, d)

    h = _rmsnorm(x2, norm1_g[0], tm=512)
    h_meta = _rmsnorm(meta_tokens.astype(x.dtype), norm1_g[0], tm=N_META)
    z, z_meta = _inproj(h, h_meta, w_in[0], tm=1024, tn=512)
    z3 = z.reshape(b, s, -1)

    y_lru = _lru_mixer(z3, z_meta, lru_conv_w[0], lru_conv_b[0], lru_w_a[0], lru_b_a[0],
                       lru_w_i[0], lru_b_i[0], lru_lambda[0])
    y_conf = _conf_mixer(z3, z_meta, conf_conv_w[0], conf_conv_b[0], conf_norm_g[0],
                         conf_norm_b[0])
    x_mid = _outproj(y_lru.reshape(t, -1), y_conf.reshape(t, -1), w_out[0], x2,
                     tm=2048, tn=256)

    out = _moe_and_final_norm(x_mid, norm2_g[0], w_router[0], b_router[0], w_gate[0], b_gate[0],
                              w_up[0], b_up[0], w_down[0], b_down[0], final_norm_g,
                              tt_router=512, tf=256, td=512, tt_combine=256)
    return out.reshape(b, s, d)
```

```python
from typing import NamedTuple

import jax
import jax.numpy as jnp
from jax import lax
from jax.experimental import pallas as pl
from jax.experimental.pallas import tpu as pltpu

N_META = 16
LRU_WIDTH = 2048
LRU_HEADS = 16
HEAD_DIM = 128
LRU_CONV_WIDTH = 4
LRU_C = 8.0
CONF_WIDTH = 2048
CONF_GROUPS = 16
CONF_KERNEL = 31
N_EXPERTS = 32
TOP_K = 4
SWIGLU_ALPHA = 1.702
SWIGLU_LIMIT = 7.0
RMS_EPS = 1e-5
LN_EPS = 1e-5
SQRT_FLOOR = 1e-30

SUBLANES = 8
V7X_VMEM_BYTES = 64 * 1024 * 1024
VMEM_LIMIT = V7X_VMEM_BYTES - 8 * 1024 * 1024

ROW_BLOCK = 128
MATMUL_ROWS = 1536
PASS_ROWS = 1536
BF16 = jnp.bfloat16
F32 = jnp.float32
U32 = jnp.uint32


def _params(*sem):
    return pltpu.CompilerParams(dimension_semantics=sem, vmem_limit_bytes=VMEM_LIMIT)


def _pack_bf16_pair(hi, lo):
    hi_bits = lax.bitcast_convert_type(hi.astype(BF16).astype(F32), U32)
    lo_bits = lax.bitcast_convert_type(lo.astype(BF16).astype(F32), U32)
    return hi_bits | lax.shift_right_logical(lo_bits, jnp.uint32(16))


def _sigmoid(x):
    return 0.5 * jnp.tanh(0.5 * x) + 0.5


def _unpack_bf16_pair(u):
    hi = lax.bitcast_convert_type(u & jnp.uint32(0xFFFF0000), F32)
    lo = lax.bitcast_convert_type(lax.shift_left(u, jnp.uint32(16)), F32)
    return hi, lo


def _rmsnorm_kernel(x_ref, g_ref, o_ref):
    x = x_ref[...]
    ms = jnp.mean(x * x, axis=-1, keepdims=True)
    o_ref[...] = (x * lax.rsqrt(ms + RMS_EPS) * g_ref[...]).astype(o_ref.dtype)


def _rmsnorm(x, g, tm):
    t, d = x.shape
    return pl.pallas_call(
        _rmsnorm_kernel,
        grid=(t // tm,),
        in_specs=[pl.BlockSpec((tm, d), lambda i: (i, 0)),
                  pl.BlockSpec((1, d), lambda i: (0, 0))],
        out_specs=pl.BlockSpec((tm, d), lambda i: (i, 0)),
        out_shape=jax.ShapeDtypeStruct((t, d), BF16),
        compiler_params=_params("parallel"),
        name="rmsnorm1",
    )(x, g.reshape(1, d))


def _inproj_kernel(a_ref, am_ref, w_ref, o_ref, om_ref, wb_ref):
    @pl.when(pl.program_id(1) == 0)
    def _():
        wb_ref[...] = w_ref[...].astype(BF16)
        om_ref[...] = jnp.dot(am_ref[...], wb_ref[...], preferred_element_type=F32)

    o_ref[...] = jnp.dot(a_ref[...], wb_ref[...], preferred_element_type=F32)


def _inproj(a, a_meta, w, tm, tn):
    t, k = a.shape
    n = w.shape[1]
    return pl.pallas_call(
        _inproj_kernel,
        grid=(n // tn, t // tm),
        in_specs=[pl.BlockSpec((tm, k), lambda j, i: (i, 0)),
                  pl.BlockSpec((N_META, k), lambda j, i: (0, 0)),
                  pl.BlockSpec((k, tn), lambda j, i: (0, j))],
        out_specs=[pl.BlockSpec((tm, tn), lambda j, i: (i, j)),
                   pl.BlockSpec((N_META, tn), lambda j, i: (0, j))],
        out_shape=[jax.ShapeDtypeStruct((t, n), F32),
                   jax.ShapeDtypeStruct((N_META, n), F32)],
        scratch_shapes=[pltpu.VMEM((k, tn), BF16)],
        compiler_params=_params("parallel", "arbitrary"),
        name="inproj",
    )(a, a_meta, w)


LRU_CHUNKS = SUBLANES
LRU_CHUNK_PAD = SUBLANES
LRU_SCAN_UNROLL = 8


def _gelu_tanh(x):
    return 0.5 * x * (1.0 + jnp.tanh(0.7978845608028654 * (x + 0.044715 * x * x * x)))


def _lru_kernel(xr_ref, gate_ref, xm_ref, cw_ref, cb_ref, wa_ref, ba_ref, wi_ref,
                bi_ref, lam_ref, o_ref, sf_ref, af_ref, bf_ref, ab_ref, bb_ref, hf_ref, hb_ref):
    s = xr_ref.shape[0]
    st = s + N_META
    pad = SUBLANES
    clen = s // LRU_CHUNKS
    pitch = clen + LRU_CHUNK_PAD
    zeros8 = jnp.zeros((pad, HEAD_DIM), F32)
    sf_ref[0:pad, :] = zeros8
    sf_ref[pad:pad + N_META, :] = xm_ref[...]
    sf_ref[pad + N_META:pad + st, :] = xr_ref[...]
    sf_ref[pad + st:pad + st + pad, :] = zeros8

    def gates(u, d):
        ub = u.astype(BF16)
        r = _sigmoid(jnp.dot(ub, wa_ref[d].astype(BF16), preferred_element_type=F32)
                           + ba_ref[d:d + 1, :])
        i = _sigmoid(jnp.dot(ub, wi_ref[d].astype(BF16), preferred_element_type=F32)
                           + bi_ref[d:d + 1, :])
        lam = lam_ref[d:d + 1, :]
        softplus_neg = jnp.maximum(-lam, 0.0) + jnp.log1p(jnp.exp(-jnp.abs(lam)))
        log_a = (-LRU_C) * r * softplus_neg
        a = jnp.exp(log_a)
        one_minus_a2 = -jnp.tanh(log_a) * (a * a + 1.0)
        root = one_minus_a2 * lax.rsqrt(jnp.maximum(one_minus_a2, SQRT_FLOOR))
        b = root * (i * u)
        return a, b

    uf = cb_ref[0:1, :] + cw_ref[0, 3:4, :] * sf_ref[pl.ds(pad, st), :]
    for j in range(1, LRU_CONV_WIDTH):
        uf = uf + cw_ref[0, 3 - j:4 - j, :] * sf_ref[pl.ds(pad - j, st), :]
    a, b = gates(uf, 0)
    h_meta = b[0:1, :]
    for r in range(1, N_META):
        h_meta = a[r:r + 1, :] * h_meta + b[r:r + 1, :]
    for c in range(LRU_CHUNKS):
        af_ref[c * pitch:c * pitch + clen, :] = a[N_META + c * clen:N_META + (c + 1) * clen, :]
        bf_ref[c * pitch:c * pitch + clen, :] = b[N_META + c * clen:N_META + (c + 1) * clen, :]

    base = pad + N_META
    ub_ = cb_ref[1:2, :] + cw_ref[1, 3:4, :] * sf_ref[pl.ds(base, s), :]
    for j in range(1, LRU_CONV_WIDTH):
        ub_ = ub_ + cw_ref[1, 3 - j:4 - j, :] * sf_ref[pl.ds(base + j, s), :]
    a, b = gates(ub_, 1)
    for c in range(LRU_CHUNKS):
        ab_ref[c * pitch:c * pitch + clen, :] = a[c * clen:(c + 1) * clen, :]
        bb_ref[c * pitch:c * pitch + clen, :] = b[c * clen:(c + 1) * clen, :]

    def step_rows(t):
        return (pl.ds(t, LRU_CHUNKS, stride=pitch),
                pl.ds(clen - 1 - t, LRU_CHUNKS, stride=pitch))

    def local_scan(t, carry):
        hf, pf, hb, pb = carry
        rf, rb = step_rows(t)
        a_f = af_ref[rf, :]
        a_b = ab_ref[rb, :]
        return (a_f * hf + bf_ref[rf, :], a_f * pf, a_b * hb + bb_ref[rb, :], a_b * pb)

    zero = sf_ref[0:SUBLANES, :]
    one = zero + 1.0
    hf, pf, hb, pb = lax.fori_loop(0, clen, local_scan, (zero, one, zero, one),
                                   unroll=LRU_SCAN_UNROLL)

    row = lax.broadcasted_iota(jnp.int32, (LRU_CHUNKS, HEAD_DIM), 0)
    c = h_meta
    start_f = jnp.where(row == 0, c, zero)
    for k in range(1, LRU_CHUNKS):
        c = hf[k - 1:k, :] + pf[k - 1:k, :] * c
        start_f = jnp.where(row == k, c, start_f)
    c = zero[0:1, :]
    start_b = zero
    for k in range(LRU_CHUNKS - 2, -1, -1):
        c = hb[k + 1:k + 2, :] + pb[k + 1:k + 2, :] * c
        start_b = jnp.where(row == k, c, start_b)

    def final_scan(t, carry):
        hf, hb = carry
        rf, rb = step_rows(t)
        hf = af_ref[rf, :] * hf + bf_ref[rf, :]
        hb = ab_ref[rb, :] * hb + bb_ref[rb, :]
        hf_ref[rf, :] = hf
        hb_ref[rb, :] = hb
        return hf, hb

    lax.fori_loop(0, clen, final_scan, (start_f, start_b), unroll=LRU_SCAN_UNROLL)
    for c in range(LRU_CHUNKS):
        hsum = hf_ref[c * pitch:c * pitch + clen, :] + hb_ref[c * pitch:c * pitch + clen, :]
        y = hsum * _gelu_tanh(gate_ref[c * clen:(c + 1) * clen, :])
        o_ref[c * clen:(c + 1) * clen, :] = y.astype(o_ref.dtype)


def _lru_mixer(z3, z_meta, cw, cb, wa, ba, wi, bi, lam):
    b, s, _ = z3.shape
    st = s + N_META
    hd = HEAD_DIM
    col = lambda off: (lambda bi_, h: (bi_, 0, off + h))
    return pl.pallas_call(
        _lru_kernel,
        grid=(b, LRU_HEADS),
        in_specs=[
            pl.BlockSpec((None, s, hd), col(0)),
            pl.BlockSpec((None, s, hd), col(LRU_HEADS)),
            pl.BlockSpec((N_META, hd), lambda bi_, h: (0, h)),
            pl.BlockSpec((2, LRU_CONV_WIDTH, hd), lambda bi_, h: (0, 0, h)),
            pl.BlockSpec((2, hd), lambda bi_, h: (0, h)),
            pl.BlockSpec((2, None, hd, hd), lambda bi_, h: (0, h, 0, 0)),
            pl.BlockSpec((2, hd), lambda bi_, h: (0, h)),
            pl.BlockSpec((2, None, hd, hd), lambda bi_, h: (0, h, 0, 0)),
            pl.BlockSpec((2, hd), lambda bi_, h: (0, h)),
            pl.BlockSpec((2, hd), lambda bi_, h: (0, h)),
        ],
        out_specs=pl.BlockSpec((None, s, hd), lambda bi_, h: (bi_, 0, h)),
        out_shape=jax.ShapeDtypeStruct((b, s, LRU_WIDTH), BF16),
        scratch_shapes=[
            pltpu.VMEM((st + 2 * SUBLANES, hd), F32),
        ] + [pltpu.VMEM((LRU_CHUNKS * (s // LRU_CHUNKS + LRU_CHUNK_PAD), hd), F32)] * 6,
        compiler_params=_params("parallel", "parallel"),
        name="lru_mixer",
    )(z3, z3, z_meta, cw, cb, wa, ba.reshape(2, LRU_WIDTH), wi, bi.reshape(2, LRU_WIDTH), lam)


CONF_CHUNK = 64
CONF_PARTIAL_SUMS = 2


def _conf_kernel(a_ref, b_ref, am_ref, bm_ref, cw_ref, cb_ref, g_ref, be_ref, o_ref,
                 cs_ref, sh_ref):
    s = a_ref.shape[0]
    st = s + N_META
    n_sh = sh_ref.shape[1]
    cs_ref[0:N_META, :] = am_ref[...] * _sigmoid(bm_ref[...])
    cs_ref[N_META:st, :] = a_ref[...] * _sigmoid(b_ref[...])
    cs_ref[st:st + N_META, :] = jnp.zeros((N_META, HEAD_DIM), F32)
    for r in range(1, SUBLANES):
        sh_ref[r - 1] = cs_ref[pl.ds(r, n_sh), :]

    n_chunks = s // CONF_CHUNK

    def conv_chunk(row0):
        parts = [None] * CONF_PARTIAL_SUMS
        for k in range(CONF_KERNEL):
            off = N_META - CONF_KERNEL // 2 + k
            r, q = off % SUBLANES, off // SUBLANES
            rows = pl.ds(row0 + q * SUBLANES, CONF_CHUNK)
            term = cw_ref[k:k + 1, :] * (cs_ref[rows, :] if r == 0 else sh_ref[r - 1, rows, :])
            p = k % CONF_PARTIAL_SUMS
            parts[p] = term if parts[p] is None else parts[p] + term
        return sum(parts[1:], parts[0]) + cb_ref[...]

    def step(j, carry):
        acc_prev, xc_prev = carry
        mean = jnp.mean(acc_prev, axis=-1, keepdims=True)
        var = jnp.mean(xc_prev * xc_prev, axis=-1, keepdims=True)
        acc = conv_chunk(pl.multiple_of(jnp.minimum(j, n_chunks - 1) * CONF_CHUNK, CONF_CHUNK))
        y = xc_prev * lax.rsqrt(var + LN_EPS) * g_ref[...] + be_ref[...]
        out_row = pl.multiple_of(jnp.maximum(j - 2, 0) * CONF_CHUNK, CONF_CHUNK)
        o_ref[pl.ds(out_row, CONF_CHUNK), :] = (y * _sigmoid(y)).astype(o_ref.dtype)
        return acc, acc_prev - mean

    warmup = cs_ref[pl.ds(0, CONF_CHUNK), :]
    lax.fori_loop(0, n_chunks + 2, step, (warmup, warmup))


def _conf_mixer(z3, z_meta, cw, cb, g, be):
    b, s, _ = z3.shape
    st = s + N_META
    hd = HEAD_DIM
    a_off = 2 * LRU_HEADS
    b_off = 2 * LRU_HEADS + CONF_GROUPS
    n_sh = st + N_META - SUBLANES
    vec = pl.BlockSpec((1, hd), lambda bi_, h: (0, h))
    return pl.pallas_call(
        _conf_kernel,
        grid=(b, CONF_GROUPS),
        in_specs=[
            pl.BlockSpec((None, s, hd), lambda bi_, h: (bi_, 0, a_off + h)),
            pl.BlockSpec((None, s, hd), lambda bi_, h: (bi_, 0, b_off + h)),
            pl.BlockSpec((N_META, hd), lambda bi_, h: (0, a_off + h)),
            pl.BlockSpec((N_META, hd), lambda bi_, h: (0, b_off + h)),
            pl.BlockSpec((CONF_KERNEL, hd), lambda bi_, h: (0, h)),
            vec, vec, vec,
        ],
        out_specs=pl.BlockSpec((None, s, hd), lambda bi_, h: (bi_, 0, h)),
        out_shape=jax.ShapeDtypeStruct((b, s, CONF_WIDTH), BF16),
        scratch_shapes=[
            pltpu.VMEM((st + N_META, hd), F32),
            pltpu.VMEM((SUBLANES - 1, n_sh, hd), F32),
        ],
        compiler_params=_params("parallel", "parallel"),
        name="conf_mixer",
    )(z3, z3, z_meta, z_meta, cw, cb.reshape(1, -1), g.reshape(1, -1), be.reshape(1, -1))


def _outproj_kernel(ya_ref, yb_ref, wa_ref, wb_ref, x_ref, o_ref):
    acc = jnp.dot(ya_ref[...], wa_ref[...].astype(BF16), preferred_element_type=F32)
    acc = acc + jnp.dot(yb_ref[...], wb_ref[...].astype(BF16), preferred_element_type=F32)
    o_ref[...] = x_ref[...] + acc


def _outproj(ya, yb, w, x, tm, tn):
    t, k = ya.shape
    n = w.shape[1]
    return pl.pallas_call(
        _outproj_kernel,
        grid=(t // tm, n // tn),
        in_specs=[pl.BlockSpec((tm, k), lambda i, j: (i, 0)),
                  pl.BlockSpec((tm, k), lambda i, j: (i, 0)),
                  pl.BlockSpec((k, tn), lambda i, j: (0, j)),
                  pl.BlockSpec((k, tn), lambda i, j: (1, j)),
                  pl.BlockSpec((tm, tn), lambda i, j: (i, j))],
        out_specs=pl.BlockSpec((tm, tn), lambda i, j: (i, j)),
        out_shape=jax.ShapeDtypeStruct((t, n), F32),
        compiler_params=_params("parallel", "parallel"),
        name="outproj",
    )(ya, yb, w, w, x)


def _router_kernel(x_ref, g_ref, wr_ref, br_ref, hp_ref, idx_ref, gate_ref, rank_ref,
                   cnt_ref, carry_ref):
    tt, d = x_ref.shape
    half = d // 2

    @pl.when(pl.program_id(0) == 0)
    def _():
        carry_ref[...] = jnp.zeros_like(carry_ref)

    x = x_ref[...]
    ms = jnp.mean(x * x, axis=-1, keepdims=True)
    h = x * lax.rsqrt(ms + RMS_EPS) * g_ref[...]
    hp_ref[...] = _pack_bf16_pair(h[:, :half], h[:, half:])

    h_hi = h.astype(BF16)
    h_lo = (h - h_hi.astype(F32)).astype(BF16)
    w = wr_ref[...]
    w_hi = w.astype(BF16)
    w_lo = (w - w_hi.astype(F32)).astype(BF16)
    hi_terms = jnp.dot(h_hi, jnp.concatenate([w_hi, w_lo], axis=1), preferred_element_type=F32)
    logits = (hi_terms[:, :N_EXPERTS] + hi_terms[:, N_EXPERTS:]
              + jnp.dot(h_lo, w_hi, preferred_element_type=F32)) + br_ref[...]

    lane = lax.broadcasted_iota(jnp.int32, (tt, N_EXPERTS), 1)
    lane_k = lax.broadcasted_iota(jnp.int32, (tt, TOP_K), 1)
    work = logits
    vals, sels = [], []
    idx_out = jnp.zeros((tt, TOP_K), jnp.int32)
    for k in range(TOP_K):
        m = jnp.max(work, axis=1, keepdims=True)
        am = jnp.min(jnp.where(work == m, lane, N_EXPERTS), axis=1, keepdims=True)
        sel = lane == am
        vals.append(m)
        sels.append(sel)
        idx_out = jnp.where(lane_k == k, am, idx_out)
        work = jnp.where(sel, -jnp.inf, work)
    idx_ref[...] = idx_out

    exps = [jnp.exp(v - vals[0]) for v in vals]
    denom = exps[0] + exps[1] + exps[2] + exps[3]
    gate_out = jnp.zeros((tt, TOP_K), F32)
    for k in range(TOP_K):
        gate_out = jnp.where(lane_k == k, exps[k] / denom, gate_out)
    gate_ref[...] = gate_out

    onehot = jnp.zeros((tt, N_EXPERTS), F32)
    for sel in sels:
        onehot = onehot + sel.astype(F32)
    r_i = lax.broadcasted_iota(jnp.int32, (tt, tt), 0)
    c_i = lax.broadcasted_iota(jnp.int32, (tt, tt), 1)
    tri = (c_i < r_i).astype(BF16)
    before = jnp.dot(tri, onehot.astype(BF16), preferred_element_type=F32) + carry_ref[...]
    rank_out = jnp.zeros((tt, TOP_K), jnp.int32)
    for k, sel in enumerate(sels):
        rk = jnp.sum(jnp.where(sel, before, 0.0), axis=1, keepdims=True).astype(jnp.int32)
        rank_out = jnp.where(lane_k == k, rk, rank_out)
    rank_ref[...] = rank_out
    carry_ref[...] = carry_ref[...] + jnp.sum(onehot, axis=0, keepdims=True)
    cnt_ref[...] = carry_ref[...].astype(jnp.int32)


def _router(x, g, wr, br, tt):
    t, d = x.shape
    small = lambda dt: jax.ShapeDtypeStruct((t, TOP_K), dt)
    return pl.pallas_call(
        _router_kernel,
        grid=(t // tt,),
        in_specs=[pl.BlockSpec((tt, d), lambda i: (i, 0)),
                  pl.BlockSpec((1, d), lambda i: (0, 0)),
                  pl.BlockSpec((d, N_EXPERTS), lambda i: (0, 0)),
                  pl.BlockSpec((1, N_EXPERTS), lambda i: (0, 0))],
        out_specs=[pl.BlockSpec((tt, d // 2), lambda i: (i, 0)),
                   pl.BlockSpec((tt, TOP_K), lambda i: (i, 0)),
                   pl.BlockSpec((tt, TOP_K), lambda i: (i, 0)),
                   pl.BlockSpec((tt, TOP_K), lambda i: (i, 0)),
                   pl.BlockSpec((1, N_EXPERTS), lambda i: (0, 0))],
        out_shape=[jax.ShapeDtypeStruct((t, d // 2), U32),
                   small(jnp.int32), small(F32), small(jnp.int32),
                   jax.ShapeDtypeStruct((1, N_EXPERTS), jnp.int32)],
        scratch_shapes=[pltpu.VMEM((1, N_EXPERTS), F32)],
        compiler_params=_params("arbitrary"),
        name="router",
    )(x, g.reshape(1, d), wr, br.reshape(1, N_EXPERTS))


def _num_passes(n_assign):
    return N_EXPERTS + n_assign // PASS_ROWS


def _routing_tables(idx, rank, counts, n_pass_max):
    t = idx.shape[0]
    padded = (counts + ROW_BLOCK - 1) // ROW_BLOCK * ROW_BLOCK
    passes_e = (padded + PASS_ROWS - 1) // PASS_ROWS
    pass_end = jnp.cumsum(passes_e)
    pass_start = pass_end - passes_e
    n_pass = pass_end[-1]
    pos = (pass_start[idx] + rank // PASS_ROWS) * PASS_ROWS + rank % PASS_ROWS
    p_ids = jnp.arange(n_pass_max, dtype=jnp.int32)
    live = p_ids < n_pass
    pe = jnp.searchsorted(pass_end, jnp.minimum(p_ids, n_pass - 1), side="right")
    pe = jnp.minimum(pe, N_EXPERTS - 1).astype(jnp.int32)
    done = (p_ids - pass_start[pe]) * PASS_ROWS
    rows = jnp.clip(padded[pe] - done, 0, PASS_ROWS)
    pass_nb = jnp.where(live, rows // ROW_BLOCK, 0).astype(jnp.int32)
    pass_valid = jnp.where(live, jnp.clip(counts[pe] - done, 0, PASS_ROWS), 0).astype(jnp.int32)
    return (pe, pass_nb, pass_valid, n_pass.reshape(1).astype(jnp.int32),
            pos.T.reshape(-1).astype(jnp.int32))


GATHER_UNROLL = SUBLANES


def _gather_rows(src_hbm, idx_ref, idx_base, n_rows, dst_ref, sem):
    def issue(g, _):
        r0 = pl.multiple_of(g * GATHER_UNROLL, GATHER_UNROLL)
        dst_tile = dst_ref.at[pl.ds(r0, GATHER_UNROLL)]
        for u in range(GATHER_UNROLL):
            pltpu.make_async_copy(src_hbm.at[pl.ds(idx_ref[idx_base + r0 + u], 1)],
                                  dst_tile.at[pl.ds(u, 1)], sem).start()
        return 0

    lax.fori_loop(0, n_rows // GATHER_UNROLL, issue, 0)


def _for_row_blocks(n_blocks, body):
    per_big = MATMUL_ROWS // ROW_BLOCK
    n_big = n_blocks // per_big

    def big(i, _):
        body(pl.multiple_of(i * MATMUL_ROWS, MATMUL_ROWS), MATMUL_ROWS)
        return 0

    lax.fori_loop(0, n_big, big, 0)
    rest = n_blocks - n_big * per_big
    start = pl.multiple_of(n_big * MATMUL_ROWS, MATMUL_ROWS)
    for m in range(1, per_big):
        @pl.when(rest == m)
        def _(m=m):
            body(start, m * ROW_BLOCK)


def _wait_row_blocks(src_hbm, dst_ref, n_blocks, rows, sem):
    def drain(i, _):
        pltpu.make_async_copy(src_hbm.at[pl.ds(0, rows)], dst_ref.at[pl.ds(0, rows)], sem).wait()
        return 0

    lax.fori_loop(0, n_blocks, drain, 0)


def _build_token_table(pos_ref, nb_ref, valid_ref, n_pass, cap, tok_ref):
    def pad_pass(q, _):
        def pad_row(r, _):
            tok_ref[q * cap + r] = 0
            return 0

        lax.fori_loop(valid_ref[q], nb_ref[q] * ROW_BLOCK, pad_row, 0)
        return 0

    lax.fori_loop(0, n_pass, pad_pass, 0)
    n_tok = pos_ref.shape[0] // TOP_K
    for k in range(TOP_K):
        def place(g, _, k=k):
            for u in range(GATHER_UNROLL):
                t = g * GATHER_UNROLL + u
                tok_ref[pos_ref[k * n_tok + t]] = t
            return 0

        lax.fori_loop(0, n_tok // GATHER_UNROLL, place, 0)


def _moe_up_kernel(pe_ref, nb_ref, valid_ref, npass_ref, pos_ref, hp_hbm, wg_ref, wu_ref, bg_ref,
                   bu_ref, o_ref, gbuf_ref, xb_ref, tok_ref, sem):
    p = pl.program_id(0)
    f = pl.program_id(1)
    cap, half = gbuf_ref.shape
    n_pass = npass_ref[0]
    live = p < n_pass
    nb = nb_ref[p]

    @pl.when(jnp.logical_and(live, f == 0))
    def _():
        @pl.when(p == 0)
        def _():
            _build_token_table(pos_ref, nb_ref, valid_ref, n_pass, cap, tok_ref)
            _gather_rows(hp_hbm, tok_ref, 0, nb * ROW_BLOCK, gbuf_ref, sem)

        _wait_row_blocks(hp_hbm, gbuf_ref, nb, ROW_BLOCK, sem)

        def unpack(rb, _):
            rows = pl.ds(pl.multiple_of(rb * ROW_BLOCK, ROW_BLOCK), ROW_BLOCK)
            hi, lo = _unpack_bf16_pair(gbuf_ref[rows, :])
            xb_ref[rows, :half] = hi.astype(BF16)
            xb_ref[rows, half:] = lo.astype(BF16)
            return 0

        lax.fori_loop(0, nb, unpack, 0)

        @pl.when(p + 1 < n_pass)
        def _():
            _gather_rows(hp_hbm, tok_ref, (p + 1) * cap, nb_ref[p + 1] * ROW_BLOCK, gbuf_ref, sem)

    @pl.when(live)
    def _():
        def block(row0, m):
            rows = pl.ds(row0, m)
            x = xb_ref[rows, :]
            hg = jnp.dot(x, wg_ref[...].astype(BF16), preferred_element_type=F32) + bg_ref[...]
            hu = jnp.dot(x, wu_ref[...].astype(BF16), preferred_element_type=F32) + bu_ref[...]
            hg = jnp.minimum(hg, SWIGLU_LIMIT)
            hu = jnp.clip(hu, -SWIGLU_LIMIT, SWIGLU_LIMIT)
            act = hg * _sigmoid(SWIGLU_ALPHA * hg) * (hu + 1.0)
            o_ref[rows, :] = act.astype(o_ref.dtype)

        _for_row_blocks(nb, block)


def _moe_up(pe, pass_nb, pass_valid, n_pass, pos_flat, hp, wg, wu, bg, bu, n_pass_max, tf):
    e, d, f = wg.shape
    nf = f // tf
    cap = PASS_ROWS

    def w_map(p, j, pe_, nb_, valid_, np_, pos_):
        return (pe_[p], 0, jnp.where(p < np_[0], j, nf - 1))

    def o_map(p, j, pe_, nb_, valid_, np_, pos_):
        ok = p < np_[0]
        return (jnp.where(ok, p, np_[0] - 1), jnp.where(ok, j, nf - 1))

    grid_spec = pltpu.PrefetchScalarGridSpec(
        num_scalar_prefetch=5,
        grid=(n_pass_max, nf),
        in_specs=[pl.BlockSpec(memory_space=pl.ANY),
                  pl.BlockSpec((None, d, tf), w_map),
                  pl.BlockSpec((None, d, tf), w_map),
                  pl.BlockSpec((None, 1, tf), w_map),
                  pl.BlockSpec((None, 1, tf), w_map)],
        out_specs=pl.BlockSpec((cap, tf), o_map),
        scratch_shapes=[pltpu.VMEM((cap, d // 2), U32),
                        pltpu.VMEM((cap, d), BF16),
                        pltpu.SMEM((n_pass_max * cap,), jnp.int32),
                        pltpu.SemaphoreType.DMA(())],
    )
    return pl.pallas_call(
        _moe_up_kernel,
        grid_spec=grid_spec,
        out_shape=jax.ShapeDtypeStruct((n_pass_max * cap, f), BF16),
        compiler_params=_params("arbitrary", "arbitrary"),
        name="moe_up",
    )(pe, pass_nb, pass_valid, n_pass, pos_flat, hp, wg, wu, bg.reshape(e, 1, f),
      bu.reshape(e, 1, f))


def _moe_down_kernel(pe_ref, nb_ref, npass_ref, a_ref, wh_ref, wl_ref, bh_ref, bl_ref, o_ref):
    p = pl.program_id(0)

    @pl.when(p < npass_ref[0])
    def _():
        def block(row0, m):
            rows = pl.ds(row0, m)
            a = a_ref[rows, :]
            hi = jnp.dot(a, wh_ref[...].astype(BF16), preferred_element_type=F32) + bh_ref[...]
            lo = jnp.dot(a, wl_ref[...].astype(BF16), preferred_element_type=F32) + bl_ref[...]
            o_ref[rows, :] = _pack_bf16_pair(hi, lo)

        _for_row_blocks(nb_ref[p], block)


def _moe_down(pe, pass_nb, n_pass, act, wd, bd, n_pass_max, td):
    e, f, d = wd.shape
    nd = d // 2 // td
    cap = PASS_ROWS

    def col(p, j, np_):
        return jnp.where(p < np_[0], j, nd - 1)

    def row(p, np_):
        return jnp.where(p < np_[0], p, np_[0] - 1)

    grid_spec = pltpu.PrefetchScalarGridSpec(
        num_scalar_prefetch=3,
        grid=(n_pass_max, nd),
        in_specs=[pl.BlockSpec((cap, f), lambda p, j, pe_, nb_, np_: (row(p, np_), 0)),
                  pl.BlockSpec((None, f, td), lambda p, j, pe_, nb_, np_: (pe_[p], 0, col(p, j, np_))),
                  pl.BlockSpec((None, f, td), lambda p, j, pe_, nb_, np_: (pe_[p], 0, nd + col(p, j, np_))),
                  pl.BlockSpec((None, 1, td), lambda p, j, pe_, nb_, np_: (pe_[p], 0, col(p, j, np_))),
                  pl.BlockSpec((None, 1, td), lambda p, j, pe_, nb_, np_: (pe_[p], 0, nd + col(p, j, np_)))],
        out_specs=pl.BlockSpec((cap, td), lambda p, j, pe_, nb_, np_: (row(p, np_), col(p, j, np_))),
    )
    bd3 = bd.reshape(e, 1, d)
    return pl.pallas_call(
        _moe_down_kernel,
        grid_spec=grid_spec,
        out_shape=jax.ShapeDtypeStruct((n_pass_max * cap, d // 2), U32),
        compiler_params=_params("arbitrary", "arbitrary"),
        name="moe_down",
    )(pe, pass_nb, n_pass, act, wd, wd, bd3, bd3)


def _combine_kernel(pos_ref, y_hbm, x_ref, gate_ref, g_ref, o_ref, ybuf_ref, sems):
    i = pl.program_id(0)
    n = pl.num_programs(0)
    tt, d = x_ref.shape
    half = d // 2
    slot = i % 2

    n_tok = n * tt

    def start_tile(step, slot_):
        for k in range(TOP_K):
            _gather_rows(y_hbm, pos_ref, k * n_tok + step * tt, tt, ybuf_ref.at[slot_, k],
                         sems.at[slot_])

    @pl.when(i == 0)
    def _():
        start_tile(0, 0)

    @pl.when(i + 1 < n)
    def _():
        start_tile(i + 1, 1 - slot)

    for k in range(TOP_K):
        pltpu.make_async_copy(y_hbm.at[pl.ds(0, tt)], ybuf_ref.at[slot, k], sems.at[slot]).wait()

    gates = gate_ref[...]
    x = x_ref[...]
    acc_hi = x[:, :half]
    acc_lo = x[:, half:]
    for k in range(TOP_K):
        hi, lo = _unpack_bf16_pair(ybuf_ref[slot, k])
        acc_hi = acc_hi + gates[:, k:k + 1] * hi
        acc_lo = acc_lo + gates[:, k:k + 1] * lo
    ms = (jnp.sum(acc_hi * acc_hi, axis=-1, keepdims=True)
          + jnp.sum(acc_lo * acc_lo, axis=-1, keepdims=True)) * (1.0 / d)
    scale = lax.rsqrt(ms + RMS_EPS)
    o_ref[:, :half] = acc_hi * scale * g_ref[:, :half]
    o_ref[:, half:] = acc_lo * scale * g_ref[:, half:]


def _combine(pos_flat, y, x, gates, g, tt):
    t, d = x.shape
    grid_spec = pltpu.PrefetchScalarGridSpec(
        num_scalar_prefetch=1,
        grid=(t // tt,),
        in_specs=[pl.BlockSpec(memory_space=pl.ANY),
                  pl.BlockSpec((tt, d), lambda i, p: (i, 0)),
                  pl.BlockSpec((tt, TOP_K), lambda i, p: (i, 0)),
                  pl.BlockSpec((1, d), lambda i, p: (0, 0))],
        out_specs=pl.BlockSpec((tt, d), lambda i, p: (i, 0)),
        scratch_shapes=[pltpu.VMEM((2, TOP_K, tt, d // 2), U32),
                        pltpu.SemaphoreType.DMA((2,))],
    )
    return pl.pallas_call(
        _combine_kernel,
        grid_spec=grid_spec,
        out_shape=jax.ShapeDtypeStruct((t, d), F32),
        compiler_params=_params("arbitrary"),
        name="combine",
    )(pos_flat, y, x, gates, g.reshape(1, d))


class _Tiles(NamedTuple):
    norm_rows: int = 512
    inproj: tuple = (1024, 512)
    outproj: tuple = (1024, 512)
    router_rows: int = 512
    up_cols: int = 256
    down_cols: int = 512
    combine_rows: int = 256


TILES = _Tiles()


def _moe_and_final_norm(x_mid, norm2_g, w_router, b_router, w_gate, b_gate, w_up, b_up,
                        w_down, b_down, final_norm_g):
    t = x_mid.shape[0]
    n_pass_max = _num_passes(t * TOP_K)
    hp, idx, gates, rank, counts = _router(x_mid, norm2_g, w_router, b_router,
                                           tt=TILES.router_rows)
    pe, pass_nb, pass_valid, n_pass, pos_flat = _routing_tables(idx, rank, counts[0], n_pass_max)
    act = _moe_up(pe, pass_nb, pass_valid, n_pass, pos_flat, hp, w_gate, w_up, b_gate, b_up,
                  n_pass_max, tf=TILES.up_cols)
    y_rows = _moe_down(pe, pass_nb, n_pass, act, w_down, b_down, n_pass_max, td=TILES.down_cols)
    return _combine(pos_flat, y_rows, x_mid, gates, final_norm_g, tt=TILES.combine_rows)


def kernel(x, meta_tokens, norm1_g, w_in, lru_conv_w, lru_conv_b, lru_w_a, lru_b_a, lru_w_i, lru_b_i, lru_lambda, conf_conv_w, conf_conv_b, conf_norm_g, conf_norm_b, w_out, norm2_g, w_router, b_router, w_gate, b_gate, w_up, b_up, w_down, b_down, final_norm_g):
    b, s, d = x.shape
    t = b * s
    assert norm1_g.shape[0] == 1, "one layer"
    assert d == LRU_WIDTH + CONF_WIDTH == LRU_HEADS * HEAD_DIM + CONF_GROUPS * HEAD_DIM
    assert s % (LRU_CHUNKS * SUBLANES) == 0 and s % CONF_CHUNK == 0
    assert all(t % rows == 0 for rows in (TILES.norm_rows, TILES.inproj[0], TILES.outproj[0],
                                          TILES.router_rows, TILES.combine_rows))
    x2 = x.reshape(t, d)

    h = _rmsnorm(x2, norm1_g[0], tm=TILES.norm_rows)
    h_meta = _rmsnorm(meta_tokens.astype(x.dtype), norm1_g[0], tm=N_META)
    z, z_meta = _inproj(h, h_meta, w_in[0], *TILES.inproj)
    z3 = z.reshape(b, s, -1)

    y_lru = _lru_mixer(z3, z_meta, lru_conv_w[0], lru_conv_b[0], lru_w_a[0], lru_b_a[0],
                       lru_w_i[0], lru_b_i[0], lru_lambda[0])
    y_conf = _conf_mixer(z3, z_meta, conf_conv_w[0], conf_conv_b[0], conf_norm_g[0],
                         conf_norm_b[0])
    x_mid = _outproj(y_lru.reshape(t, -1), y_conf.reshape(t, -1), w_out[0], x2, *TILES.outproj)

    out = _moe_and_final_norm(x_mid, norm2_g[0], w_router[0], b_router[0], w_gate[0], b_gate[0],
                              w_up[0], b_up[0], w_down[0], b_down[0], final_norm_g)
    return out.reshape(b, s, d)
```

```python
from typing import NamedTuple

import jax
import jax.numpy as jnp
from jax import lax
from jax.experimental import pallas as pl
from jax.experimental.pallas import tpu as pltpu

N_META = 16
LRU_WIDTH = 2048
LRU_HEADS = 16
HEAD_DIM = 128
LRU_CONV_WIDTH = 4
LRU_C = 8.0
CONF_WIDTH = 2048
CONF_GROUPS = 16
CONF_KERNEL = 31
N_EXPERTS = 32
TOP_K = 4
SWIGLU_ALPHA = 1.702
SWIGLU_LIMIT = 7.0
RMS_EPS = 1e-5
LN_EPS = 1e-5
SQRT_FLOOR = 1e-30

SUBLANES = 8
V7X_VMEM_BYTES = 64 * 1024 * 1024
VMEM_LIMIT = V7X_VMEM_BYTES - 6 * 1024 * 1024

ROW_BLOCK = 128
MATMUL_ROWS = 1536
PASS_ROWS = 1536
BF16 = jnp.bfloat16
F32 = jnp.float32
U32 = jnp.uint32


def _params(*sem):
    return pltpu.CompilerParams(dimension_semantics=sem, vmem_limit_bytes=VMEM_LIMIT)


def _pack_bf16_pair(hi, lo):
    hi_bits = lax.bitcast_convert_type(hi.astype(BF16).astype(F32), U32)
    lo_bits = lax.bitcast_convert_type(lo.astype(BF16).astype(F32), U32)
    return hi_bits | lax.shift_right_logical(lo_bits, jnp.uint32(16))


def _sigmoid(x):
    return 0.5 * jnp.tanh(0.5 * x) + 0.5


def _unpack_bf16_pair(u):
    hi = lax.bitcast_convert_type(u & jnp.uint32(0xFFFF0000), F32)
    lo = lax.bitcast_convert_type(lax.shift_left(u, jnp.uint32(16)), F32)
    return hi, lo


def _rmsnorm_kernel(x_ref, g_ref, o_ref):
    x = x_ref[...]
    ms = jnp.mean(x * x, axis=-1, keepdims=True)
    o_ref[...] = (x * lax.rsqrt(ms + RMS_EPS) * g_ref[...]).astype(o_ref.dtype)


def _rmsnorm(x, g, tm):
    t, d = x.shape
    return pl.pallas_call(
        _rmsnorm_kernel,
        grid=(t // tm,),
        in_specs=[pl.BlockSpec((tm, d), lambda i: (i, 0)),
                  pl.BlockSpec((1, d), lambda i: (0, 0))],
        out_specs=pl.BlockSpec((tm, d), lambda i: (i, 0)),
        out_shape=jax.ShapeDtypeStruct((t, d), BF16),
        compiler_params=_params("parallel"),
        name="rmsnorm1",
    )(x, g.reshape(1, d))


def _inproj_kernel(a_ref, am_ref, w_ref, o_ref, om_ref, wb_ref):
    @pl.when(pl.program_id(1) == 0)
    def _():
        wb_ref[...] = w_ref[...].astype(BF16)
        om_ref[...] = jnp.dot(am_ref[...], wb_ref[...], preferred_element_type=F32)

    o_ref[...] = jnp.dot(a_ref[...], wb_ref[...], preferred_element_type=F32)


def _inproj(a, a_meta, w, tm, tn):
    t, k = a.shape
    n = w.shape[1]
    return pl.pallas_call(
        _inproj_kernel,
        grid=(n // tn, t // tm),
        in_specs=[pl.BlockSpec((tm, k), lambda j, i: (i, 0)),
                  pl.BlockSpec((N_META, k), lambda j, i: (0, 0)),
                  pl.BlockSpec((k, tn), lambda j, i: (0, j))],
        out_specs=[pl.BlockSpec((tm, tn), lambda j, i: (i, j)),
                   pl.BlockSpec((N_META, tn), lambda j, i: (0, j))],
        out_shape=[jax.ShapeDtypeStruct((t, n), F32),
                   jax.ShapeDtypeStruct((N_META, n), F32)],
        scratch_shapes=[pltpu.VMEM((k, tn), BF16)],
        compiler_params=_params("parallel", "arbitrary"),
        name="inproj",
    )(a, a_meta, w)


LRU_CHUNKS = SUBLANES
LRU_CHUNK_PAD = SUBLANES
LRU_SCAN_UNROLL = 8


def _gelu_tanh(x):
    return 0.5 * x * (1.0 + jnp.tanh(0.7978845608028654 * (x + 0.044715 * x * x * x)))


def _lru_kernel(xr_ref, gate_ref, xm_ref, cw_ref, cb_ref, wa_ref, ba_ref, wi_ref,
                bi_ref, lam_ref, o_ref, sf_ref, af_ref, bf_ref, ab_ref, bb_ref, hf_ref, hb_ref):
    s = xr_ref.shape[0]
    st = s + N_META
    pad = SUBLANES
    clen = s // LRU_CHUNKS
    pitch = clen + LRU_CHUNK_PAD
    zeros8 = jnp.zeros((pad, HEAD_DIM), F32)
    sf_ref[0:pad, :] = zeros8
    sf_ref[pad:pad + N_META, :] = xm_ref[...]
    sf_ref[pad + N_META:pad + st, :] = xr_ref[...]
    sf_ref[pad + st:pad + st + pad, :] = zeros8

    def gates(u, d):
        ub = u.astype(BF16)
        t_r = jnp.tanh(jnp.dot(ub, (0.5 * wa_ref[d]).astype(BF16), preferred_element_type=F32)
                       + 0.5 * ba_ref[d:d + 1, :])
        t_i = jnp.tanh(jnp.dot(ub, (0.5 * wi_ref[d]).astype(BF16), preferred_element_type=F32)
                       + 0.5 * bi_ref[d:d + 1, :])
        lam = lam_ref[d:d + 1, :]
        softplus_neg = jnp.maximum(-lam, 0.0) + jnp.log1p(jnp.exp(-jnp.abs(lam)))
        half_c = (-0.5 * LRU_C) * softplus_neg
        log_a = half_c * t_r + half_c
        a = jnp.exp(log_a)
        one_minus_a2 = -jnp.tanh(log_a) * (a * a + 1.0)
        root = one_minus_a2 * lax.rsqrt(jnp.maximum(one_minus_a2, SQRT_FLOOR))
        half_u = 0.5 * u
        b = root * (half_u * t_i + half_u)
        return a, b

    uf = cb_ref[0:1, :] + cw_ref[0, 3:4, :] * sf_ref[pl.ds(pad, st), :]
    for j in range(1, LRU_CONV_WIDTH):
        uf = uf + cw_ref[0, 3 - j:4 - j, :] * sf_ref[pl.ds(pad - j, st), :]
    a, b = gates(uf, 0)
    h_meta = b[0:1, :]
    for r in range(1, N_META):
        h_meta = a[r:r + 1, :] * h_meta + b[r:r + 1, :]
    for c in range(LRU_CHUNKS):
        af_ref[c * pitch:c * pitch + clen, :] = a[N_META + c * clen:N_META + (c + 1) * clen, :]
        bf_ref[c * pitch:c * pitch + clen, :] = b[N_META + c * clen:N_META + (c + 1) * clen, :]

    base = pad + N_META
    ub_ = cb_ref[1:2, :] + cw_ref[1, 3:4, :] * sf_ref[pl.ds(base, s), :]
    for j in range(1, LRU_CONV_WIDTH):
        ub_ = ub_ + cw_ref[1, 3 - j:4 - j, :] * sf_ref[pl.ds(base + j, s), :]
    a, b = gates(ub_, 1)
    for c in range(LRU_CHUNKS):
        ab_ref[c * pitch:c * pitch + clen, :] = a[c * clen:(c + 1) * clen, :]
        bb_ref[c * pitch:c * pitch + clen, :] = b[c * clen:(c + 1) * clen, :]

    def step_rows(t):
        return (pl.ds(t, LRU_CHUNKS, stride=pitch),
                pl.ds(clen - 1 - t, LRU_CHUNKS, stride=pitch))

    def local_scan(t, carry):
        hf, pf, hb, pb = carry
        rf, rb = step_rows(t)
        a_f = af_ref[rf, :]
        a_b = ab_ref[rb, :]
        return (a_f * hf + bf_ref[rf, :], a_f * pf, a_b * hb + bb_ref[rb, :], a_b * pb)

    zero = sf_ref[0:SUBLANES, :]
    one = zero + 1.0
    hf, pf, hb, pb = lax.fori_loop(0, clen, local_scan, (zero, one, zero, one),
                                   unroll=LRU_SCAN_UNROLL)

    row = lax.broadcasted_iota(jnp.int32, (LRU_CHUNKS, HEAD_DIM), 0)
    c = h_meta
    start_f = jnp.where(row == 0, c, zero)
    for k in range(1, LRU_CHUNKS):
        c = hf[k - 1:k, :] + pf[k - 1:k, :] * c
        start_f = jnp.where(row == k, c, start_f)
    c = zero[0:1, :]
    start_b = zero
    for k in range(LRU_CHUNKS - 2, -1, -1):
        c = hb[k + 1:k + 2, :] + pb[k + 1:k + 2, :] * c
        start_b = jnp.where(row == k, c, start_b)

    def final_scan(t, carry):
        hf, hb = carry
        rf, rb = step_rows(t)
        hf = af_ref[rf, :] * hf + bf_ref[rf, :]
        hb = ab_ref[rb, :] * hb + bb_ref[rb, :]
        hf_ref[rf, :] = hf
        hb_ref[rb, :] = hb
        return hf, hb

    lax.fori_loop(0, clen, final_scan, (start_f, start_b), unroll=LRU_SCAN_UNROLL)
    for c in range(LRU_CHUNKS):
        hsum = hf_ref[c * pitch:c * pitch + clen, :] + hb_ref[c * pitch:c * pitch + clen, :]
        y = hsum * _gelu_tanh(gate_ref[c * clen:(c + 1) * clen, :])
        o_ref[c * clen:(c + 1) * clen, :] = y.astype(o_ref.dtype)


def _lru_mixer(z3, z_meta, cw, cb, wa, ba, wi, bi, lam):
    b, s, _ = z3.shape
    st = s + N_META
    hd = HEAD_DIM
    col = lambda off: (lambda bi_, h: (bi_, 0, off + h))
    return pl.pallas_call(
        _lru_kernel,
        grid=(b, LRU_HEADS),
        in_specs=[
            pl.BlockSpec((None, s, hd), col(0)),
            pl.BlockSpec((None, s, hd), col(LRU_HEADS)),
            pl.BlockSpec((N_META, hd), lambda bi_, h: (0, h)),
            pl.BlockSpec((2, LRU_CONV_WIDTH, hd), lambda bi_, h: (0, 0, h)),
            pl.BlockSpec((2, hd), lambda bi_, h: (0, h)),
            pl.BlockSpec((2, None, hd, hd), lambda bi_, h: (0, h, 0, 0)),
            pl.BlockSpec((2, hd), lambda bi_, h: (0, h)),
            pl.BlockSpec((2, None, hd, hd), lambda bi_, h: (0, h, 0, 0)),
            pl.BlockSpec((2, hd), lambda bi_, h: (0, h)),
            pl.BlockSpec((2, hd), lambda bi_, h: (0, h)),
        ],
        out_specs=pl.BlockSpec((None, s, hd), lambda bi_, h: (bi_, 0, h)),
        out_shape=jax.ShapeDtypeStruct((b, s, LRU_WIDTH), BF16),
        scratch_shapes=[
            pltpu.VMEM((st + 2 * SUBLANES, hd), F32),
        ] + [pltpu.VMEM((LRU_CHUNKS * (s // LRU_CHUNKS + LRU_CHUNK_PAD), hd), F32)] * 6,
        compiler_params=_params("parallel", "parallel"),
        name="lru_mixer",
    )(z3, z3, z_meta, cw, cb, wa, ba.reshape(2, LRU_WIDTH), wi, bi.reshape(2, LRU_WIDTH), lam)


CONF_CHUNK = 64
CONF_PARTIAL_SUMS = 2


def _conf_kernel(a_ref, b_ref, am_ref, bm_ref, cw_ref, cb_ref, g_ref, be_ref, o_ref,
                 cs_ref, sh_ref):
    s = a_ref.shape[0]
    st = s + N_META
    n_sh = sh_ref.shape[1]
    cs_ref[0:N_META, :] = am_ref[...] * _sigmoid(bm_ref[...])
    cs_ref[N_META:st, :] = a_ref[...] * _sigmoid(b_ref[...])
    cs_ref[st:st + N_META, :] = jnp.zeros((N_META, HEAD_DIM), F32)
    for r in range(1, SUBLANES):
        sh_ref[r - 1] = cs_ref[pl.ds(r, n_sh), :]

    n_chunks = s // CONF_CHUNK

    def conv_chunk(row0):
        parts = [None] * CONF_PARTIAL_SUMS
        for k in range(CONF_KERNEL):
            off = N_META - CONF_KERNEL // 2 + k
            r, q = off % SUBLANES, off // SUBLANES
            rows = pl.ds(row0 + q * SUBLANES, CONF_CHUNK)
            term = cw_ref[k:k + 1, :] * (cs_ref[rows, :] if r == 0 else sh_ref[r - 1, rows, :])
            p = k % CONF_PARTIAL_SUMS
            parts[p] = term if parts[p] is None else parts[p] + term
        return sum(parts[1:], parts[0]) + cb_ref[...]

    def step(j, carry):
        acc_prev, xc_prev = carry
        mean = jnp.mean(acc_prev, axis=-1, keepdims=True)
        var = jnp.mean(xc_prev * xc_prev, axis=-1, keepdims=True)
        acc = conv_chunk(pl.multiple_of(jnp.minimum(j, n_chunks - 1) * CONF_CHUNK, CONF_CHUNK))
        y = xc_prev * lax.rsqrt(var + LN_EPS) * g_ref[...] + be_ref[...]
        out_row = pl.multiple_of(jnp.maximum(j - 2, 0) * CONF_CHUNK, CONF_CHUNK)
        half_y = 0.5 * y
        o_ref[pl.ds(out_row, CONF_CHUNK), :] = (half_y * jnp.tanh(half_y) + half_y).astype(o_ref.dtype)
        return acc, acc_prev - mean

    warmup = cs_ref[pl.ds(0, CONF_CHUNK), :]
    lax.fori_loop(0, n_chunks + 2, step, (warmup, warmup))


def _conf_mixer(z3, z_meta, cw, cb, g, be):
    b, s, _ = z3.shape
    st = s + N_META
    hd = HEAD_DIM
    a_off = 2 * LRU_HEADS
    b_off = 2 * LRU_HEADS + CONF_GROUPS
    n_sh = st + N_META - SUBLANES
    vec = pl.BlockSpec((1, hd), lambda bi_, h: (0, h))
    return pl.pallas_call(
        _conf_kernel,
        grid=(b, CONF_GROUPS),
        in_specs=[
            pl.BlockSpec((None, s, hd), lambda bi_, h: (bi_, 0, a_off + h)),
            pl.BlockSpec((None, s, hd), lambda bi_, h: (bi_, 0, b_off + h)),
            pl.BlockSpec((N_META, hd), lambda bi_, h: (0, a_off + h)),
            pl.BlockSpec((N_META, hd), lambda bi_, h: (0, b_off + h)),
            pl.BlockSpec((CONF_KERNEL, hd), lambda bi_, h: (0, h)),
            vec, vec, vec,
        ],
        out_specs=pl.BlockSpec((None, s, hd), lambda bi_, h: (bi_, 0, h)),
        out_shape=jax.ShapeDtypeStruct((b, s, CONF_WIDTH), BF16),
        scratch_shapes=[
            pltpu.VMEM((st + N_META, hd), F32),
            pltpu.VMEM((SUBLANES - 1, n_sh, hd), F32),
        ],
        compiler_params=_params("parallel", "parallel"),
        name="conf_mixer",
    )(z3, z3, z_meta, z_meta, cw, cb.reshape(1, -1), g.reshape(1, -1), be.reshape(1, -1))


def _outproj_kernel(ya_ref, yb_ref, wa_ref, wb_ref, x_ref, o_ref):
    acc = jnp.dot(ya_ref[...], wa_ref[...].astype(BF16), preferred_element_type=F32)
    acc = acc + jnp.dot(yb_ref[...], wb_ref[...].astype(BF16), preferred_element_type=F32)
    o_ref[...] = x_ref[...] + acc


def _outproj(ya, yb, w, x, tm, tn):
    t, k = ya.shape
    n = w.shape[1]
    return pl.pallas_call(
        _outproj_kernel,
        grid=(t // tm, n // tn),
        in_specs=[pl.BlockSpec((tm, k), lambda i, j: (i, 0)),
                  pl.BlockSpec((tm, k), lambda i, j: (i, 0)),
                  pl.BlockSpec((k, tn), lambda i, j: (0, j)),
                  pl.BlockSpec((k, tn), lambda i, j: (1, j)),
                  pl.BlockSpec((tm, tn), lambda i, j: (i, j))],
        out_specs=pl.BlockSpec((tm, tn), lambda i, j: (i, j)),
        out_shape=jax.ShapeDtypeStruct((t, n), F32),
        compiler_params=_params("parallel", "parallel"),
        name="outproj",
    )(ya, yb, w, w, x)


def _router_kernel(x_ref, g_ref, wr_ref, br_ref, hp_ref, idx_ref, gate_ref, rank_ref,
                   cnt_ref, carry_ref):
    tt, d = x_ref.shape
    half = d // 2

    @pl.when(pl.program_id(0) == 0)
    def _():
        carry_ref[...] = jnp.zeros_like(carry_ref)

    x = x_ref[...]
    ms = jnp.mean(x * x, axis=-1, keepdims=True)
    h = x * lax.rsqrt(ms + RMS_EPS) * g_ref[...]
    hp_ref[...] = _pack_bf16_pair(h[:, :half], h[:, half:])

    h_hi = h.astype(BF16)
    h_lo = (h - h_hi.astype(F32)).astype(BF16)
    w = wr_ref[...]
    w_hi = w.astype(BF16)
    w_lo = (w - w_hi.astype(F32)).astype(BF16)
    hi_terms = jnp.dot(h_hi, jnp.concatenate([w_hi, w_lo], axis=1), preferred_element_type=F32)
    logits = (hi_terms[:, :N_EXPERTS] + hi_terms[:, N_EXPERTS:]
              + jnp.dot(h_lo, w_hi, preferred_element_type=F32)) + br_ref[...]

    lane = lax.broadcasted_iota(jnp.int32, (tt, N_EXPERTS), 1)
    lane_k = lax.broadcasted_iota(jnp.int32, (tt, TOP_K), 1)
    work = logits
    vals, sels = [], []
    idx_out = jnp.zeros((tt, TOP_K), jnp.int32)
    for k in range(TOP_K):
        m = jnp.max(work, axis=1, keepdims=True)
        am = jnp.min(jnp.where(work == m, lane, N_EXPERTS), axis=1, keepdims=True)
        sel = lane == am
        vals.append(m)
        sels.append(sel)
        idx_out = jnp.where(lane_k == k, am, idx_out)
        work = jnp.where(sel, -jnp.inf, work)
    idx_ref[...] = idx_out

    exps = [jnp.exp(v - vals[0]) for v in vals]
    denom = exps[0] + exps[1] + exps[2] + exps[3]
    gate_out = jnp.zeros((tt, TOP_K), F32)
    for k in range(TOP_K):
        gate_out = jnp.where(lane_k == k, exps[k] / denom, gate_out)
    gate_ref[...] = gate_out

    onehot = jnp.zeros((tt, N_EXPERTS), F32)
    for sel in sels:
        onehot = onehot + sel.astype(F32)
    r_i = lax.broadcasted_iota(jnp.int32, (tt, tt), 0)
    c_i = lax.broadcasted_iota(jnp.int32, (tt, tt), 1)
    tri = (c_i < r_i).astype(BF16)
    before = jnp.dot(tri, onehot.astype(BF16), preferred_element_type=F32) + carry_ref[...]
    rank_out = jnp.zeros((tt, TOP_K), jnp.int32)
    for k, sel in enumerate(sels):
        rk = jnp.sum(jnp.where(sel, before, 0.0), axis=1, keepdims=True).astype(jnp.int32)
        rank_out = jnp.where(lane_k == k, rk, rank_out)
    rank_ref[...] = rank_out
    carry_ref[...] = carry_ref[...] + jnp.sum(onehot, axis=0, keepdims=True)
    cnt_ref[...] = carry_ref[...].astype(jnp.int32)


def _router(x, g, wr, br, tt):
    t, d = x.shape
    small = lambda dt: jax.ShapeDtypeStruct((t, TOP_K), dt)
    return pl.pallas_call(
        _router_kernel,
        grid=(t // tt,),
        in_specs=[pl.BlockSpec((tt, d), lambda i: (i, 0)),
                  pl.BlockSpec((1, d), lambda i: (0, 0)),
                  pl.BlockSpec((d, N_EXPERTS), lambda i: (0, 0)),
                  pl.BlockSpec((1, N_EXPERTS), lambda i: (0, 0))],
        out_specs=[pl.BlockSpec((tt, d // 2), lambda i: (i, 0)),
                   pl.BlockSpec((tt, TOP_K), lambda i: (i, 0)),
                   pl.BlockSpec((tt, TOP_K), lambda i: (i, 0)),
                   pl.BlockSpec((tt, TOP_K), lambda i: (i, 0)),
                   pl.BlockSpec((1, N_EXPERTS), lambda i: (0, 0))],
        out_shape=[jax.ShapeDtypeStruct((t, d // 2), U32),
                   small(jnp.int32), small(F32), small(jnp.int32),
                   jax.ShapeDtypeStruct((1, N_EXPERTS), jnp.int32)],
        scratch_shapes=[pltpu.VMEM((1, N_EXPERTS), F32)],
        compiler_params=_params("arbitrary"),
        name="router",
    )(x, g.reshape(1, d), wr, br.reshape(1, N_EXPERTS))


def _num_passes(n_assign):
    return N_EXPERTS + n_assign // PASS_ROWS


def _routing_tables(idx, rank, counts, n_pass_max):
    t = idx.shape[0]
    padded = (counts + ROW_BLOCK - 1) // ROW_BLOCK * ROW_BLOCK
    passes_e = (padded + PASS_ROWS - 1) // PASS_ROWS
    pass_end = jnp.cumsum(passes_e)
    pass_start = pass_end - passes_e
    n_pass = pass_end[-1]
    pos = (pass_start[idx] + rank // PASS_ROWS) * PASS_ROWS + rank % PASS_ROWS
    p_ids = jnp.arange(n_pass_max, dtype=jnp.int32)
    live = p_ids < n_pass
    pe = jnp.searchsorted(pass_end, jnp.minimum(p_ids, n_pass - 1), side="right")
    pe = jnp.minimum(pe, N_EXPERTS - 1).astype(jnp.int32)
    done = (p_ids - pass_start[pe]) * PASS_ROWS
    rows = jnp.clip(padded[pe] - done, 0, PASS_ROWS)
    pass_nb = jnp.where(live, rows // ROW_BLOCK, 0).astype(jnp.int32)
    pass_valid = jnp.where(live, jnp.clip(counts[pe] - done, 0, PASS_ROWS), 0).astype(jnp.int32)
    return (pe, pass_nb, pass_valid, n_pass.reshape(1).astype(jnp.int32),
            pos.T.reshape(-1).astype(jnp.int32))


GATHER_UNROLL = SUBLANES


def _gather_rows(src_hbm, idx_ref, idx_base, n_rows, dst_ref, sem):
    def issue(g, _):
        r0 = pl.multiple_of(g * GATHER_UNROLL, GATHER_UNROLL)
        dst_tile = dst_ref.at[pl.ds(r0, GATHER_UNROLL)]
        for u in range(GATHER_UNROLL):
            pltpu.make_async_copy(src_hbm.at[pl.ds(idx_ref[idx_base + r0 + u], 1)],
                                  dst_tile.at[pl.ds(u, 1)], sem).start()
        return 0

    lax.fori_loop(0, n_rows // GATHER_UNROLL, issue, 0)


def _for_row_blocks(n_blocks, body):
    per_big = MATMUL_ROWS // ROW_BLOCK
    n_big = n_blocks // per_big

    def big(i, _):
        body(pl.multiple_of(i * MATMUL_ROWS, MATMUL_ROWS), MATMUL_ROWS)
        return 0

    lax.fori_loop(0, n_big, big, 0)
    rest = n_blocks - n_big * per_big
    start = pl.multiple_of(n_big * MATMUL_ROWS, MATMUL_ROWS)
    for m in range(1, per_big):
        @pl.when(rest == m)
        def _(m=m):
            body(start, m * ROW_BLOCK)


def _wait_row_blocks(src_hbm, dst_ref, n_blocks, rows, sem):
    def drain(i, _):
        pltpu.make_async_copy(src_hbm.at[pl.ds(0, rows)], dst_ref.at[pl.ds(0, rows)], sem).wait()
        return 0

    lax.fori_loop(0, n_blocks, drain, 0)


def _build_token_table(pos_ref, nb_ref, valid_ref, n_pass, cap, tok_ref):
    def pad_pass(q, _):
        def pad_row(r, _):
            tok_ref[q * cap + r] = 0
            return 0

        lax.fori_loop(valid_ref[q], nb_ref[q] * ROW_BLOCK, pad_row, 0)
        return 0

    lax.fori_loop(0, n_pass, pad_pass, 0)
    n_tok = pos_ref.shape[0] // TOP_K
    for k in range(TOP_K):
        def place(g, _, k=k):
            for u in range(GATHER_UNROLL):
                t = g * GATHER_UNROLL + u
                tok_ref[pos_ref[k * n_tok + t]] = t
            return 0

        lax.fori_loop(0, n_tok // GATHER_UNROLL, place, 0)


def _moe_up_kernel(pe_ref, nb_ref, valid_ref, npass_ref, pos_ref, hp_hbm, wg_ref, wu_ref, bg_ref,
                   bu_ref, o_ref, gbuf_ref, xb_ref, tok_ref, sem):
    p = pl.program_id(0)
    f = pl.program_id(1)
    cap, half = gbuf_ref.shape
    n_pass = npass_ref[0]
    live = p < n_pass
    nb = nb_ref[p]

    @pl.when(jnp.logical_and(live, f == 0))
    def _():
        @pl.when(p == 0)
        def _():
            _build_token_table(pos_ref, nb_ref, valid_ref, n_pass, cap, tok_ref)
            _gather_rows(hp_hbm, tok_ref, 0, nb * ROW_BLOCK, gbuf_ref, sem)

        _wait_row_blocks(hp_hbm, gbuf_ref, nb, ROW_BLOCK, sem)

        def unpack(rb, _):
            rows = pl.ds(pl.multiple_of(rb * ROW_BLOCK, ROW_BLOCK), ROW_BLOCK)
            hi, lo = _unpack_bf16_pair(gbuf_ref[rows, :])
            xb_ref[rows, :half] = hi.astype(BF16)
            xb_ref[rows, half:] = lo.astype(BF16)
            return 0

        lax.fori_loop(0, nb, unpack, 0)

        @pl.when(p + 1 < n_pass)
        def _():
            _gather_rows(hp_hbm, tok_ref, (p + 1) * cap, nb_ref[p + 1] * ROW_BLOCK, gbuf_ref, sem)

    @pl.when(live)
    def _():
        def block(row0, m):
            rows = pl.ds(row0, m)
            x = xb_ref[rows, :]
            hg = jnp.dot(x, wg_ref[...].astype(BF16), preferred_element_type=F32) + bg_ref[...]
            hu = jnp.dot(x, wu_ref[...].astype(BF16), preferred_element_type=F32) + bu_ref[...]
            hg = jnp.minimum(hg, SWIGLU_LIMIT)
            hu = jnp.clip(hu, -SWIGLU_LIMIT, SWIGLU_LIMIT)
            act = hg * _sigmoid(SWIGLU_ALPHA * hg) * (hu + 1.0)
            o_ref[rows, :] = act.astype(o_ref.dtype)

        _for_row_blocks(nb, block)


def _moe_up(pe, pass_nb, pass_valid, n_pass, pos_flat, hp, wg, wu, bg, bu, n_pass_max, tf):
    e, d, f = wg.shape
    nf = f // tf
    cap = PASS_ROWS

    def w_map(p, j, pe_, nb_, valid_, np_, pos_):
        return (pe_[p], 0, jnp.where(p < np_[0], j, nf - 1))

    def o_map(p, j, pe_, nb_, valid_, np_, pos_):
        ok = p < np_[0]
        return (jnp.where(ok, p, np_[0] - 1), jnp.where(ok, j, nf - 1))

    grid_spec = pltpu.PrefetchScalarGridSpec(
        num_scalar_prefetch=5,
        grid=(n_pass_max, nf),
        in_specs=[pl.BlockSpec(memory_space=pl.ANY),
                  pl.BlockSpec((None, d, tf), w_map),
                  pl.BlockSpec((None, d, tf), w_map),
                  pl.BlockSpec((None, 1, tf), w_map),
                  pl.BlockSpec((None, 1, tf), w_map)],
        out_specs=pl.BlockSpec((cap, tf), o_map),
        scratch_shapes=[pltpu.VMEM((cap, d // 2), U32),
                        pltpu.VMEM((cap, d), BF16),
                        pltpu.SMEM((n_pass_max * cap,), jnp.int32),
                        pltpu.SemaphoreType.DMA(())],
    )
    return pl.pallas_call(
        _moe_up_kernel,
        grid_spec=grid_spec,
        out_shape=jax.ShapeDtypeStruct((n_pass_max * cap, f), BF16),
        compiler_params=_params("arbitrary", "arbitrary"),
        name="moe_up",
    )(pe, pass_nb, pass_valid, n_pass, pos_flat, hp, wg, wu, bg.reshape(e, 1, f),
      bu.reshape(e, 1, f))


def _moe_down_kernel(pe_ref, nb_ref, npass_ref, a_ref, wh_ref, wl_ref, bh_ref, bl_ref, o_ref):
    p = pl.program_id(0)

    @pl.when(p < npass_ref[0])
    def _():
        def block(row0, m):
            rows = pl.ds(row0, m)
            a = a_ref[rows, :]
            hi = jnp.dot(a, wh_ref[...].astype(BF16), preferred_element_type=F32) + bh_ref[...]
            lo = jnp.dot(a, wl_ref[...].astype(BF16), preferred_element_type=F32) + bl_ref[...]
            o_ref[rows, :] = _pack_bf16_pair(hi, lo)

        _for_row_blocks(nb_ref[p], block)


def _moe_down(pe, pass_nb, n_pass, act, wd, bd, n_pass_max, td):
    e, f, d = wd.shape
    nd = d // 2 // td
    cap = PASS_ROWS

    def col(p, j, np_):
        return jnp.where(p < np_[0], j, nd - 1)

    def row(p, np_):
        return jnp.where(p < np_[0], p, np_[0] - 1)

    grid_spec = pltpu.PrefetchScalarGridSpec(
        num_scalar_prefetch=3,
        grid=(n_pass_max, nd),
        in_specs=[pl.BlockSpec((cap, f), lambda p, j, pe_, nb_, np_: (row(p, np_), 0)),
                  pl.BlockSpec((None, f, td), lambda p, j, pe_, nb_, np_: (pe_[p], 0, col(p, j, np_))),
                  pl.BlockSpec((None, f, td), lambda p, j, pe_, nb_, np_: (pe_[p], 0, nd + col(p, j, np_))),
                  pl.BlockSpec((None, 1, td), lambda p, j, pe_, nb_, np_: (pe_[p], 0, col(p, j, np_))),
                  pl.BlockSpec((None, 1, td), lambda p, j, pe_, nb_, np_: (pe_[p], 0, nd + col(p, j, np_)))],
        out_specs=pl.BlockSpec((cap, td), lambda p, j, pe_, nb_, np_: (row(p, np_), col(p, j, np_))),
    )
    bd3 = bd.reshape(e, 1, d)
    return pl.pallas_call(
        _moe_down_kernel,
        grid_spec=grid_spec,
        out_shape=jax.ShapeDtypeStruct((n_pass_max * cap, d // 2), U32),
        compiler_params=_params("arbitrary", "arbitrary"),
        name="moe_down",
    )(pe, pass_nb, n_pass, act, wd, wd, bd3, bd3)


def _combine_kernel(pos_ref, y_hbm, x_ref, gate_ref, g_ref, o_ref, ybuf_ref, sems):
    i = pl.program_id(0)
    n = pl.num_programs(0)
    tt, d = x_ref.shape
    half = d // 2
    slot = i % 2

    n_tok = n * tt

    def start_tile(step, slot_):
        for k in range(TOP_K):
            _gather_rows(y_hbm, pos_ref, k * n_tok + step * tt, tt, ybuf_ref.at[slot_, k],
                         sems.at[slot_])

    @pl.when(i == 0)
    def _():
        start_tile(0, 0)

    @pl.when(i + 1 < n)
    def _():
        start_tile(i + 1, 1 - slot)

    for k in range(TOP_K):
        pltpu.make_async_copy(y_hbm.at[pl.ds(0, tt)], ybuf_ref.at[slot, k], sems.at[slot]).wait()

    gates = gate_ref[...]
    x = x_ref[...]
    acc_hi = x[:, :half]
    acc_lo = x[:, half:]
    for k in range(TOP_K):
        hi, lo = _unpack_bf16_pair(ybuf_ref[slot, k])
        acc_hi = acc_hi + gates[:, k:k + 1] * hi
        acc_lo = acc_lo + gates[:, k:k + 1] * lo
    ms = (jnp.sum(acc_hi * acc_hi, axis=-1, keepdims=True)
          + jnp.sum(acc_lo * acc_lo, axis=-1, keepdims=True)) * (1.0 / d)
    scale = lax.rsqrt(ms + RMS_EPS)
    o_ref[:, :half] = acc_hi * scale * g_ref[:, :half]
    o_ref[:, half:] = acc_lo * scale * g_ref[:, half:]


def _combine(pos_flat, y, x, gates, g, tt):
    t, d = x.shape
    grid_spec = pltpu.PrefetchScalarGridSpec(
        num_scalar_prefetch=1,
        grid=(t // tt,),
        in_specs=[pl.BlockSpec(memory_space=pl.ANY),
                  pl.BlockSpec((tt, d), lambda i, p: (i, 0)),
                  pl.BlockSpec((tt, TOP_K), lambda i, p: (i, 0)),
                  pl.BlockSpec((1, d), lambda i, p: (0, 0))],
        out_specs=pl.BlockSpec((tt, d), lambda i, p: (i, 0)),
        scratch_shapes=[pltpu.VMEM((2, TOP_K, tt, d // 2), U32),
                        pltpu.SemaphoreType.DMA((2,))],
    )
    return pl.pallas_call(
        _combine_kernel,
        grid_spec=grid_spec,
        out_shape=jax.ShapeDtypeStruct((t, d), F32),
        compiler_params=_params("arbitrary"),
        name="combine",
    )(pos_flat, y, x, gates, g.reshape(1, d))


class _Tiles(NamedTuple):
    norm_rows: int = 512
    inproj: tuple = (512, 1024)
    outproj: tuple = (1024, 512)
    router_rows: int = 512
    up_cols: int = 256
    down_cols: int = 512
    combine_rows: int = 256


TILES = _Tiles()


def _moe_and_final_norm(x_mid, norm2_g, w_router, b_router, w_gate, b_gate, w_up, b_up,
                        w_down, b_down, final_norm_g):
    t = x_mid.shape[0]
    n_pass_max = _num_passes(t * TOP_K)
    hp, idx, gates, rank, counts = _router(x_mid, norm2_g, w_router, b_router,
                                           tt=TILES.router_rows)
    pe, pass_nb, pass_valid, n_pass, pos_flat = _routing_tables(idx, rank, counts[0], n_pass_max)
    act = _moe_up(pe, pass_nb, pass_valid, n_pass, pos_flat, hp, w_gate, w_up, b_gate, b_up,
                  n_pass_max, tf=TILES.up_cols)
    y_rows = _moe_down(pe, pass_nb, n_pass, act, w_down, b_down, n_pass_max, td=TILES.down_cols)
    return _combine(pos_flat, y_rows, x_mid, gates, final_norm_g, tt=TILES.combine_rows)


def kernel(x, meta_tokens, norm1_g, w_in, lru_conv_w, lru_conv_b, lru_w_a, lru_b_a, lru_w_i, lru_b_i, lru_lambda, conf_conv_w, conf_conv_b, conf_norm_g, conf_norm_b, w_out, norm2_g, w_router, b_router, w_gate, b_gate, w_up, b_up, w_down, b_down, final_norm_g):
    b, s, d = x.shape
    t = b * s
    assert norm1_g.shape[0] == 1, "one layer"
    assert d == LRU_WIDTH + CONF_WIDTH == LRU_HEADS * HEAD_DIM + CONF_GROUPS * HEAD_DIM
    assert s % (LRU_CHUNKS * SUBLANES) == 0 and s % CONF_CHUNK == 0
    assert all(t % rows == 0 for rows in (TILES.norm_rows, TILES.inproj[0], TILES.outproj[0],
                                          TILES.router_rows, TILES.combine_rows))
    x2 = x.reshape(t, d)

    h = _rmsnorm(x2, norm1_g[0], tm=TILES.norm_rows)
    h_meta = _rmsnorm(meta_tokens.astype(x.dtype), norm1_g[0], tm=N_META)
    z, z_meta = _inproj(h, h_meta, w_in[0], *TILES.inproj)
    z3 = z.reshape(b, s, -1)

    y_lru = _lru_mixer(z3, z_meta, lru_conv_w[0], lru_conv_b[0], lru_w_a[0], lru_b_a[0],
                       lru_w_i[0], lru_b_i[0], lru_lambda[0])
    y_conf = _conf_mixer(z3, z_meta, conf_conv_w[0], conf_conv_b[0], conf_norm_g[0],
                         conf_norm_b[0])
    x_mid = _outproj(y_lru.reshape(t, -1), y_conf.reshape(t, -1), w_out[0], x2, *TILES.outproj)

    out = _moe_and_final_norm(x_mid, norm2_g[0], w_router[0], b_router[0], w_gate[0], b_gate[0],
                              w_up[0], b_up[0], w_down[0], b_down[0], final_norm_g)
    return out.reshape(b, s, d)
```

```python
from typing import NamedTuple

import jax
import jax.numpy as jnp
from jax import lax
from jax.experimental import pallas as pl
from jax.experimental.pallas import tpu as pltpu

N_META = 16
LRU_WIDTH = 2048
LRU_HEADS = 16
HEAD_DIM = 128
LRU_CONV_WIDTH = 4
LRU_C = 8.0
CONF_WIDTH = 2048
CONF_GROUPS = 16
CONF_KERNEL = 31
N_EXPERTS = 32
TOP_K = 4
SWIGLU_ALPHA = 1.702
SWIGLU_LIMIT = 7.0
RMS_EPS = 1e-5
LN_EPS = 1e-5
SQRT_FLOOR = 1e-30

SUBLANES = 8
V7X_VMEM_BYTES = 64 * 1024 * 1024
VMEM_LIMIT = V7X_VMEM_BYTES - 6 * 1024 * 1024

ROW_BLOCK = 128
MATMUL_ROWS = 1536
PASS_ROWS = 1536
BF16 = jnp.bfloat16
F32 = jnp.float32
U32 = jnp.uint32


def _params(*sem):
    return pltpu.CompilerParams(dimension_semantics=sem, vmem_limit_bytes=VMEM_LIMIT)


def _pack_bf16_pair(hi, lo):
    hi_bits = lax.bitcast_convert_type(hi.astype(BF16).astype(F32), U32)
    lo_bits = lax.bitcast_convert_type(lo.astype(BF16).astype(F32), U32)
    return hi_bits | lax.shift_right_logical(lo_bits, jnp.uint32(16))


def _sigmoid(x):
    return 0.5 * jnp.tanh(0.5 * x) + 0.5


def _unpack_bf16_pair(u):
    hi = lax.bitcast_convert_type(u & jnp.uint32(0xFFFF0000), F32)
    lo = lax.bitcast_convert_type(lax.shift_left(u, jnp.uint32(16)), F32)
    return hi, lo


def _rmsnorm_kernel(x_ref, g_ref, o_ref):
    x = x_ref[...]
    ms = jnp.mean(x * x, axis=-1, keepdims=True)
    o_ref[...] = (x * lax.rsqrt(ms + RMS_EPS) * g_ref[...]).astype(o_ref.dtype)


def _rmsnorm(x, g, tm):
    t, d = x.shape
    return pl.pallas_call(
        _rmsnorm_kernel,
        grid=(t // tm,),
        in_specs=[pl.BlockSpec((tm, d), lambda i: (i, 0)),
                  pl.BlockSpec((1, d), lambda i: (0, 0))],
        out_specs=pl.BlockSpec((tm, d), lambda i: (i, 0)),
        out_shape=jax.ShapeDtypeStruct((t, d), BF16),
        compiler_params=_params("parallel"),
        name="rmsnorm1",
    )(x, g.reshape(1, d))


def _inproj_kernel(a_ref, am_ref, w_ref, o_ref, om_ref, wb_ref):
    @pl.when(pl.program_id(1) == 0)
    def _():
        wb_ref[...] = w_ref[...].astype(BF16)
        om_ref[...] = jnp.dot(am_ref[...], wb_ref[...], preferred_element_type=F32)

    o_ref[...] = jnp.dot(a_ref[...], wb_ref[...], preferred_element_type=F32)


def _inproj(a, a_meta, w, tm, tn):
    t, k = a.shape
    n = w.shape[1]
    return pl.pallas_call(
        _inproj_kernel,
        grid=(n // tn, t // tm),
        in_specs=[pl.BlockSpec((tm, k), lambda j, i: (i, 0)),
                  pl.BlockSpec((N_META, k), lambda j, i: (0, 0)),
                  pl.BlockSpec((k, tn), lambda j, i: (0, j))],
        out_specs=[pl.BlockSpec((tm, tn), lambda j, i: (i, j)),
                   pl.BlockSpec((N_META, tn), lambda j, i: (0, j))],
        out_shape=[jax.ShapeDtypeStruct((t, n), F32),
                   jax.ShapeDtypeStruct((N_META, n), F32)],
        scratch_shapes=[pltpu.VMEM((k, tn), BF16)],
        compiler_params=_params("parallel", "arbitrary"),
        name="inproj",
    )(a, a_meta, w)


LRU_CHUNKS = SUBLANES
LRU_CHUNK_PAD = SUBLANES
LRU_SCAN_UNROLL = 8


def _gelu_tanh(x):
    return 0.5 * x * (1.0 + jnp.tanh(0.7978845608028654 * (x + 0.044715 * x * x * x)))


def _lru_kernel(xr_ref, gate_ref, xm_ref, cw_ref, cb_ref, wa_ref, ba_ref, wi_ref,
                bi_ref, lam_ref, o_ref, sf_ref, af_ref, bf_ref, ab_ref, bb_ref, hf_ref, hb_ref):
    s = xr_ref.shape[0]
    st = s + N_META
    pad = SUBLANES
    clen = s // LRU_CHUNKS
    pitch = clen + LRU_CHUNK_PAD
    zeros8 = jnp.zeros((pad, HEAD_DIM), F32)
    sf_ref[0:pad, :] = zeros8
    sf_ref[pad:pad + N_META, :] = xm_ref[...]
    sf_ref[pad + N_META:pad + st, :] = xr_ref[...]
    sf_ref[pad + st:pad + st + pad, :] = zeros8

    def gates(u, d):
        ub = u.astype(BF16)
        t_r = jnp.tanh(jnp.dot(ub, (0.5 * wa_ref[d]).astype(BF16), preferred_element_type=F32)
                       + 0.5 * ba_ref[d:d + 1, :])
        t_i = jnp.tanh(jnp.dot(ub, (0.5 * wi_ref[d]).astype(BF16), preferred_element_type=F32)
                       + 0.5 * bi_ref[d:d + 1, :])
        lam = lam_ref[d:d + 1, :]
        softplus_neg = jnp.maximum(-lam, 0.0) + jnp.log1p(jnp.exp(-jnp.abs(lam)))
        half_c = (-0.5 * LRU_C) * softplus_neg
        log_a = half_c * t_r + half_c
        a = jnp.exp(log_a)
        one_minus_a2 = -jnp.tanh(log_a) * (a * a + 1.0)
        root = one_minus_a2 * lax.rsqrt(jnp.maximum(one_minus_a2, SQRT_FLOOR))
        half_u = 0.5 * u
        b = root * (half_u * t_i + half_u)
        return a, b

    uf = cb_ref[0:1, :] + cw_ref[0, 3:4, :] * sf_ref[pl.ds(pad, st), :]
    for j in range(1, LRU_CONV_WIDTH):
        uf = uf + cw_ref[0, 3 - j:4 - j, :] * sf_ref[pl.ds(pad - j, st), :]
    a, b = gates(uf, 0)
    h_meta = b[0:1, :]
    for r in range(1, N_META):
        h_meta = a[r:r + 1, :] * h_meta + b[r:r + 1, :]
    for c in range(LRU_CHUNKS):
        af_ref[c * pitch:c * pitch + clen, :] = a[N_META + c * clen:N_META + (c + 1) * clen, :]
        bf_ref[c * pitch:c * pitch + clen, :] = b[N_META + c * clen:N_META + (c + 1) * clen, :]

    base = pad + N_META
    ub_ = cb_ref[1:2, :] + cw_ref[1, 3:4, :] * sf_ref[pl.ds(base, s), :]
    for j in range(1, LRU_CONV_WIDTH):
        ub_ = ub_ + cw_ref[1, 3 - j:4 - j, :] * sf_ref[pl.ds(base + j, s), :]
    a, b = gates(ub_, 1)
    for c in range(LRU_CHUNKS):
        ab_ref[c * pitch:c * pitch + clen, :] = a[c * clen:(c + 1) * clen, :]
        bb_ref[c * pitch:c * pitch + clen, :] = b[c * clen:(c + 1) * clen, :]

    def step_rows(t):
        return (pl.ds(t, LRU_CHUNKS, stride=pitch),
                pl.ds(clen - 1 - t, LRU_CHUNKS, stride=pitch))

    def local_scan(t, carry):
        hf, pf, hb, pb = carry
        rf, rb = step_rows(t)
        a_f = af_ref[rf, :]
        a_b = ab_ref[rb, :]
        return (a_f * hf + bf_ref[rf, :], a_f * pf, a_b * hb + bb_ref[rb, :], a_b * pb)

    zero = sf_ref[0:SUBLANES, :]
    one = zero + 1.0
    hf, pf, hb, pb = lax.fori_loop(0, clen, local_scan, (zero, one, zero, one),
                                   unroll=LRU_SCAN_UNROLL)

    row = lax.broadcasted_iota(jnp.int32, (LRU_CHUNKS, HEAD_DIM), 0)
    c = h_meta
    start_f = jnp.where(row == 0, c, zero)
    for k in range(1, LRU_CHUNKS):
        c = hf[k - 1:k, :] + pf[k - 1:k, :] * c
        start_f = jnp.where(row == k, c, start_f)
    c = zero[0:1, :]
    start_b = zero
    for k in range(LRU_CHUNKS - 2, -1, -1):
        c = hb[k + 1:k + 2, :] + pb[k + 1:k + 2, :] * c
        start_b = jnp.where(row == k, c, start_b)

    def final_scan(t, carry):
        hf, hb = carry
        rf, rb = step_rows(t)
        hf = af_ref[rf, :] * hf + bf_ref[rf, :]
        hb = ab_ref[rb, :] * hb + bb_ref[rb, :]
        hf_ref[rf, :] = hf
        hb_ref[rb, :] = hb
        return hf, hb

    lax.fori_loop(0, clen, final_scan, (start_f, start_b), unroll=LRU_SCAN_UNROLL)
    for c in range(LRU_CHUNKS):
        hsum = hf_ref[c * pitch:c * pitch + clen, :] + hb_ref[c * pitch:c * pitch + clen, :]
        y = hsum * _gelu_tanh(gate_ref[c * clen:(c + 1) * clen, :])
        o_ref[c * clen:(c + 1) * clen, :] = y.astype(o_ref.dtype)


def _lru_mixer(z3, z_meta, cw, cb, wa, ba, wi, bi, lam):
    b, s, _ = z3.shape
    st = s + N_META
    hd = HEAD_DIM
    col = lambda off: (lambda bi_, h: (bi_, 0, off + h))
    return pl.pallas_call(
        _lru_kernel,
        grid=(b, LRU_HEADS),
        in_specs=[
            pl.BlockSpec((None, s, hd), col(0)),
            pl.BlockSpec((None, s, hd), col(LRU_HEADS)),
            pl.BlockSpec((N_META, hd), lambda bi_, h: (0, h)),
            pl.BlockSpec((2, LRU_CONV_WIDTH, hd), lambda bi_, h: (0, 0, h)),
            pl.BlockSpec((2, hd), lambda bi_, h: (0, h)),
            pl.BlockSpec((2, None, hd, hd), lambda bi_, h: (0, h, 0, 0)),
            pl.BlockSpec((2, hd), lambda bi_, h: (0, h)),
            pl.BlockSpec((2, None, hd, hd), lambda bi_, h: (0, h, 0, 0)),
            pl.BlockSpec((2, hd), lambda bi_, h: (0, h)),
            pl.BlockSpec((2, hd), lambda bi_, h: (0, h)),
        ],
        out_specs=pl.BlockSpec((None, s, hd), lambda bi_, h: (bi_, 0, h)),
        out_shape=jax.ShapeDtypeStruct((b, s, LRU_WIDTH), BF16),
        scratch_shapes=[
            pltpu.VMEM((st + 2 * SUBLANES, hd), F32),
        ] + [pltpu.VMEM((LRU_CHUNKS * (s // LRU_CHUNKS + LRU_CHUNK_PAD), hd), F32)] * 6,
        compiler_params=_params("parallel", "parallel"),
        name="lru_mixer",
    )(z3, z3, z_meta, cw, cb, wa, ba.reshape(2, LRU_WIDTH), wi, bi.reshape(2, LRU_WIDTH), lam)


CONF_CHUNK = 64
CONF_PARTIAL_SUMS = 2


def _conf_kernel(a_ref, b_ref, am_ref, bm_ref, cw_ref, cb_ref, g_ref, be_ref, o_ref,
                 cs_ref, sh_ref):
    s = a_ref.shape[0]
    st = s + N_META
    n_sh = sh_ref.shape[1]
    cs_ref[0:N_META, :] = am_ref[...] * _sigmoid(bm_ref[...])
    cs_ref[N_META:st, :] = a_ref[...] * _sigmoid(b_ref[...])
    cs_ref[st:st + N_META, :] = jnp.zeros((N_META, HEAD_DIM), F32)
    for r in range(1, SUBLANES):
        sh_ref[r - 1] = cs_ref[pl.ds(r, n_sh), :]

    n_chunks = s // CONF_CHUNK

    def conv_chunk(row0):
        parts = [None] * CONF_PARTIAL_SUMS
        for k in range(CONF_KERNEL):
            off = N_META - CONF_KERNEL // 2 + k
            r, q = off % SUBLANES, off // SUBLANES
            rows = pl.ds(row0 + q * SUBLANES, CONF_CHUNK)
            term = cw_ref[k:k + 1, :] * (cs_ref[rows, :] if r == 0 else sh_ref[r - 1, rows, :])
            p = k % CONF_PARTIAL_SUMS
            parts[p] = term if parts[p] is None else parts[p] + term
        return sum(parts[1:], parts[0]) + cb_ref[...]

    def step(j, carry):
        acc_prev, xc_prev = carry
        mean = jnp.mean(acc_prev, axis=-1, keepdims=True)
        var = jnp.mean(xc_prev * xc_prev, axis=-1, keepdims=True)
        acc = conv_chunk(pl.multiple_of(jnp.minimum(j, n_chunks - 1) * CONF_CHUNK, CONF_CHUNK))
        y = xc_prev * lax.rsqrt(var + LN_EPS) * g_ref[...] + be_ref[...]
        out_row = pl.multiple_of(jnp.maximum(j - 2, 0) * CONF_CHUNK, CONF_CHUNK)
        half_y = 0.5 * y
        o_ref[pl.ds(out_row, CONF_CHUNK), :] = (half_y * jnp.tanh(half_y) + half_y).astype(o_ref.dtype)
        return acc, acc_prev - mean

    warmup = cs_ref[pl.ds(0, CONF_CHUNK), :]
    lax.fori_loop(0, n_chunks + 2, step, (warmup, warmup))


def _conf_mixer(z3, z_meta, cw, cb, g, be):
    b, s, _ = z3.shape
    st = s + N_META
    hd = HEAD_DIM
    a_off = 2 * LRU_HEADS
    b_off = 2 * LRU_HEADS + CONF_GROUPS
    n_sh = st + N_META - SUBLANES
    vec = pl.BlockSpec((1, hd), lambda bi_, h: (0, h))
    return pl.pallas_call(
        _conf_kernel,
        grid=(b, CONF_GROUPS),
        in_specs=[
            pl.BlockSpec((None, s, hd), lambda bi_, h: (bi_, 0, a_off + h)),
            pl.BlockSpec((None, s, hd), lambda bi_, h: (bi_, 0, b_off + h)),
            pl.BlockSpec((N_META, hd), lambda bi_, h: (0, a_off + h)),
            pl.BlockSpec((N_META, hd), lambda bi_, h: (0, b_off + h)),
            pl.BlockSpec((CONF_KERNEL, hd), lambda bi_, h: (0, h)),
            vec, vec, vec,
        ],
        out_specs=pl.BlockSpec((None, s, hd), lambda bi_, h: (bi_, 0, h)),
        out_shape=jax.ShapeDtypeStruct((b, s, CONF_WIDTH), BF16),
        scratch_shapes=[
            pltpu.VMEM((st + N_META, hd), F32),
            pltpu.VMEM((SUBLANES - 1, n_sh, hd), F32),
        ],
        compiler_params=_params("parallel", "parallel"),
        name="conf_mixer",
    )(z3, z3, z_meta, z_meta, cw, cb.reshape(1, -1), g.reshape(1, -1), be.reshape(1, -1))


def _outproj_kernel(ya_ref, yb_ref, wa_ref, wb_ref, x_ref, o_ref):
    acc = jnp.dot(ya_ref[...], wa_ref[...].astype(BF16), preferred_element_type=F32)
    acc = acc + jnp.dot(yb_ref[...], wb_ref[...].astype(BF16), preferred_element_type=F32)
    o_ref[...] = x_ref[...] + acc


def _outproj(ya, yb, w, x, tm, tn):
    t, k = ya.shape
    n = w.shape[1]
    return pl.pallas_call(
        _outproj_kernel,
        grid=(t // tm, n // tn),
        in_specs=[pl.BlockSpec((tm, k), lambda i, j: (i, 0)),
                  pl.BlockSpec((tm, k), lambda i, j: (i, 0)),
                  pl.BlockSpec((k, tn), lambda i, j: (0, j)),
                  pl.BlockSpec((k, tn), lambda i, j: (1, j)),
                  pl.BlockSpec((tm, tn), lambda i, j: (i, j))],
        out_specs=pl.BlockSpec((tm, tn), lambda i, j: (i, j)),
        out_shape=jax.ShapeDtypeStruct((t, n), F32),
        compiler_params=_params("parallel", "parallel"),
        name="outproj",
    )(ya, yb, w, w, x)


def _router_kernel(x_ref, g_ref, wr_ref, br_ref, hp_ref, idx_ref, gate_ref, rank_ref,
                   cnt_ref, carry_ref):
    tt, d = x_ref.shape
    half = d // 2

    @pl.when(pl.program_id(0) == 0)
    def _():
        carry_ref[...] = jnp.zeros_like(carry_ref)

    x = x_ref[...]
    ms = jnp.mean(x * x, axis=-1, keepdims=True)
    h = x * lax.rsqrt(ms + RMS_EPS) * g_ref[...]
    hp_ref[...] = _pack_bf16_pair(h[:, :half], h[:, half:])

    h_hi = h.astype(BF16)
    h_lo = (h - h_hi.astype(F32)).astype(BF16)
    w = wr_ref[...]
    w_hi = w.astype(BF16)
    w_lo = (w - w_hi.astype(F32)).astype(BF16)
    hi_terms = jnp.dot(h_hi, jnp.concatenate([w_hi, w_lo], axis=1), preferred_element_type=F32)
    logits = (hi_terms[:, :N_EXPERTS] + hi_terms[:, N_EXPERTS:]
              + jnp.dot(h_lo, w_hi, preferred_element_type=F32)) + br_ref[...]

    lane = lax.broadcasted_iota(jnp.int32, (tt, N_EXPERTS), 1)
    lane_k = lax.broadcasted_iota(jnp.int32, (tt, TOP_K), 1)
    work = logits
    vals, sels = [], []
    idx_out = jnp.zeros((tt, TOP_K), jnp.int32)
    for k in range(TOP_K):
        m = jnp.max(work, axis=1, keepdims=True)
        am = jnp.min(jnp.where(work == m, lane, N_EXPERTS), axis=1, keepdims=True)
        sel = lane == am
        vals.append(m)
        sels.append(sel)
        idx_out = jnp.where(lane_k == k, am, idx_out)
        work = jnp.where(sel, -jnp.inf, work)
    idx_ref[...] = idx_out

    exps = [jnp.exp(v - vals[0]) for v in vals]
    denom = exps[0] + exps[1] + exps[2] + exps[3]
    gate_out = jnp.zeros((tt, TOP_K), F32)
    for k in range(TOP_K):
        gate_out = jnp.where(lane_k == k, exps[k] / denom, gate_out)
    gate_ref[...] = gate_out

    onehot = jnp.zeros((tt, N_EXPERTS), F32)
    for sel in sels:
        onehot = onehot + sel.astype(F32)
    r_i = lax.broadcasted_iota(jnp.int32, (tt, tt), 0)
    c_i = lax.broadcasted_iota(jnp.int32, (tt, tt), 1)
    tri = (c_i < r_i).astype(BF16)
    before = jnp.dot(tri, onehot.astype(BF16), preferred_element_type=F32) + carry_ref[...]
    rank_out = jnp.zeros((tt, TOP_K), jnp.int32)
    for k, sel in enumerate(sels):
        rk = jnp.sum(jnp.where(sel, before, 0.0), axis=1, keepdims=True).astype(jnp.int32)
        rank_out = jnp.where(lane_k == k, rk, rank_out)
    rank_ref[...] = rank_out
    carry_ref[...] = carry_ref[...] + jnp.sum(onehot, axis=0, keepdims=True)
    cnt_ref[...] = carry_ref[...].astype(jnp.int32)


def _router(x, g, wr, br, tt):
    t, d = x.shape
    small = lambda dt: jax.ShapeDtypeStruct((t, TOP_K), dt)
    return pl.pallas_call(
        _router_kernel,
        grid=(t // tt,),
        in_specs=[pl.BlockSpec((tt, d), lambda i: (i, 0)),
                  pl.BlockSpec((1, d), lambda i: (0, 0)),
                  pl.BlockSpec((d, N_EXPERTS), lambda i: (0, 0)),
                  pl.BlockSpec((1, N_EXPERTS), lambda i: (0, 0))],
        out_specs=[pl.BlockSpec((tt, d // 2), lambda i: (i, 0)),
                   pl.BlockSpec((tt, TOP_K), lambda i: (i, 0)),
                   pl.BlockSpec((tt, TOP_K), lambda i: (i, 0)),
                   pl.BlockSpec((tt, TOP_K), lambda i: (i, 0)),
                   pl.BlockSpec((1, N_EXPERTS), lambda i: (0, 0))],
        out_shape=[jax.ShapeDtypeStruct((t, d // 2), U32),
                   small(jnp.int32), small(F32), small(jnp.int32),
                   jax.ShapeDtypeStruct((1, N_EXPERTS), jnp.int32)],
        scratch_shapes=[pltpu.VMEM((1, N_EXPERTS), F32)],
        compiler_params=_params("arbitrary"),
        name="router",
    )(x, g.reshape(1, d), wr, br.reshape(1, N_EXPERTS))


def _num_passes(n_assign):
    return N_EXPERTS + n_assign // PASS_ROWS


def _per_expert(table, experts):
    onehot = experts[..., None] == jnp.arange(N_EXPERTS, dtype=experts.dtype)
    return jnp.sum(jnp.where(onehot, table, 0), axis=-1)


def _routing_tables(idx, rank, counts, n_pass_max):
    padded = (counts + ROW_BLOCK - 1) // ROW_BLOCK * ROW_BLOCK
    passes_e = (padded + PASS_ROWS - 1) // PASS_ROWS
    pass_end = jnp.cumsum(passes_e)
    pass_start = pass_end - passes_e
    n_pass = pass_end[-1]
    pos = (_per_expert(pass_start, idx) + rank // PASS_ROWS) * PASS_ROWS + rank % PASS_ROWS
    p_ids = jnp.arange(n_pass_max, dtype=jnp.int32)
    live = p_ids < n_pass
    pe = jnp.sum(jnp.minimum(p_ids, n_pass - 1)[:, None] >= pass_end[None, :], axis=1)
    pe = jnp.minimum(pe, N_EXPERTS - 1).astype(jnp.int32)
    done = (p_ids - _per_expert(pass_start, pe)) * PASS_ROWS
    rows = jnp.clip(_per_expert(padded, pe) - done, 0, PASS_ROWS)
    pass_nb = jnp.where(live, rows // ROW_BLOCK, 0).astype(jnp.int32)
    pass_valid = jnp.where(live, jnp.clip(_per_expert(counts, pe) - done, 0, PASS_ROWS),
                           0).astype(jnp.int32)
    return (pe, pass_nb, pass_valid, n_pass.reshape(1).astype(jnp.int32),
            pos.T.reshape(-1).astype(jnp.int32))


GATHER_UNROLL = SUBLANES


def _gather_rows(src_hbm, idx_ref, idx_base, n_rows, dst_ref, sem):
    def issue(g, _):
        r0 = pl.multiple_of(g * GATHER_UNROLL, GATHER_UNROLL)
        dst_tile = dst_ref.at[pl.ds(r0, GATHER_UNROLL)]
        for u in range(GATHER_UNROLL):
            pltpu.make_async_copy(src_hbm.at[pl.ds(idx_ref[idx_base + r0 + u], 1)],
                                  dst_tile.at[pl.ds(u, 1)], sem).start()
        return 0

    lax.fori_loop(0, n_rows // GATHER_UNROLL, issue, 0)


def _for_row_blocks(n_blocks, body):
    per_big = MATMUL_ROWS // ROW_BLOCK
    n_big = n_blocks // per_big

    def big(i, _):
        body(pl.multiple_of(i * MATMUL_ROWS, MATMUL_ROWS), MATMUL_ROWS)
        return 0

    lax.fori_loop(0, n_big, big, 0)
    rest = n_blocks - n_big * per_big
    start = pl.multiple_of(n_big * MATMUL_ROWS, MATMUL_ROWS)
    for m in range(1, per_big):
        @pl.when(rest == m)
        def _(m=m):
            body(start, m * ROW_BLOCK)


def _wait_row_blocks(src_hbm, dst_ref, n_blocks, rows, sem):
    def drain(i, _):
        pltpu.make_async_copy(src_hbm.at[pl.ds(0, rows)], dst_ref.at[pl.ds(0, rows)], sem).wait()
        return 0

    lax.fori_loop(0, n_blocks, drain, 0)


def _build_token_table(pos_ref, nb_ref, valid_ref, n_pass, cap, tok_ref):
    def pad_pass(q, _):
        def pad_row(r, _):
            tok_ref[q * cap + r] = 0
            return 0

        lax.fori_loop(valid_ref[q], nb_ref[q] * ROW_BLOCK, pad_row, 0)
        return 0

    lax.fori_loop(0, n_pass, pad_pass, 0)
    n_tok = pos_ref.shape[0] // TOP_K
    for k in range(TOP_K):
        def place(g, _, k=k):
            for u in range(GATHER_UNROLL):
                t = g * GATHER_UNROLL + u
                tok_ref[pos_ref[k * n_tok + t]] = t
            return 0

        lax.fori_loop(0, n_tok // GATHER_UNROLL, place, 0)


def _moe_up_kernel(pe_ref, nb_ref, valid_ref, npass_ref, pos_ref, hp_hbm, wg_ref, wu_ref, bg_ref,
                   bu_ref, o_ref, gbuf_ref, xb_ref, tok_ref, sem):
    p = pl.program_id(0)
    f = pl.program_id(1)
    cap, half = gbuf_ref.shape
    n_pass = npass_ref[0]
    live = p < n_pass
    nb = nb_ref[p]

    @pl.when(jnp.logical_and(live, f == 0))
    def _():
        @pl.when(p == 0)
        def _():
            _build_token_table(pos_ref, nb_ref, valid_ref, n_pass, cap, tok_ref)
            _gather_rows(hp_hbm, tok_ref, 0, nb * ROW_BLOCK, gbuf_ref, sem)

        _wait_row_blocks(hp_hbm, gbuf_ref, nb, ROW_BLOCK, sem)

        def unpack(rb, _):
            rows = pl.ds(pl.multiple_of(rb * ROW_BLOCK, ROW_BLOCK), ROW_BLOCK)
            hi, lo = _unpack_bf16_pair(gbuf_ref[rows, :])
            xb_ref[rows, :half] = hi.astype(BF16)
            xb_ref[rows, half:] = lo.astype(BF16)
            return 0

        lax.fori_loop(0, nb, unpack, 0)

        @pl.when(p + 1 < n_pass)
        def _():
            _gather_rows(hp_hbm, tok_ref, (p + 1) * cap, nb_ref[p + 1] * ROW_BLOCK, gbuf_ref, sem)

    @pl.when(live)
    def _():
        def block(row0, m):
            rows = pl.ds(row0, m)
            x = xb_ref[rows, :]
            hg = jnp.dot(x, wg_ref[...].astype(BF16), preferred_element_type=F32) + bg_ref[...]
            hu = jnp.dot(x, wu_ref[...].astype(BF16), preferred_element_type=F32) + bu_ref[...]
            hg = jnp.minimum(hg, SWIGLU_LIMIT)
            hu = jnp.clip(hu, -SWIGLU_LIMIT, SWIGLU_LIMIT)
            act = hg * _sigmoid(SWIGLU_ALPHA * hg) * (hu + 1.0)
            o_ref[rows, :] = act.astype(o_ref.dtype)

        _for_row_blocks(nb, block)


def _moe_up(pe, pass_nb, pass_valid, n_pass, pos_flat, hp, wg, wu, bg, bu, n_pass_max, tf):
    e, d, f = wg.shape
    nf = f // tf
    cap = PASS_ROWS

    def w_map(p, j, pe_, nb_, valid_, np_, pos_):
        return (pe_[p], 0, jnp.where(p < np_[0], j, nf - 1))

    def o_map(p, j, pe_, nb_, valid_, np_, pos_):
        ok = p < np_[0]
        return (jnp.where(ok, p, np_[0] - 1), jnp.where(ok, j, nf - 1))

    grid_spec = pltpu.PrefetchScalarGridSpec(
        num_scalar_prefetch=5,
        grid=(n_pass_max, nf),
        in_specs=[pl.BlockSpec(memory_space=pl.ANY),
                  pl.BlockSpec((None, d, tf), w_map),
                  pl.BlockSpec((None, d, tf), w_map),
                  pl.BlockSpec((None, 1, tf), w_map),
                  pl.BlockSpec((None, 1, tf), w_map)],
        out_specs=pl.BlockSpec((cap, tf), o_map),
        scratch_shapes=[pltpu.VMEM((cap, d // 2), U32),
                        pltpu.VMEM((cap, d), BF16),
                        pltpu.SMEM((n_pass_max * cap,), jnp.int32),
                        pltpu.SemaphoreType.DMA(())],
    )
    return pl.pallas_call(
        _moe_up_kernel,
        grid_spec=grid_spec,
        out_shape=jax.ShapeDtypeStruct((n_pass_max * cap, f), BF16),
        compiler_params=_params("arbitrary", "arbitrary"),
        name="moe_up",
    )(pe, pass_nb, pass_valid, n_pass, pos_flat, hp, wg, wu, bg.reshape(e, 1, f),
      bu.reshape(e, 1, f))


def _moe_down_kernel(pe_ref, nb_ref, npass_ref, a_ref, wh_ref, wl_ref, bh_ref, bl_ref, o_ref):
    p = pl.program_id(0)

    @pl.when(p < npass_ref[0])
    def _():
        def block(row0, m):
            rows = pl.ds(row0, m)
            a = a_ref[rows, :]
            hi = jnp.dot(a, wh_ref[...].astype(BF16), preferred_element_type=F32) + bh_ref[...]
            lo = jnp.dot(a, wl_ref[...].astype(BF16), preferred_element_type=F32) + bl_ref[...]
            o_ref[rows, :] = _pack_bf16_pair(hi, lo)

        _for_row_blocks(nb_ref[p], block)


def _moe_down(pe, pass_nb, n_pass, act, wd, bd, n_pass_max, td):
    e, f, d = wd.shape
    nd = d // 2 // td
    cap = PASS_ROWS

    def col(p, j, np_):
        return jnp.where(p < np_[0], j, nd - 1)

    def row(p, np_):
        return jnp.where(p < np_[0], p, np_[0] - 1)

    grid_spec = pltpu.PrefetchScalarGridSpec(
        num_scalar_prefetch=3,
        grid=(n_pass_max, nd),
        in_specs=[pl.BlockSpec((cap, f), lambda p, j, pe_, nb_, np_: (row(p, np_), 0)),
                  pl.BlockSpec((None, f, td), lambda p, j, pe_, nb_, np_: (pe_[p], 0, col(p, j, np_))),
                  pl.BlockSpec((None, f, td), lambda p, j, pe_, nb_, np_: (pe_[p], 0, nd + col(p, j, np_))),
                  pl.BlockSpec((None, 1, td), lambda p, j, pe_, nb_, np_: (pe_[p], 0, col(p, j, np_))),
                  pl.BlockSpec((None, 1, td), lambda p, j, pe_, nb_, np_: (pe_[p], 0, nd + col(p, j, np_)))],
        out_specs=pl.BlockSpec((cap, td), lambda p, j, pe_, nb_, np_: (row(p, np_), col(p, j, np_))),
    )
    bd3 = bd.reshape(e, 1, d)
    return pl.pallas_call(
        _moe_down_kernel,
        grid_spec=grid_spec,
        out_shape=jax.ShapeDtypeStruct((n_pass_max * cap, d // 2), U32),
        compiler_params=_params("arbitrary", "arbitrary"),
        name="moe_down",
    )(pe, pass_nb, n_pass, act, wd, wd, bd3, bd3)


def _combine_kernel(pos_ref, y_hbm, x_ref, gate_ref, g_ref, o_ref, ybuf_ref, sems):
    i = pl.program_id(0)
    n = pl.num_programs(0)
    tt, d = x_ref.shape
    half = d // 2
    slot = i % 2

    n_tok = n * tt

    def start_tile(step, slot_):
        for k in range(TOP_K):
            _gather_rows(y_hbm, pos_ref, k * n_tok + step * tt, tt, ybuf_ref.at[slot_, k],
                         sems.at[slot_])

    @pl.when(i == 0)
    def _():
        start_tile(0, 0)

    @pl.when(i + 1 < n)
    def _():
        start_tile(i + 1, 1 - slot)

    for k in range(TOP_K):
        pltpu.make_async_copy(y_hbm.at[pl.ds(0, tt)], ybuf_ref.at[slot, k], sems.at[slot]).wait()

    gates = gate_ref[...]
    x = x_ref[...]
    acc_hi = x[:, :half]
    acc_lo = x[:, half:]
    for k in range(TOP_K):
        hi, lo = _unpack_bf16_pair(ybuf_ref[slot, k])
        acc_hi = acc_hi + gates[:, k:k + 1] * hi
        acc_lo = acc_lo + gates[:, k:k + 1] * lo
    ms = (jnp.sum(acc_hi * acc_hi, axis=-1, keepdims=True)
          + jnp.sum(acc_lo * acc_lo, axis=-1, keepdims=True)) * (1.0 / d)
    scale = lax.rsqrt(ms + RMS_EPS)
    o_ref[:, :half] = acc_hi * scale * g_ref[:, :half]
    o_ref[:, half:] = acc_lo * scale * g_ref[:, half:]


def _combine(pos_flat, y, x, gates, g, tt):
    t, d = x.shape
    grid_spec = pltpu.PrefetchScalarGridSpec(
        num_scalar_prefetch=1,
        grid=(t // tt,),
        in_specs=[pl.BlockSpec(memory_space=pl.ANY),
                  pl.BlockSpec((tt, d), lambda i, p: (i, 0)),
                  pl.BlockSpec((tt, TOP_K), lambda i, p: (i, 0)),
                  pl.BlockSpec((1, d), lambda i, p: (0, 0))],
        out_specs=pl.BlockSpec((tt, d), lambda i, p: (i, 0)),
        scratch_shapes=[pltpu.VMEM((2, TOP_K, tt, d // 2), U32),
                        pltpu.SemaphoreType.DMA((2,))],
    )
    return pl.pallas_call(
        _combine_kernel,
        grid_spec=grid_spec,
        out_shape=jax.ShapeDtypeStruct((t, d), F32),
        compiler_params=_params("arbitrary"),
        name="combine",
    )(pos_flat, y, x, gates, g.reshape(1, d))


class _Tiles(NamedTuple):
    norm_rows: int = 512
    inproj: tuple = (512, 1024)
    outproj: tuple = (1024, 512)
    router_rows: int = 512
    up_cols: int = 256
    down_cols: int = 512
    combine_rows: int = 256


TILES = _Tiles()


def _moe_and_final_norm(x_mid, norm2_g, w_router, b_router, w_gate, b_gate, w_up, b_up,
                        w_down, b_down, final_norm_g):
    t = x_mid.shape[0]
    n_pass_max = _num_passes(t * TOP_K)
    hp, idx, gates, rank, counts = _router(x_mid, norm2_g, w_router, b_router,
                                           tt=TILES.router_rows)
    pe, pass_nb, pass_valid, n_pass, pos_flat = _routing_tables(idx, rank, counts[0], n_pass_max)
    act = _moe_up(pe, pass_nb, pass_valid, n_pass, pos_flat, hp, w_gate, w_up, b_gate, b_up,
                  n_pass_max, tf=TILES.up_cols)
    y_rows = _moe_down(pe, pass_nb, n_pass, act, w_down, b_down, n_pass_max, td=TILES.down_cols)
    return _combine(pos_flat, y_rows, x_mid, gates, final_norm_g, tt=TILES.combine_rows)


def kernel(x, meta_tokens, norm1_g, w_in, lru_conv_w, lru_conv_b, lru_w_a, lru_b_a, lru_w_i, lru_b_i, lru_lambda, conf_conv_w, conf_conv_b, conf_norm_g, conf_norm_b, w_out, norm2_g, w_router, b_router, w_gate, b_gate, w_up, b_up, w_down, b_down, final_norm_g):
    b, s, d = x.shape
    t = b * s
    assert norm1_g.shape[0] == 1, "one layer"
    assert d == LRU_WIDTH + CONF_WIDTH == LRU_HEADS * HEAD_DIM + CONF_GROUPS * HEAD_DIM
    assert s % (LRU_CHUNKS * SUBLANES) == 0 and s % CONF_CHUNK == 0
    assert all(t % rows == 0 for rows in (TILES.norm_rows, TILES.inproj[0], TILES.outproj[0],
                                          TILES.router_rows, TILES.combine_rows))
    x2 = x.reshape(t, d)

    h = _rmsnorm(x2, norm1_g[0], tm=TILES.norm_rows)
    h_meta = _rmsnorm(meta_tokens.astype(x.dtype), norm1_g[0], tm=N_META)
    z, z_meta = _inproj(h, h_meta, w_in[0], *TILES.inproj)
    z3 = z.reshape(b, s, -1)

    y_lru = _lru_mixer(z3, z_meta, lru_conv_w[0], lru_conv_b[0], lru_w_a[0], lru_b_a[0],
                       lru_w_i[0], lru_b_i[0], lru_lambda[0])
    y_conf = _conf_mixer(z3, z_meta, conf_conv_w[0], conf_conv_b[0], conf_norm_g[0],
                         conf_norm_b[0])
    x_mid = _outproj(y_lru.reshape(t, -1), y_conf.reshape(t, -1), w_out[0], x2, *TILES.outproj)

    out = _moe_and_final_norm(x_mid, norm2_g[0], w_router[0], b_router[0], w_gate[0], b_gate[0],
                              w_up[0], b_up[0], w_down[0], b_down[0], final_norm_g)
    return out.reshape(b, s, d)
```

```python
from typing import NamedTuple

import jax
import jax.numpy as jnp
from jax import lax
from jax.experimental import pallas as pl
from jax.experimental.pallas import tpu as pltpu

N_META = 16
LRU_WIDTH = 2048
LRU_HEADS = 16
HEAD_DIM = 128
LRU_CONV_WIDTH = 4
LRU_C = 8.0
CONF_WIDTH = 2048
CONF_GROUPS = 16
CONF_KERNEL = 31
N_EXPERTS = 32
TOP_K = 4
SWIGLU_ALPHA = 1.702
SWIGLU_LIMIT = 7.0
RMS_EPS = 1e-5
LN_EPS = 1e-5
SQRT_FLOOR = 1e-30

SUBLANES = 8
V7X_VMEM_BYTES = 64 * 1024 * 1024
VMEM_LIMIT = V7X_VMEM_BYTES - 6 * 1024 * 1024

ROW_BLOCK = 128
MATMUL_ROWS = 1536
PASS_ROWS = 1536
BF16 = jnp.bfloat16
F32 = jnp.float32
U32 = jnp.uint32


def _params(*sem):
    return pltpu.CompilerParams(dimension_semantics=sem, vmem_limit_bytes=VMEM_LIMIT)


def _pack_bf16_pair(hi, lo):
    hi_bits = lax.bitcast_convert_type(hi.astype(BF16).astype(F32), U32)
    lo_bits = lax.bitcast_convert_type(lo.astype(BF16).astype(F32), U32)
    return hi_bits | lax.shift_right_logical(lo_bits, jnp.uint32(16))


def _sigmoid(x):
    return 0.5 * jnp.tanh(0.5 * x) + 0.5


def _unpack_bf16_pair(u):
    hi = lax.bitcast_convert_type(u & jnp.uint32(0xFFFF0000), F32)
    lo = lax.bitcast_convert_type(lax.shift_left(u, jnp.uint32(16)), F32)
    return hi, lo


def _rmsnorm_kernel(x_ref, g_ref, o_ref):
    x = x_ref[...]
    ms = jnp.mean(x * x, axis=-1, keepdims=True)
    o_ref[...] = (x * lax.rsqrt(ms + RMS_EPS) * g_ref[...]).astype(o_ref.dtype)


def _rmsnorm(x, g, tm):
    t, d = x.shape
    return pl.pallas_call(
        _rmsnorm_kernel,
        grid=(t // tm,),
        in_specs=[pl.BlockSpec((tm, d), lambda i: (i, 0)),
                  pl.BlockSpec((1, d), lambda i: (0, 0))],
        out_specs=pl.BlockSpec((tm, d), lambda i: (i, 0)),
        out_shape=jax.ShapeDtypeStruct((t, d), BF16),
        compiler_params=_params("parallel"),
        name="rmsnorm1",
    )(x, g.reshape(1, d))


def _inproj_kernel(a_ref, am_ref, w_ref, o_ref, om_ref, wb_ref):
    @pl.when(pl.program_id(1) == 0)
    def _():
        wb_ref[...] = w_ref[...].astype(BF16)
        om_ref[...] = jnp.dot(am_ref[...], wb_ref[...], preferred_element_type=F32)

    o_ref[...] = jnp.dot(a_ref[...], wb_ref[...], preferred_element_type=F32)


def _inproj(a, a_meta, w, tm, tn):
    t, k = a.shape
    n = w.shape[1]
    return pl.pallas_call(
        _inproj_kernel,
        grid=(n // tn, t // tm),
        in_specs=[pl.BlockSpec((tm, k), lambda j, i: (i, 0)),
                  pl.BlockSpec((N_META, k), lambda j, i: (0, 0)),
                  pl.BlockSpec((k, tn), lambda j, i: (0, j))],
        out_specs=[pl.BlockSpec((tm, tn), lambda j, i: (i, j)),
                   pl.BlockSpec((N_META, tn), lambda j, i: (0, j))],
        out_shape=[jax.ShapeDtypeStruct((t, n), F32),
                   jax.ShapeDtypeStruct((N_META, n), F32)],
        scratch_shapes=[pltpu.VMEM((k, tn), BF16)],
        compiler_params=_params("parallel", "arbitrary"),
        name="inproj",
    )(a, a_meta, w)


LRU_CHUNKS = SUBLANES
LRU_CHUNK_PAD = SUBLANES
LRU_SCAN_UNROLL = 8


def _gelu_tanh(x):
    return 0.5 * x * (1.0 + jnp.tanh(0.7978845608028654 * (x + 0.044715 * x * x * x)))


def _lru_kernel(xr_ref, gate_ref, xm_ref, cw_ref, cb_ref, wa_ref, ba_ref, wi_ref,
                bi_ref, lam_ref, o_ref, sf_ref, af_ref, bf_ref, ab_ref, bb_ref, hf_ref, hb_ref):
    s = xr_ref.shape[0]
    st = s + N_META
    pad = SUBLANES
    clen = s // LRU_CHUNKS
    pitch = clen + LRU_CHUNK_PAD
    zeros8 = jnp.zeros((pad, HEAD_DIM), F32)
    sf_ref[0:pad, :] = zeros8
    sf_ref[pad:pad + N_META, :] = xm_ref[...]
    sf_ref[pad + N_META:pad + st, :] = xr_ref[...]
    sf_ref[pad + st:pad + st + pad, :] = zeros8

    def gates(u, d):
        ub = u.astype(BF16)
        t_r = jnp.tanh(jnp.dot(ub, (0.5 * wa_ref[d]).astype(BF16), preferred_element_type=F32)
                       + 0.5 * ba_ref[d:d + 1, :])
        t_i = jnp.tanh(jnp.dot(ub, (0.5 * wi_ref[d]).astype(BF16), preferred_element_type=F32)
                       + 0.5 * bi_ref[d:d + 1, :])
        lam = lam_ref[d:d + 1, :]
        softplus_neg = jnp.maximum(-lam, 0.0) + jnp.log1p(jnp.exp(-jnp.abs(lam)))
        half_c = (-0.5 * LRU_C) * softplus_neg
        log_a = half_c * t_r + half_c
        a = jnp.exp(log_a)
        one_minus_a2 = -jnp.tanh(log_a) * (a * a + 1.0)
        root = one_minus_a2 * lax.rsqrt(jnp.maximum(one_minus_a2, SQRT_FLOOR))
        half_u = 0.5 * u
        b = root * (half_u * t_i + half_u)
        return a, b

    uf = cb_ref[0:1, :] + cw_ref[0, 3:4, :] * sf_ref[pl.ds(pad, st), :]
    for j in range(1, LRU_CONV_WIDTH):
        uf = uf + cw_ref[0, 3 - j:4 - j, :] * sf_ref[pl.ds(pad - j, st), :]
    a, b = gates(uf, 0)
    h_meta = b[0:1, :]
    for r in range(1, N_META):
        h_meta = a[r:r + 1, :] * h_meta + b[r:r + 1, :]
    for c in range(LRU_CHUNKS):
        af_ref[c * pitch:c * pitch + clen, :] = a[N_META + c * clen:N_META + (c + 1) * clen, :]
        bf_ref[c * pitch:c * pitch + clen, :] = b[N_META + c * clen:N_META + (c + 1) * clen, :]

    base = pad + N_META
    ub_ = cb_ref[1:2, :] + cw_ref[1, 3:4, :] * sf_ref[pl.ds(base, s), :]
    for j in range(1, LRU_CONV_WIDTH):
        ub_ = ub_ + cw_ref[1, 3 - j:4 - j, :] * sf_ref[pl.ds(base + j, s), :]
    a, b = gates(ub_, 1)
    for c in range(LRU_CHUNKS):
        ab_ref[c * pitch:c * pitch + clen, :] = a[c * clen:(c + 1) * clen, :]
        bb_ref[c * pitch:c * pitch + clen, :] = b[c * clen:(c + 1) * clen, :]

    def step_rows(t):
        return (pl.ds(t, LRU_CHUNKS, stride=pitch),
                pl.ds(clen - 1 - t, LRU_CHUNKS, stride=pitch))

    def local_scan(t, carry):
        hf, pf, hb, pb = carry
        rf, rb = step_rows(t)
        a_f = af_ref[rf, :]
        a_b = ab_ref[rb, :]
        return (a_f * hf + bf_ref[rf, :], a_f * pf, a_b * hb + bb_ref[rb, :], a_b * pb)

    zero = sf_ref[0:SUBLANES, :]
    one = zero + 1.0
    hf, pf, hb, pb = lax.fori_loop(0, clen, local_scan, (zero, one, zero, one),
                                   unroll=LRU_SCAN_UNROLL)

    row = lax.broadcasted_iota(jnp.int32, (LRU_CHUNKS, HEAD_DIM), 0)
    c = h_meta
    start_f = jnp.where(row == 0, c, zero)
    for k in range(1, LRU_CHUNKS):
        c = hf[k - 1:k, :] + pf[k - 1:k, :] * c
        start_f = jnp.where(row == k, c, start_f)
    c = zero[0:1, :]
    start_b = zero
    for k in range(LRU_CHUNKS - 2, -1, -1):
        c = hb[k + 1:k + 2, :] + pb[k + 1:k + 2, :] * c
        start_b = jnp.where(row == k, c, start_b)

    def final_scan(t, carry):
        hf, hb = carry
        rf, rb = step_rows(t)
        hf = af_ref[rf, :] * hf + bf_ref[rf, :]
        hb = ab_ref[rb, :] * hb + bb_ref[rb, :]
        hf_ref[rf, :] = hf
        hb_ref[rb, :] = hb
        return hf, hb

    lax.fori_loop(0, clen, final_scan, (start_f, start_b), unroll=LRU_SCAN_UNROLL)
    for c in range(LRU_CHUNKS):
        hsum = hf_ref[c * pitch:c * pitch + clen, :] + hb_ref[c * pitch:c * pitch + clen, :]
        y = hsum * _gelu_tanh(gate_ref[c * clen:(c + 1) * clen, :])
        o_ref[c * clen:(c + 1) * clen, :] = y.astype(o_ref.dtype)


def _lru_mixer(z3, z_meta, cw, cb, wa, ba, wi, bi, lam):
    b, s, _ = z3.shape
    st = s + N_META
    hd = HEAD_DIM
    col = lambda off: (lambda bi_, h: (bi_, 0, off + h))
    return pl.pallas_call(
        _lru_kernel,
        grid=(b, LRU_HEADS),
        in_specs=[
            pl.BlockSpec((None, s, hd), col(0)),
            pl.BlockSpec((None, s, hd), col(LRU_HEADS)),
            pl.BlockSpec((N_META, hd), lambda bi_, h: (0, h)),
            pl.BlockSpec((2, LRU_CONV_WIDTH, hd), lambda bi_, h: (0, 0, h)),
            pl.BlockSpec((2, hd), lambda bi_, h: (0, h)),
            pl.BlockSpec((2, None, hd, hd), lambda bi_, h: (0, h, 0, 0)),
            pl.BlockSpec((2, hd), lambda bi_, h: (0, h)),
            pl.BlockSpec((2, None, hd, hd), lambda bi_, h: (0, h, 0, 0)),
            pl.BlockSpec((2, hd), lambda bi_, h: (0, h)),
            pl.BlockSpec((2, hd), lambda bi_, h: (0, h)),
        ],
        out_specs=pl.BlockSpec((None, s, hd), lambda bi_, h: (bi_, 0, h)),
        out_shape=jax.ShapeDtypeStruct((b, s, LRU_WIDTH), BF16),
        scratch_shapes=[
            pltpu.VMEM((st + 2 * SUBLANES, hd), F32),
        ] + [pltpu.VMEM((LRU_CHUNKS * (s // LRU_CHUNKS + LRU_CHUNK_PAD), hd), F32)] * 6,
        compiler_params=_params("parallel", "parallel"),
        name="lru_mixer",
    )(z3, z3, z_meta, cw, cb, wa, ba.reshape(2, LRU_WIDTH), wi, bi.reshape(2, LRU_WIDTH), lam)


CONF_CHUNK = 64
CONF_PARTIAL_SUMS = 2


def _conf_kernel(a_ref, b_ref, am_ref, bm_ref, cw_ref, cb_ref, g_ref, be_ref, o_ref,
                 cs_ref, sh_ref):
    s = a_ref.shape[0]
    st = s + N_META
    n_sh = sh_ref.shape[1]
    cs_ref[0:N_META, :] = am_ref[...] * _sigmoid(bm_ref[...])
    cs_ref[N_META:st, :] = a_ref[...] * _sigmoid(b_ref[...])
    cs_ref[st:st + N_META, :] = jnp.zeros((N_META, HEAD_DIM), F32)
    for r in range(1, SUBLANES):
        sh_ref[r - 1] = cs_ref[pl.ds(r, n_sh), :]

    n_chunks = s // CONF_CHUNK

    def conv_chunk(row0):
        parts = [None] * CONF_PARTIAL_SUMS
        for k in range(CONF_KERNEL):
            off = N_META - CONF_KERNEL // 2 + k
            r, q = off % SUBLANES, off // SUBLANES
            rows = pl.ds(row0 + q * SUBLANES, CONF_CHUNK)
            term = cw_ref[k:k + 1, :] * (cs_ref[rows, :] if r == 0 else sh_ref[r - 1, rows, :])
            p = k % CONF_PARTIAL_SUMS
            parts[p] = term if parts[p] is None else parts[p] + term
        return sum(parts[1:], parts[0]) + cb_ref[...]

    def step(j, carry):
        acc_prev, xc_prev = carry
        mean = jnp.mean(acc_prev, axis=-1, keepdims=True)
        var = jnp.mean(xc_prev * xc_prev, axis=-1, keepdims=True)
        acc = conv_chunk(pl.multiple_of(jnp.minimum(j, n_chunks - 1) * CONF_CHUNK, CONF_CHUNK))
        y = xc_prev * lax.rsqrt(var + LN_EPS) * g_ref[...] + be_ref[...]
        out_row = pl.multiple_of(jnp.maximum(j - 2, 0) * CONF_CHUNK, CONF_CHUNK)
        half_y = 0.5 * y
        o_ref[pl.ds(out_row, CONF_CHUNK), :] = (half_y * jnp.tanh(half_y) + half_y).astype(o_ref.dtype)
        return acc, acc_prev - mean

    warmup = cs_ref[pl.ds(0, CONF_CHUNK), :]
    lax.fori_loop(0, n_chunks + 2, step, (warmup, warmup))


def _conf_mixer(z3, z_meta, cw, cb, g, be):
    b, s, _ = z3.shape
    st = s + N_META
    hd = HEAD_DIM
    a_off = 2 * LRU_HEADS
    b_off = 2 * LRU_HEADS + CONF_GROUPS
    n_sh = st + N_META - SUBLANES
    vec = pl.BlockSpec((1, hd), lambda bi_, h: (0, h))
    return pl.pallas_call(
        _conf_kernel,
        grid=(b, CONF_GROUPS),
        in_specs=[
            pl.BlockSpec((None, s, hd), lambda bi_, h: (bi_, 0, a_off + h)),
            pl.BlockSpec((None, s, hd), lambda bi_, h: (bi_, 0, b_off + h)),
            pl.BlockSpec((N_META, hd), lambda bi_, h: (0, a_off + h)),
            pl.BlockSpec((N_META, hd), lambda bi_, h: (0, b_off + h)),
            pl.BlockSpec((CONF_KERNEL, hd), lambda bi_, h: (0, h)),
            vec, vec, vec,
        ],
        out_specs=pl.BlockSpec((None, s, hd), lambda bi_, h: (bi_, 0, h)),
        out_shape=jax.ShapeDtypeStruct((b, s, CONF_WIDTH), BF16),
        scratch_shapes=[
            pltpu.VMEM((st + N_META, hd), F32),
            pltpu.VMEM((SUBLANES - 1, n_sh, hd), F32),
        ],
        compiler_params=_params("parallel", "parallel"),
        name="conf_mixer",
    )(z3, z3, z_meta, z_meta, cw, cb.reshape(1, -1), g.reshape(1, -1), be.reshape(1, -1))


def _outproj_kernel(ya_ref, yb_ref, wa_ref, wb_ref, x_ref, o_ref):
    acc = jnp.dot(ya_ref[...], wa_ref[...].astype(BF16), preferred_element_type=F32)
    acc = acc + jnp.dot(yb_ref[...], wb_ref[...].astype(BF16), preferred_element_type=F32)
    o_ref[...] = x_ref[...] + acc


def _outproj(ya, yb, w, x, tm, tn):
    t, k = ya.shape
    n = w.shape[1]
    return pl.pallas_call(
        _outproj_kernel,
        grid=(t // tm, n // tn),
        in_specs=[pl.BlockSpec((tm, k), lambda i, j: (i, 0)),
                  pl.BlockSpec((tm, k), lambda i, j: (i, 0)),
                  pl.BlockSpec((k, tn), lambda i, j: (0, j)),
                  pl.BlockSpec((k, tn), lambda i, j: (1, j)),
                  pl.BlockSpec((tm, tn), lambda i, j: (i, j))],
        out_specs=pl.BlockSpec((tm, tn), lambda i, j: (i, j)),
        out_shape=jax.ShapeDtypeStruct((t, n), F32),
        compiler_params=_params("parallel", "parallel"),
        name="outproj",
    )(ya, yb, w, w, x)


def _router_kernel(x_ref, g_ref, wr_ref, br_ref, hp_ref, idx_ref, gate_ref, rank_ref,
                   cnt_ref, carry_ref):
    tt, d = x_ref.shape
    half = d // 2

    @pl.when(pl.program_id(0) == 0)
    def _():
        carry_ref[...] = jnp.zeros_like(carry_ref)

    x = x_ref[...]
    ms = jnp.mean(x * x, axis=-1, keepdims=True)
    h = x * lax.rsqrt(ms + RMS_EPS) * g_ref[...]
    hp_ref[...] = _pack_bf16_pair(h[:, :half], h[:, half:])

    h_hi = h.astype(BF16)
    h_lo = (h - h_hi.astype(F32)).astype(BF16)
    w = wr_ref[...]
    w_hi = w.astype(BF16)
    w_lo = (w - w_hi.astype(F32)).astype(BF16)
    hi_terms = jnp.dot(h_hi, jnp.concatenate([w_hi, w_lo], axis=1), preferred_element_type=F32)
    logits = (hi_terms[:, :N_EXPERTS] + hi_terms[:, N_EXPERTS:]
              + jnp.dot(h_lo, w_hi, preferred_element_type=F32)) + br_ref[...]

    lane = lax.broadcasted_iota(jnp.int32, (tt, N_EXPERTS), 1)
    lane_k = lax.broadcasted_iota(jnp.int32, (tt, TOP_K), 1)
    work = logits
    vals, sels = [], []
    idx_out = jnp.zeros((tt, TOP_K), jnp.int32)
    for k in range(TOP_K):
        m = jnp.max(work, axis=1, keepdims=True)
        am = jnp.min(jnp.where(work == m, lane, N_EXPERTS), axis=1, keepdims=True)
        sel = lane == am
        vals.append(m)
        sels.append(sel)
        idx_out = jnp.where(lane_k == k, am, idx_out)
        work = jnp.where(sel, -jnp.inf, work)
    idx_ref[...] = idx_out

    exps = [jnp.exp(v - vals[0]) for v in vals]
    denom = exps[0] + exps[1] + exps[2] + exps[3]
    gate_out = jnp.zeros((tt, TOP_K), F32)
    for k in range(TOP_K):
        gate_out = jnp.where(lane_k == k, exps[k] / denom, gate_out)
    gate_ref[...] = gate_out

    onehot = jnp.zeros((tt, N_EXPERTS), F32)
    for sel in sels:
        onehot = onehot + sel.astype(F32)
    r_i = lax.broadcasted_iota(jnp.int32, (tt, tt), 0)
    c_i = lax.broadcasted_iota(jnp.int32, (tt, tt), 1)
    tri = (c_i < r_i).astype(BF16)
    before = jnp.dot(tri, onehot.astype(BF16), preferred_element_type=F32) + carry_ref[...]
    rank_out = jnp.zeros((tt, TOP_K), jnp.int32)
    for k, sel in enumerate(sels):
        rk = jnp.sum(jnp.where(sel, before, 0.0), axis=1, keepdims=True).astype(jnp.int32)
        rank_out = jnp.where(lane_k == k, rk, rank_out)
    rank_ref[...] = rank_out
    carry_ref[...] = carry_ref[...] + jnp.sum(onehot, axis=0, keepdims=True)
    cnt_ref[...] = carry_ref[...].astype(jnp.int32)


def _router(x, g, wr, br, tt):
    t, d = x.shape
    small = lambda dt: jax.ShapeDtypeStruct((t, TOP_K), dt)
    return pl.pallas_call(
        _router_kernel,
        grid=(t // tt,),
        in_specs=[pl.BlockSpec((tt, d), lambda i: (i, 0)),
                  pl.BlockSpec((1, d), lambda i: (0, 0)),
                  pl.BlockSpec((d, N_EXPERTS), lambda i: (0, 0)),
                  pl.BlockSpec((1, N_EXPERTS), lambda i: (0, 0))],
        out_specs=[pl.BlockSpec((tt, d // 2), lambda i: (i, 0)),
                   pl.BlockSpec((tt, TOP_K), lambda i: (i, 0)),
                   pl.BlockSpec((tt, TOP_K), lambda i: (i, 0)),
                   pl.BlockSpec((tt, TOP_K), lambda i: (i, 0)),
                   pl.BlockSpec((1, N_EXPERTS), lambda i: (0, 0))],
        out_shape=[jax.ShapeDtypeStruct((t, d // 2), U32),
                   small(jnp.int32), small(F32), small(jnp.int32),
                   jax.ShapeDtypeStruct((1, N_EXPERTS), jnp.int32)],
        scratch_shapes=[pltpu.VMEM((1, N_EXPERTS), F32)],
        compiler_params=_params("arbitrary"),
        name="router",
    )(x, g.reshape(1, d), wr, br.reshape(1, N_EXPERTS))


def _num_passes(n_assign):
    return N_EXPERTS + n_assign // PASS_ROWS


def _per_expert(table, experts):
    onehot = experts[..., None] == jnp.arange(N_EXPERTS, dtype=experts.dtype)
    return jnp.sum(jnp.where(onehot, table, 0), axis=-1)


def _routing_tables(idx, rank, counts, n_pass_max):
    padded = (counts + ROW_BLOCK - 1) // ROW_BLOCK * ROW_BLOCK
    passes_e = (padded + PASS_ROWS - 1) // PASS_ROWS
    pass_end = jnp.cumsum(passes_e)
    pass_start = pass_end - passes_e
    n_pass = pass_end[-1]
    pos = (_per_expert(pass_start, idx) + rank // PASS_ROWS) * PASS_ROWS + rank % PASS_ROWS
    p_ids = jnp.arange(n_pass_max, dtype=jnp.int32)
    live = p_ids < n_pass
    pe = jnp.sum(jnp.minimum(p_ids, n_pass - 1)[:, None] >= pass_end[None, :], axis=1)
    pe = jnp.minimum(pe, N_EXPERTS - 1).astype(jnp.int32)
    done = (p_ids - _per_expert(pass_start, pe)) * PASS_ROWS
    rows = jnp.clip(_per_expert(padded, pe) - done, 0, PASS_ROWS)
    pass_nb = jnp.where(live, rows // ROW_BLOCK, 0).astype(jnp.int32)
    pass_valid = jnp.where(live, jnp.clip(_per_expert(counts, pe) - done, 0, PASS_ROWS),
                           0).astype(jnp.int32)
    return (pe, pass_nb, pass_valid, n_pass.reshape(1).astype(jnp.int32),
            pos.T.reshape(-1).astype(jnp.int32))


GATHER_UNROLL = SUBLANES


def _gather_rows(src_hbm, idx_ref, idx_base, n_rows, dst_ref, sem):
    def issue(g, _):
        r0 = pl.multiple_of(g * GATHER_UNROLL, GATHER_UNROLL)
        dst_tile = dst_ref.at[pl.ds(r0, GATHER_UNROLL)]
        for u in range(GATHER_UNROLL):
            pltpu.make_async_copy(src_hbm.at[pl.ds(idx_ref[idx_base + r0 + u], 1)],
                                  dst_tile.at[pl.ds(u, 1)], sem).start()
        return 0

    lax.fori_loop(0, n_rows // GATHER_UNROLL, issue, 0)


def _for_row_blocks(n_blocks, body):
    per_big = MATMUL_ROWS // ROW_BLOCK
    n_big = n_blocks // per_big

    def big(i, _):
        body(pl.multiple_of(i * MATMUL_ROWS, MATMUL_ROWS), MATMUL_ROWS)
        return 0

    lax.fori_loop(0, n_big, big, 0)
    rest = n_blocks - n_big * per_big
    start = pl.multiple_of(n_big * MATMUL_ROWS, MATMUL_ROWS)
    for m in range(1, per_big):
        @pl.when(rest == m)
        def _(m=m):
            body(start, m * ROW_BLOCK)


def _wait_row_blocks(src_hbm, dst_ref, n_blocks, rows, sem):
    def drain(i, _):
        pltpu.make_async_copy(src_hbm.at[pl.ds(0, rows)], dst_ref.at[pl.ds(0, rows)], sem).wait()
        return 0

    lax.fori_loop(0, n_blocks, drain, 0)


def _build_token_table(pos_ref, nb_ref, valid_ref, n_pass, cap, tok_ref):
    def pad_pass(q, _):
        def pad_row(r, _):
            tok_ref[q * cap + r] = 0
            return 0

        lax.fori_loop(valid_ref[q], nb_ref[q] * ROW_BLOCK, pad_row, 0)
        return 0

    lax.fori_loop(0, n_pass, pad_pass, 0)
    n_tok = pos_ref.shape[0] // TOP_K
    for k in range(TOP_K):
        def place(g, _, k=k):
            for u in range(GATHER_UNROLL):
                t = g * GATHER_UNROLL + u
                tok_ref[pos_ref[k * n_tok + t]] = t
            return 0

        lax.fori_loop(0, n_tok // GATHER_UNROLL, place, 0)


def _moe_up_kernel(pe_ref, nb_ref, valid_ref, npass_ref, pos_ref, hp_hbm, wg_ref, wu_ref, bg_ref,
                   bu_ref, o_ref, gbuf_ref, xb_ref, tok_ref, sem):
    p = pl.program_id(0)
    f = pl.program_id(1)
    cap, half = gbuf_ref.shape
    n_pass = npass_ref[0]
    live = p < n_pass
    nb = nb_ref[p]

    @pl.when(jnp.logical_and(live, f == 0))
    def _():
        @pl.when(p == 0)
        def _():
            _build_token_table(pos_ref, nb_ref, valid_ref, n_pass, cap, tok_ref)
            _gather_rows(hp_hbm, tok_ref, 0, nb * ROW_BLOCK, gbuf_ref, sem)

        _wait_row_blocks(hp_hbm, gbuf_ref, nb, ROW_BLOCK, sem)

        def unpack(rb, _):
            rows = pl.ds(pl.multiple_of(rb * ROW_BLOCK, ROW_BLOCK), ROW_BLOCK)
            hi, lo = _unpack_bf16_pair(gbuf_ref[rows, :])
            xb_ref[rows, :half] = hi.astype(BF16)
            xb_ref[rows, half:] = lo.astype(BF16)
            return 0

        lax.fori_loop(0, nb, unpack, 0)

        @pl.when(p + 1 < n_pass)
        def _():
            _gather_rows(hp_hbm, tok_ref, (p + 1) * cap, nb_ref[p + 1] * ROW_BLOCK, gbuf_ref, sem)

    @pl.when(live)
    def _():
        def block(row0, m):
            rows = pl.ds(row0, m)
            x = xb_ref[rows, :]
            hg = jnp.dot(x, wg_ref[...].astype(BF16), preferred_element_type=F32) + bg_ref[...]
            hu = jnp.dot(x, wu_ref[...].astype(BF16), preferred_element_type=F32) + bu_ref[...]
            hg = jnp.minimum(hg, SWIGLU_LIMIT)
            hu = jnp.clip(hu, -SWIGLU_LIMIT, SWIGLU_LIMIT)
            act = hg * _sigmoid(SWIGLU_ALPHA * hg) * (hu + 1.0)
            o_ref[rows, :] = act.astype(o_ref.dtype)

        _for_row_blocks(nb, block)


def _moe_up(pe, pass_nb, pass_valid, n_pass, pos_flat, hp, wg, wu, bg, bu, n_pass_max, tf):
    e, d, f = wg.shape
    nf = f // tf
    cap = PASS_ROWS

    def w_map(p, j, pe_, nb_, valid_, np_, pos_):
        return (pe_[p], 0, jnp.where(p < np_[0], j, nf - 1))

    def o_map(p, j, pe_, nb_, valid_, np_, pos_):
        ok = p < np_[0]
        return (jnp.where(ok, p, np_[0] - 1), jnp.where(ok, j, nf - 1))

    grid_spec = pltpu.PrefetchScalarGridSpec(
        num_scalar_prefetch=5,
        grid=(n_pass_max, nf),
        in_specs=[pl.BlockSpec(memory_space=pl.ANY),
                  pl.BlockSpec((None, d, tf), w_map),
                  pl.BlockSpec((None, d, tf), w_map),
                  pl.BlockSpec((None, 1, tf), w_map),
                  pl.BlockSpec((None, 1, tf), w_map)],
        out_specs=pl.BlockSpec((cap, tf), o_map),
        scratch_shapes=[pltpu.VMEM((cap, d // 2), U32),
                        pltpu.VMEM((cap, d), BF16),
                        pltpu.SMEM((n_pass_max * cap,), jnp.int32),
                        pltpu.SemaphoreType.DMA(())],
    )
    return pl.pallas_call(
        _moe_up_kernel,
        grid_spec=grid_spec,
        out_shape=jax.ShapeDtypeStruct((n_pass_max * cap, f), BF16),
        compiler_params=_params("arbitrary", "arbitrary"),
        name="moe_up",
    )(pe, pass_nb, pass_valid, n_pass, pos_flat, hp, wg, wu, bg.reshape(e, 1, f),
      bu.reshape(e, 1, f))


def _moe_down_kernel(pe_ref, nb_ref, npass_ref, a_ref, wh_ref, wl_ref, bh_ref, bl_ref, o_ref):
    p = pl.program_id(0)

    @pl.when(p < npass_ref[0])
    def _():
        def block(row0, m):
            rows = pl.ds(row0, m)
            a = a_ref[rows, :]
            hi = jnp.dot(a, wh_ref[...].astype(BF16), preferred_element_type=F32) + bh_ref[...]
            lo = jnp.dot(a, wl_ref[...].astype(BF16), preferred_element_type=F32) + bl_ref[...]
            o_ref[rows, :] = _pack_bf16_pair(hi, lo)

        _for_row_blocks(nb_ref[p], block)


def _moe_down(pe, pass_nb, n_pass, act, wd, bd, n_pass_max, td):
    e, f, d = wd.shape
    nd = d // 2 // td
    cap = PASS_ROWS

    def col(p, j, np_):
        return jnp.where(p < np_[0], j, nd - 1)

    def row(p, np_):
        return jnp.where(p < np_[0], p, np_[0] - 1)

    grid_spec = pltpu.PrefetchScalarGridSpec(
        num_scalar_prefetch=3,
        grid=(n_pass_max, nd),
        in_specs=[pl.BlockSpec((cap, f), lambda p, j, pe_, nb_, np_: (row(p, np_), 0)),
                  pl.BlockSpec((None, f, td), lambda p, j, pe_, nb_, np_: (pe_[p], 0, col(p, j, np_))),
                  pl.BlockSpec((None, f, td), lambda p, j, pe_, nb_, np_: (pe_[p], 0, nd + col(p, j, np_))),
                  pl.BlockSpec((None, 1, td), lambda p, j, pe_, nb_, np_: (pe_[p], 0, col(p, j, np_))),
                  pl.BlockSpec((None, 1, td), lambda p, j, pe_, nb_, np_: (pe_[p], 0, nd + col(p, j, np_)))],
        out_specs=pl.BlockSpec((cap, td), lambda p, j, pe_, nb_, np_: (row(p, np_), col(p, j, np_))),
    )
    bd3 = bd.reshape(e, 1, d)
    return pl.pallas_call(
        _moe_down_kernel,
        grid_spec=grid_spec,
        out_shape=jax.ShapeDtypeStruct((n_pass_max * cap, d // 2), U32),
        compiler_params=_params("arbitrary", "arbitrary"),
        name="moe_down",
    )(pe, pass_nb, n_pass, act, wd, wd, bd3, bd3)


def _combine_kernel(pos_ref, y_hbm, x_ref, gate_ref, g_ref, o_ref, ybuf_ref, sems):
    i = pl.program_id(0)
    n = pl.num_programs(0)
    tt, d = x_ref.shape
    half = d // 2
    slot = i % 2

    n_tok = n * tt

    def start_tile(step, slot_):
        for k in range(TOP_K):
            _gather_rows(y_hbm, pos_ref, k * n_tok + step * tt, tt, ybuf_ref.at[slot_, k],
                         sems.at[slot_])

    @pl.when(i == 0)
    def _():
        start_tile(0, 0)

    def wait_tile(slot_):
        for k in range(TOP_K):
            pltpu.make_async_copy(y_hbm.at[pl.ds(0, tt)], ybuf_ref.at[slot_, k],
                                  sems.at[slot_]).wait()

    wait_tile(slot)
    nxt = jnp.minimum(i + 1, n - 1)
    for k in range(TOP_K):
        for r0 in range(0, tt, GATHER_UNROLL):
            dst_tile = ybuf_ref.at[1 - slot, k].at[pl.ds(r0, GATHER_UNROLL)]
            for u in range(GATHER_UNROLL):
                pltpu.make_async_copy(
                    y_hbm.at[pl.ds(pos_ref[k * n_tok + nxt * tt + r0 + u], 1)],
                    dst_tile.at[pl.ds(u, 1)], sems.at[1 - slot]).start()

    gates = gate_ref[...]
    x = x_ref[...]
    acc_hi = x[:, :half]
    acc_lo = x[:, half:]
    for k in range(TOP_K):
        hi, lo = _unpack_bf16_pair(ybuf_ref[slot, k])
        acc_hi = acc_hi + gates[:, k:k + 1] * hi
        acc_lo = acc_lo + gates[:, k:k + 1] * lo
    ms = (jnp.sum(acc_hi * acc_hi, axis=-1, keepdims=True)
          + jnp.sum(acc_lo * acc_lo, axis=-1, keepdims=True)) * (1.0 / d)
    scale = lax.rsqrt(ms + RMS_EPS)
    o_ref[:, :half] = acc_hi * scale * g_ref[:, :half]
    o_ref[:, half:] = acc_lo * scale * g_ref[:, half:]

    @pl.when(i == n - 1)
    def _():
        wait_tile(1 - slot)


def _combine(pos_flat, y, x, gates, g, tt):
    t, d = x.shape
    grid_spec = pltpu.PrefetchScalarGridSpec(
        num_scalar_prefetch=1,
        grid=(t // tt,),
        in_specs=[pl.BlockSpec(memory_space=pl.ANY),
                  pl.BlockSpec((tt, d), lambda i, p: (i, 0)),
                  pl.BlockSpec((tt, TOP_K), lambda i, p: (i, 0)),
                  pl.BlockSpec((1, d), lambda i, p: (0, 0))],
        out_specs=pl.BlockSpec((tt, d), lambda i, p: (i, 0)),
        scratch_shapes=[pltpu.VMEM((2, TOP_K, tt, d // 2), U32),
                        pltpu.SemaphoreType.DMA((2,))],
    )
    return pl.pallas_call(
        _combine_kernel,
        grid_spec=grid_spec,
        out_shape=jax.ShapeDtypeStruct((t, d), F32),
        compiler_params=_params("arbitrary"),
        name="combine",
    )(pos_flat, y, x, gates, g.reshape(1, d))


class _Tiles(NamedTuple):
    norm_rows: int = 512
    inproj: tuple = (512, 1024)
    outproj: tuple = (1024, 512)
    router_rows: int = 512
    up_cols: int = 256
    down_cols: int = 512
    combine_rows: int = 256


TILES = _Tiles()


def _moe_and_final_norm(x_mid, norm2_g, w_router, b_router, w_gate, b_gate, w_up, b_up,
                        w_down, b_down, final_norm_g):
    t = x_mid.shape[0]
    n_pass_max = _num_passes(t * TOP_K)
    hp, idx, gates, rank, counts = _router(x_mid, norm2_g, w_router, b_router,
                                           tt=TILES.router_rows)
    pe, pass_nb, pass_valid, n_pass, pos_flat = _routing_tables(idx, rank, counts[0], n_pass_max)
    act = _moe_up(pe, pass_nb, pass_valid, n_pass, pos_flat, hp, w_gate, w_up, b_gate, b_up,
                  n_pass_max, tf=TILES.up_cols)
    y_rows = _moe_down(pe, pass_nb, n_pass, act, w_down, b_down, n_pass_max, td=TILES.down_cols)
    return _combine(pos_flat, y_rows, x_mid, gates, final_norm_g, tt=TILES.combine_rows)


def kernel(x, meta_tokens, norm1_g, w_in, lru_conv_w, lru_conv_b, lru_w_a, lru_b_a, lru_w_i, lru_b_i, lru_lambda, conf_conv_w, conf_conv_b, conf_norm_g, conf_norm_b, w_out, norm2_g, w_router, b_router, w_gate, b_gate, w_up, b_up, w_down, b_down, final_norm_g):
    b, s, d = x.shape
    t = b * s
    assert norm1_g.shape[0] == 1, "one layer"
    assert d == LRU_WIDTH + CONF_WIDTH == LRU_HEADS * HEAD_DIM + CONF_GROUPS * HEAD_DIM
    assert s % (LRU_CHUNKS * SUBLANES) == 0 and s % CONF_CHUNK == 0
    assert all(t % rows == 0 for rows in (TILES.norm_rows, TILES.inproj[0], TILES.outproj[0],
                                          TILES.router_rows, TILES.combine_rows))
    x2 = x.reshape(t, d)

    h = _rmsnorm(x2, norm1_g[0], tm=TILES.norm_rows)
    h_meta = _rmsnorm(meta_tokens.astype(x.dtype), norm1_g[0], tm=N_META)
    z, z_meta = _inproj(h, h_meta, w_in[0], *TILES.inproj)
    z3 = z.reshape(b, s, -1)

    y_lru = _lru_mixer(z3, z_meta, lru_conv_w[0], lru_conv_b[0], lru_w_a[0], lru_b_a[0],
                       lru_w_i[0], lru_b_i[0], lru_lambda[0])
    y_conf = _conf_mixer(z3, z_meta, conf_conv_w[0], conf_conv_b[0], conf_norm_g[0],
                         conf_norm_b[0])
    x_mid = _outproj(y_lru.reshape(t, -1), y_conf.reshape(t, -1), w_out[0], x2, *TILES.outproj)

    out = _moe_and_final_norm(x_mid, norm2_g[0], w_router[0], b_router[0], w_gate[0], b_gate[0],
                              w_up[0], b_up[0], w_down[0], b_down[0], final_norm_g)
    return out.reshape(b, s, d)
```

```python
from typing import NamedTuple

import jax
import jax.numpy as jnp
from jax import lax
from jax.experimental import pallas as pl
from jax.experimental.pallas import tpu as pltpu

N_META = 16
LRU_WIDTH = 2048
LRU_HEADS = 16
HEAD_DIM = 128
LRU_CONV_WIDTH = 4
LRU_C = 8.0
CONF_WIDTH = 2048
CONF_GROUPS = 16
CONF_KERNEL = 31
N_EXPERTS = 32
TOP_K = 4
SWIGLU_ALPHA = 1.702
SWIGLU_LIMIT = 7.0
RMS_EPS = 1e-5
LN_EPS = 1e-5
SQRT_FLOOR = 1e-30

SUBLANES = 8
V7X_VMEM_BYTES = 64 * 1024 * 1024
VMEM_LIMIT = V7X_VMEM_BYTES - 6 * 1024 * 1024

ROW_BLOCK = 128
MATMUL_ROWS = 1536
PASS_ROWS = 1536
BF16 = jnp.bfloat16
F32 = jnp.float32
U32 = jnp.uint32


def _params(*sem):
    return pltpu.CompilerParams(dimension_semantics=sem, vmem_limit_bytes=VMEM_LIMIT)


def _pack_bf16_pair(hi, lo):
    hi_bits = lax.bitcast_convert_type(hi.astype(BF16).astype(F32), U32)
    lo_bits = lax.bitcast_convert_type(lo.astype(BF16).astype(F32), U32)
    return hi_bits | lax.shift_right_logical(lo_bits, jnp.uint32(16))


def _sigmoid(x):
    return 0.5 * jnp.tanh(0.5 * x) + 0.5


def _unpack_bf16_pair(u):
    hi = lax.bitcast_convert_type(u & jnp.uint32(0xFFFF0000), F32)
    lo = lax.bitcast_convert_type(lax.shift_left(u, jnp.uint32(16)), F32)
    return hi, lo


def _rmsnorm_kernel(x_ref, g_ref, o_ref):
    x = x_ref[...]
    ms = jnp.mean(x * x, axis=-1, keepdims=True)
    o_ref[...] = (x * lax.rsqrt(ms + RMS_EPS) * g_ref[...]).astype(o_ref.dtype)


def _rmsnorm(x, g, tm):
    t, d = x.shape
    return pl.pallas_call(
        _rmsnorm_kernel,
        grid=(t // tm,),
        in_specs=[pl.BlockSpec((tm, d), lambda i: (i, 0)),
                  pl.BlockSpec((1, d), lambda i: (0, 0))],
        out_specs=pl.BlockSpec((tm, d), lambda i: (i, 0)),
        out_shape=jax.ShapeDtypeStruct((t, d), BF16),
        compiler_params=_params("parallel"),
        name="rmsnorm1",
    )(x, g.reshape(1, d))


def _inproj_kernel(a_ref, am_ref, w_ref, o_ref, om_ref, wb_ref):
    @pl.when(pl.program_id(1) == 0)
    def _():
        wb_ref[...] = w_ref[...].astype(BF16)
        om_ref[...] = jnp.dot(am_ref[...], wb_ref[...], preferred_element_type=F32)

    o_ref[...] = jnp.dot(a_ref[...], wb_ref[...], preferred_element_type=F32)


def _inproj(a, a_meta, w, tm, tn):
    t, k = a.shape
    n = w.shape[1]
    return pl.pallas_call(
        _inproj_kernel,
        grid=(n // tn, t // tm),
        in_specs=[pl.BlockSpec((tm, k), lambda j, i: (i, 0)),
                  pl.BlockSpec((N_META, k), lambda j, i: (0, 0)),
                  pl.BlockSpec((k, tn), lambda j, i: (0, j))],
        out_specs=[pl.BlockSpec((tm, tn), lambda j, i: (i, j)),
                   pl.BlockSpec((N_META, tn), lambda j, i: (0, j))],
        out_shape=[jax.ShapeDtypeStruct((t, n), F32),
                   jax.ShapeDtypeStruct((N_META, n), F32)],
        scratch_shapes=[pltpu.VMEM((k, tn), BF16)],
        compiler_params=_params("parallel", "arbitrary"),
        name="inproj",
    )(a, a_meta, w)


LRU_CHUNKS = SUBLANES
LRU_CHUNK_PAD = SUBLANES
LRU_SCAN_UNROLL = 8


def _gelu_tanh(x):
    return 0.5 * x * (1.0 + jnp.tanh(0.7978845608028654 * (x + 0.044715 * x * x * x)))


def _lru_kernel(xr_ref, gate_ref, xm_ref, cw_ref, cb_ref, wa_ref, ba_ref, wi_ref,
                bi_ref, lam_ref, o_ref, sf_ref, af_ref, bf_ref, ab_ref, bb_ref, hf_ref, hb_ref):
    s = xr_ref.shape[0]
    st = s + N_META
    pad = SUBLANES
    clen = s // LRU_CHUNKS
    pitch = clen + LRU_CHUNK_PAD
    zeros8 = jnp.zeros((pad, HEAD_DIM), F32)
    sf_ref[0:pad, :] = zeros8
    sf_ref[pad:pad + N_META, :] = xm_ref[...]
    sf_ref[pad + N_META:pad + st, :] = xr_ref[...]
    sf_ref[pad + st:pad + st + pad, :] = zeros8

    def gates(u, d):
        ub = u.astype(BF16)
        t_r = jnp.tanh(jnp.dot(ub, (0.5 * wa_ref[d]).astype(BF16), preferred_element_type=F32)
                       + 0.5 * ba_ref[d:d + 1, :])
        t_i = jnp.tanh(jnp.dot(ub, (0.5 * wi_ref[d]).astype(BF16), preferred_element_type=F32)
                       + 0.5 * bi_ref[d:d + 1, :])
        lam = lam_ref[d:d + 1, :]
        softplus_neg = jnp.maximum(-lam, 0.0) + jnp.log1p(jnp.exp(-jnp.abs(lam)))
        half_c = (-0.5 * LRU_C) * softplus_neg
        log_a = half_c * t_r + half_c
        a = jnp.exp(log_a)
        one_minus_a2 = -jnp.tanh(log_a) * (a * a + 1.0)
        root = one_minus_a2 * lax.rsqrt(jnp.maximum(one_minus_a2, SQRT_FLOOR))
        half_u = 0.5 * u
        b = root * (half_u * t_i + half_u)
        return a, b

    uf = cb_ref[0:1, :] + cw_ref[0, 3:4, :] * sf_ref[pl.ds(pad, st), :]
    for j in range(1, LRU_CONV_WIDTH):
        uf = uf + cw_ref[0, 3 - j:4 - j, :] * sf_ref[pl.ds(pad - j, st), :]
    a, b = gates(uf, 0)
    h_meta = b[0:1, :]
    for r in range(1, N_META):
        h_meta = a[r:r + 1, :] * h_meta + b[r:r + 1, :]
    for c in range(LRU_CHUNKS):
        af_ref[c * pitch:c * pitch + clen, :] = a[N_META + c * clen:N_META + (c + 1) * clen, :]
        bf_ref[c * pitch:c * pitch + clen, :] = b[N_META + c * clen:N_META + (c + 1) * clen, :]

    base = pad + N_META
    ub_ = cb_ref[1:2, :] + cw_ref[1, 3:4, :] * sf_ref[pl.ds(base, s), :]
    for j in range(1, LRU_CONV_WIDTH):
        ub_ = ub_ + cw_ref[1, 3 - j:4 - j, :] * sf_ref[pl.ds(base + j, s), :]
    a, b = gates(ub_, 1)
    for c in range(LRU_CHUNKS):
        ab_ref[c * pitch:c * pitch + clen, :] = a[c * clen:(c + 1) * clen, :]
        bb_ref[c * pitch:c * pitch + clen, :] = b[c * clen:(c + 1) * clen, :]

    def step_rows(t):
        return (pl.ds(t, LRU_CHUNKS, stride=pitch),
                pl.ds(clen - 1 - t, LRU_CHUNKS, stride=pitch))

    def local_scan(t, carry):
        hf, pf, hb, pb = carry
        rf, rb = step_rows(t)
        a_f = af_ref[rf, :]
        a_b = ab_ref[rb, :]
        return (a_f * hf + bf_ref[rf, :], a_f * pf, a_b * hb + bb_ref[rb, :], a_b * pb)

    zero = sf_ref[0:SUBLANES, :]
    one = zero + 1.0
    hf, pf, hb, pb = lax.fori_loop(0, clen, local_scan, (zero, one, zero, one),
                                   unroll=LRU_SCAN_UNROLL)

    row = lax.broadcasted_iota(jnp.int32, (LRU_CHUNKS, HEAD_DIM), 0)
    c = h_meta
    start_f = jnp.where(row == 0, c, zero)
    for k in range(1, LRU_CHUNKS):
        c = hf[k - 1:k, :] + pf[k - 1:k, :] * c
        start_f = jnp.where(row == k, c, start_f)
    c = zero[0:1, :]
    start_b = zero
    for k in range(LRU_CHUNKS - 2, -1, -1):
        c = hb[k + 1:k + 2, :] + pb[k + 1:k + 2, :] * c
        start_b = jnp.where(row == k, c, start_b)

    def final_scan(t, carry):
        hf, hb = carry
        rf, rb = step_rows(t)
        hf = af_ref[rf, :] * hf + bf_ref[rf, :]
        hb = ab_ref[rb, :] * hb + bb_ref[rb, :]
        hf_ref[rf, :] = hf
        hb_ref[rb, :] = hb
        return hf, hb

    lax.fori_loop(0, clen, final_scan, (start_f, start_b), unroll=LRU_SCAN_UNROLL)
    for c in range(LRU_CHUNKS):
        hsum = hf_ref[c * pitch:c * pitch + clen, :] + hb_ref[c * pitch:c * pitch + clen, :]
        y = hsum * _gelu_tanh(gate_ref[c * clen:(c + 1) * clen, :])
        o_ref[c * clen:(c + 1) * clen, :] = y.astype(o_ref.dtype)


def _lru_mixer(z3, z_meta, cw, cb, wa, ba, wi, bi, lam):
    b, s, _ = z3.shape
    st = s + N_META
    hd = HEAD_DIM
    col = lambda off: (lambda bi_, h: (bi_, 0, off + h))
    return pl.pallas_call(
        _lru_kernel,
        grid=(b, LRU_HEADS),
        in_specs=[
            pl.BlockSpec((None, s, hd), col(0)),
            pl.BlockSpec((None, s, hd), col(LRU_HEADS)),
            pl.BlockSpec((N_META, hd), lambda bi_, h: (0, h)),
            pl.BlockSpec((2, LRU_CONV_WIDTH, hd), lambda bi_, h: (0, 0, h)),
            pl.BlockSpec((2, hd), lambda bi_, h: (0, h)),
            pl.BlockSpec((2, None, hd, hd), lambda bi_, h: (0, h, 0, 0)),
            pl.BlockSpec((2, hd), lambda bi_, h: (0, h)),
            pl.BlockSpec((2, None, hd, hd), lambda bi_, h: (0, h, 0, 0)),
            pl.BlockSpec((2, hd), lambda bi_, h: (0, h)),
            pl.BlockSpec((2, hd), lambda bi_, h: (0, h)),
        ],
        out_specs=pl.BlockSpec((None, s, hd), lambda bi_, h: (bi_, 0, h)),
        out_shape=jax.ShapeDtypeStruct((b, s, LRU_WIDTH), BF16),
        scratch_shapes=[
            pltpu.VMEM((st + 2 * SUBLANES, hd), F32),
        ] + [pltpu.VMEM((LRU_CHUNKS * (s // LRU_CHUNKS + LRU_CHUNK_PAD), hd), F32)] * 6,
        compiler_params=_params("parallel", "parallel"),
        name="lru_mixer",
    )(z3, z3, z_meta, cw, cb, wa, ba.reshape(2, LRU_WIDTH), wi, bi.reshape(2, LRU_WIDTH), lam)


CONF_CHUNK = 64
CONF_PARTIAL_SUMS = 2


def _conf_kernel(a_ref, b_ref, am_ref, bm_ref, cw_ref, cb_ref, g_ref, be_ref, o_ref,
                 cs_ref, sh_ref):
    s = a_ref.shape[0]
    st = s + N_META
    n_sh = sh_ref.shape[1]
    cs_ref[0:N_META, :] = am_ref[...] * _sigmoid(bm_ref[...])
    cs_ref[N_META:st, :] = a_ref[...] * _sigmoid(b_ref[...])
    cs_ref[st:st + N_META, :] = jnp.zeros((N_META, HEAD_DIM), F32)
    for r in range(1, SUBLANES):
        sh_ref[r - 1] = cs_ref[pl.ds(r, n_sh), :]

    n_chunks = s // CONF_CHUNK

    def conv_chunk(row0):
        parts = [None] * CONF_PARTIAL_SUMS
        for k in range(CONF_KERNEL):
            off = N_META - CONF_KERNEL // 2 + k
            r, q = off % SUBLANES, off // SUBLANES
            rows = pl.ds(row0 + q * SUBLANES, CONF_CHUNK)
            term = cw_ref[k:k + 1, :] * (cs_ref[rows, :] if r == 0 else sh_ref[r - 1, rows, :])
            p = k % CONF_PARTIAL_SUMS
            parts[p] = term if parts[p] is None else parts[p] + term
        return sum(parts[1:], parts[0]) + cb_ref[...]

    def step(j, carry):
        acc_prev, xc_prev = carry
        mean = jnp.mean(acc_prev, axis=-1, keepdims=True)
        var = jnp.mean(xc_prev * xc_prev, axis=-1, keepdims=True)
        acc = conv_chunk(pl.multiple_of(jnp.minimum(j, n_chunks - 1) * CONF_CHUNK, CONF_CHUNK))
        y = xc_prev * lax.rsqrt(var + LN_EPS) * g_ref[...] + be_ref[...]
        out_row = pl.multiple_of(jnp.maximum(j - 2, 0) * CONF_CHUNK, CONF_CHUNK)
        half_y = 0.5 * y
        o_ref[pl.ds(out_row, CONF_CHUNK), :] = (half_y * jnp.tanh(half_y) + half_y).astype(o_ref.dtype)
        return acc, acc_prev - mean

    warmup = cs_ref[pl.ds(0, CONF_CHUNK), :]
    lax.fori_loop(0, n_chunks + 2, step, (warmup, warmup))


def _conf_mixer(z3, z_meta, cw, cb, g, be):
    b, s, _ = z3.shape
    st = s + N_META
    hd = HEAD_DIM
    a_off = 2 * LRU_HEADS
    b_off = 2 * LRU_HEADS + CONF_GROUPS
    n_sh = st + N_META - SUBLANES
    vec = pl.BlockSpec((1, hd), lambda bi_, h: (0, h))
    return pl.pallas_call(
        _conf_kernel,
        grid=(b, CONF_GROUPS),
        in_specs=[
            pl.BlockSpec((None, s, hd), lambda bi_, h: (bi_, 0, a_off + h)),
            pl.BlockSpec((None, s, hd), lambda bi_, h: (bi_, 0, b_off + h)),
            pl.BlockSpec((N_META, hd), lambda bi_, h: (0, a_off + h)),
            pl.BlockSpec((N_META, hd), lambda bi_, h: (0, b_off + h)),
            pl.BlockSpec((CONF_KERNEL, hd), lambda bi_, h: (0, h)),
            vec, vec, vec,
        ],
        out_specs=pl.BlockSpec((None, s, hd), lambda bi_, h: (bi_, 0, h)),
        out_shape=jax.ShapeDtypeStruct((b, s, CONF_WIDTH), BF16),
        scratch_shapes=[
            pltpu.VMEM((st + N_META, hd), F32),
            pltpu.VMEM((SUBLANES - 1, n_sh, hd), F32),
        ],
        compiler_params=_params("parallel", "parallel"),
        name="conf_mixer",
    )(z3, z3, z_meta, z_meta, cw, cb.reshape(1, -1), g.reshape(1, -1), be.reshape(1, -1))


def _outproj_kernel(ya_ref, yb_ref, wa_ref, wb_ref, x_ref, o_ref):
    acc = jnp.dot(ya_ref[...], wa_ref[...].astype(BF16), preferred_element_type=F32)
    acc = acc + jnp.dot(yb_ref[...], wb_ref[...].astype(BF16), preferred_element_type=F32)
    o_ref[...] = x_ref[...] + acc


def _outproj(ya, yb, w, x, tm, tn):
    t, k = ya.shape
    n = w.shape[1]
    return pl.pallas_call(
        _outproj_kernel,
        grid=(t // tm, n // tn),
        in_specs=[pl.BlockSpec((tm, k), lambda i, j: (i, 0)),
                  pl.BlockSpec((tm, k), lambda i, j: (i, 0)),
                  pl.BlockSpec((k, tn), lambda i, j: (0, j)),
                  pl.BlockSpec((k, tn), lambda i, j: (1, j)),
                  pl.BlockSpec((tm, tn), lambda i, j: (i, j))],
        out_specs=pl.BlockSpec((tm, tn), lambda i, j: (i, j)),
        out_shape=jax.ShapeDtypeStruct((t, n), F32),
        compiler_params=_params("parallel", "parallel"),
        name="outproj",
    )(ya, yb, w, w, x)


def _router_kernel(x_ref, g_ref, wr_ref, br_ref, hp_ref, idx_ref, gate_ref, rank_ref,
                   cnt_ref, carry_ref):
    tt, d = x_ref.shape
    half = d // 2

    @pl.when(pl.program_id(0) == 0)
    def _():
        carry_ref[...] = jnp.zeros_like(carry_ref)

    x = x_ref[...]
    ms = jnp.mean(x * x, axis=-1, keepdims=True)
    h = x * lax.rsqrt(ms + RMS_EPS) * g_ref[...]
    hp_ref[...] = _pack_bf16_pair(h[:, :half], h[:, half:])

    h_hi = h.astype(BF16)
    h_lo = (h - h_hi.astype(F32)).astype(BF16)
    w = wr_ref[...]
    w_hi = w.astype(BF16)
    w_lo = (w - w_hi.astype(F32)).astype(BF16)
    hi_terms = jnp.dot(h_hi, jnp.concatenate([w_hi, w_lo], axis=1), preferred_element_type=F32)
    logits = (hi_terms[:, :N_EXPERTS] + hi_terms[:, N_EXPERTS:]
              + jnp.dot(h_lo, w_hi, preferred_element_type=F32)) + br_ref[...]

    lane = lax.broadcasted_iota(jnp.int32, (tt, N_EXPERTS), 1)
    lane_k = lax.broadcasted_iota(jnp.int32, (tt, TOP_K), 1)
    work = logits
    vals, sels = [], []
    idx_out = jnp.zeros((tt, TOP_K), jnp.int32)
    for k in range(TOP_K):
        m = jnp.max(work, axis=1, keepdims=True)
        am = jnp.min(jnp.where(work == m, lane, N_EXPERTS), axis=1, keepdims=True)
        sel = lane == am
        vals.append(m)
        sels.append(sel)
        idx_out = jnp.where(lane_k == k, am, idx_out)
        work = jnp.where(sel, -jnp.inf, work)
    idx_ref[...] = idx_out

    exps = [jnp.exp(v - vals[0]) for v in vals]
    denom = exps[0] + exps[1] + exps[2] + exps[3]
    gate_out = jnp.zeros((tt, TOP_K), F32)
    for k in range(TOP_K):
        gate_out = jnp.where(lane_k == k, exps[k] / denom, gate_out)
    gate_ref[...] = gate_out

    onehot = jnp.zeros((tt, N_EXPERTS), F32)
    for sel in sels:
        onehot = onehot + sel.astype(F32)
    r_i = lax.broadcasted_iota(jnp.int32, (tt, tt), 0)
    c_i = lax.broadcasted_iota(jnp.int32, (tt, tt), 1)
    tri = (c_i < r_i).astype(BF16)
    before = jnp.dot(tri, onehot.astype(BF16), preferred_element_type=F32) + carry_ref[...]
    rank_out = jnp.zeros((tt, TOP_K), jnp.int32)
    for k, sel in enumerate(sels):
        rk = jnp.sum(jnp.where(sel, before, 0.0), axis=1, keepdims=True).astype(jnp.int32)
        rank_out = jnp.where(lane_k == k, rk, rank_out)
    rank_ref[...] = rank_out
    carry_ref[...] = carry_ref[...] + jnp.sum(onehot, axis=0, keepdims=True)
    cnt_ref[...] = carry_ref[...].astype(jnp.int32)


def _router(x, g, wr, br, tt):
    t, d = x.shape
    small = lambda dt: jax.ShapeDtypeStruct((t, TOP_K), dt)
    return pl.pallas_call(
        _router_kernel,
        grid=(t // tt,),
        in_specs=[pl.BlockSpec((tt, d), lambda i: (i, 0)),
                  pl.BlockSpec((1, d), lambda i: (0, 0)),
                  pl.BlockSpec((d, N_EXPERTS), lambda i: (0, 0)),
                  pl.BlockSpec((1, N_EXPERTS), lambda i: (0, 0))],
        out_specs=[pl.BlockSpec((tt, d // 2), lambda i: (i, 0)),
                   pl.BlockSpec((tt, TOP_K), lambda i: (i, 0)),
                   pl.BlockSpec((tt, TOP_K), lambda i: (i, 0)),
                   pl.BlockSpec((tt, TOP_K), lambda i: (i, 0)),
                   pl.BlockSpec((1, N_EXPERTS), lambda i: (0, 0))],
        out_shape=[jax.ShapeDtypeStruct((t, d // 2), U32),
                   small(jnp.int32), small(F32), small(jnp.int32),
                   jax.ShapeDtypeStruct((1, N_EXPERTS), jnp.int32)],
        scratch_shapes=[pltpu.VMEM((1, N_EXPERTS), F32)],
        compiler_params=_params("arbitrary"),
        name="router",
    )(x, g.reshape(1, d), wr, br.reshape(1, N_EXPERTS))


def _num_passes(n_assign):
    return N_EXPERTS + n_assign // PASS_ROWS


def _per_expert(table, experts):
    onehot = experts[..., None] == jnp.arange(N_EXPERTS, dtype=experts.dtype)
    return jnp.sum(jnp.where(onehot, table, 0), axis=-1)


def _routing_tables(idx, rank, counts, n_pass_max):
    padded = (counts + ROW_BLOCK - 1) // ROW_BLOCK * ROW_BLOCK
    passes_e = (padded + PASS_ROWS - 1) // PASS_ROWS
    pass_end = jnp.cumsum(passes_e)
    pass_start = pass_end - passes_e
    n_pass = pass_end[-1]
    pos = (_per_expert(pass_start, idx) + rank // PASS_ROWS) * PASS_ROWS + rank % PASS_ROWS
    p_ids = jnp.arange(n_pass_max, dtype=jnp.int32)
    live = p_ids < n_pass
    pe = jnp.sum(jnp.minimum(p_ids, n_pass - 1)[:, None] >= pass_end[None, :], axis=1)
    pe = jnp.minimum(pe, N_EXPERTS - 1).astype(jnp.int32)
    done = (p_ids - _per_expert(pass_start, pe)) * PASS_ROWS
    rows = jnp.clip(_per_expert(padded, pe) - done, 0, PASS_ROWS)
    pass_nb = jnp.where(live, rows // ROW_BLOCK, 0).astype(jnp.int32)
    pass_valid = jnp.where(live, jnp.clip(_per_expert(counts, pe) - done, 0, PASS_ROWS),
                           0).astype(jnp.int32)
    return (pe, pass_nb, pass_valid, n_pass.reshape(1).astype(jnp.int32),
            pos.T.reshape(-1).astype(jnp.int32))


GATHER_UNROLL = SUBLANES
GATHER_WINDOWS = 4


def _gather_rows(src_hbm, idx_ref, idx_base, n_rows, dst_ref, sem):
    per_trip = GATHER_UNROLL * GATHER_WINDOWS

    def issue(g, _):
        for w in range(GATHER_WINDOWS):
            r0 = pl.multiple_of(g * per_trip + w * GATHER_UNROLL, GATHER_UNROLL)
            dst_tile = dst_ref.at[pl.ds(r0, GATHER_UNROLL)]
            for u in range(GATHER_UNROLL):
                pltpu.make_async_copy(src_hbm.at[pl.ds(idx_ref[idx_base + r0 + u], 1)],
                                      dst_tile.at[pl.ds(u, 1)], sem).start()
        return 0

    lax.fori_loop(0, n_rows // per_trip, issue, 0)


def _for_row_blocks(n_blocks, body):
    per_big = MATMUL_ROWS // ROW_BLOCK
    n_big = n_blocks // per_big

    def big(i, _):
        body(pl.multiple_of(i * MATMUL_ROWS, MATMUL_ROWS), MATMUL_ROWS)
        return 0

    lax.fori_loop(0, n_big, big, 0)
    rest = n_blocks - n_big * per_big
    start = pl.multiple_of(n_big * MATMUL_ROWS, MATMUL_ROWS)
    for m in range(1, per_big):
        @pl.when(rest == m)
        def _(m=m):
            body(start, m * ROW_BLOCK)


def _wait_row_blocks(src_hbm, dst_ref, n_blocks, rows, sem):
    def drain(i, _):
        pltpu.make_async_copy(src_hbm.at[pl.ds(0, rows)], dst_ref.at[pl.ds(0, rows)], sem).wait()
        return 0

    lax.fori_loop(0, n_blocks, drain, 0)


def _build_token_table(pos_ref, nb_ref, valid_ref, n_pass, cap, tok_ref):
    def pad_pass(q, _):
        def pad_row(r, _):
            tok_ref[q * cap + r] = 0
            return 0

        lax.fori_loop(valid_ref[q], nb_ref[q] * ROW_BLOCK, pad_row, 0)
        return 0

    lax.fori_loop(0, n_pass, pad_pass, 0)
    n_tok = pos_ref.shape[0] // TOP_K
    for k in range(TOP_K):
        def place(g, _, k=k):
            for u in range(GATHER_UNROLL):
                t = g * GATHER_UNROLL + u
                tok_ref[pos_ref[k * n_tok + t]] = t
            return 0

        lax.fori_loop(0, n_tok // GATHER_UNROLL, place, 0)


def _moe_up_kernel(pe_ref, nb_ref, valid_ref, npass_ref, pos_ref, hp_hbm, wg_ref, wu_ref, bg_ref,
                   bu_ref, o_ref, gbuf_ref, xb_ref, tok_ref, sem):
    p = pl.program_id(0)
    f = pl.program_id(1)
    cap, half = gbuf_ref.shape
    n_pass = npass_ref[0]
    live = p < n_pass
    nb = nb_ref[p]

    @pl.when(jnp.logical_and(live, f == 0))
    def _():
        @pl.when(p == 0)
        def _():
            _build_token_table(pos_ref, nb_ref, valid_ref, n_pass, cap, tok_ref)
            _gather_rows(hp_hbm, tok_ref, 0, nb * ROW_BLOCK, gbuf_ref, sem)

        _wait_row_blocks(hp_hbm, gbuf_ref, nb, ROW_BLOCK, sem)

        def unpack(rb, _):
            rows = pl.ds(pl.multiple_of(rb * ROW_BLOCK, ROW_BLOCK), ROW_BLOCK)
            hi, lo = _unpack_bf16_pair(gbuf_ref[rows, :])
            xb_ref[rows, :half] = hi.astype(BF16)
            xb_ref[rows, half:] = lo.astype(BF16)
            return 0

        lax.fori_loop(0, nb, unpack, 0)

        @pl.when(p + 1 < n_pass)
        def _():
            _gather_rows(hp_hbm, tok_ref, (p + 1) * cap, nb_ref[p + 1] * ROW_BLOCK, gbuf_ref, sem)

    @pl.when(live)
    def _():
        def block(row0, m):
            rows = pl.ds(row0, m)
            x = xb_ref[rows, :]
            hg = jnp.dot(x, wg_ref[...].astype(BF16), preferred_element_type=F32) + bg_ref[...]
            hu = jnp.dot(x, wu_ref[...].astype(BF16), preferred_element_type=F32) + bu_ref[...]
            hg = jnp.minimum(hg, SWIGLU_LIMIT)
            hu = jnp.clip(hu, -SWIGLU_LIMIT, SWIGLU_LIMIT)
            act = hg * _sigmoid(SWIGLU_ALPHA * hg) * (hu + 1.0)
            o_ref[rows, :] = act.astype(o_ref.dtype)

        _for_row_blocks(nb, block)


def _moe_up(pe, pass_nb, pass_valid, n_pass, pos_flat, hp, wg, wu, bg, bu, n_pass_max, tf):
    e, d, f = wg.shape
    nf = f // tf
    cap = PASS_ROWS
    assert ROW_BLOCK % (GATHER_UNROLL * GATHER_WINDOWS) == 0 and cap % ROW_BLOCK == 0

    def w_map(p, j, pe_, nb_, valid_, np_, pos_):
        return (pe_[p], 0, jnp.where(p < np_[0], j, nf - 1))

    def o_map(p, j, pe_, nb_, valid_, np_, pos_):
        ok = p < np_[0]
        return (jnp.where(ok, p, np_[0] - 1), jnp.where(ok, j, nf - 1))

    grid_spec = pltpu.PrefetchScalarGridSpec(
        num_scalar_prefetch=5,
        grid=(n_pass_max, nf),
        in_specs=[pl.BlockSpec(memory_space=pl.ANY),
                  pl.BlockSpec((None, d, tf), w_map),
                  pl.BlockSpec((None, d, tf), w_map),
                  pl.BlockSpec((None, 1, tf), w_map),
                  pl.BlockSpec((None, 1, tf), w_map)],
        out_specs=pl.BlockSpec((cap, tf), o_map),
        scratch_shapes=[pltpu.VMEM((cap, d // 2), U32),
                        pltpu.VMEM((cap, d), BF16),
                        pltpu.SMEM((n_pass_max * cap,), jnp.int32),
                        pltpu.SemaphoreType.DMA(())],
    )
    return pl.pallas_call(
        _moe_up_kernel,
        grid_spec=grid_spec,
        out_shape=jax.ShapeDtypeStruct((n_pass_max * cap, f), BF16),
        compiler_params=_params("arbitrary", "arbitrary"),
        name="moe_up",
    )(pe, pass_nb, pass_valid, n_pass, pos_flat, hp, wg, wu, bg.reshape(e, 1, f),
      bu.reshape(e, 1, f))


def _moe_down_kernel(pe_ref, nb_ref, npass_ref, a_ref, wh_ref, wl_ref, bh_ref, bl_ref, o_ref):
    p = pl.program_id(0)

    @pl.when(p < npass_ref[0])
    def _():
        def block(row0, m):
            rows = pl.ds(row0, m)
            a = a_ref[rows, :]
            hi = jnp.dot(a, wh_ref[...].astype(BF16), preferred_element_type=F32) + bh_ref[...]
            lo = jnp.dot(a, wl_ref[...].astype(BF16), preferred_element_type=F32) + bl_ref[...]
            o_ref[rows, :] = _pack_bf16_pair(hi, lo)

        _for_row_blocks(nb_ref[p], block)


def _moe_down(pe, pass_nb, n_pass, act, wd, bd, n_pass_max, td):
    e, f, d = wd.shape
    nd = d // 2 // td
    cap = PASS_ROWS

    def col(p, j, np_):
        return jnp.where(p < np_[0], j, nd - 1)

    def row(p, np_):
        return jnp.where(p < np_[0], p, np_[0] - 1)

    grid_spec = pltpu.PrefetchScalarGridSpec(
        num_scalar_prefetch=3,
        grid=(n_pass_max, nd),
        in_specs=[pl.BlockSpec((cap, f), lambda p, j, pe_, nb_, np_: (row(p, np_), 0)),
                  pl.BlockSpec((None, f, td), lambda p, j, pe_, nb_, np_: (pe_[p], 0, col(p, j, np_))),
                  pl.BlockSpec((None, f, td), lambda p, j, pe_, nb_, np_: (pe_[p], 0, nd + col(p, j, np_))),
                  pl.BlockSpec((None, 1, td), lambda p, j, pe_, nb_, np_: (pe_[p], 0, col(p, j, np_))),
                  pl.BlockSpec((None, 1, td), lambda p, j, pe_, nb_, np_: (pe_[p], 0, nd + col(p, j, np_)))],
        out_specs=pl.BlockSpec((cap, td), lambda p, j, pe_, nb_, np_: (row(p, np_), col(p, j, np_))),
    )
    bd3 = bd.reshape(e, 1, d)
    return pl.pallas_call(
        _moe_down_kernel,
        grid_spec=grid_spec,
        out_shape=jax.ShapeDtypeStruct((n_pass_max * cap, d // 2), U32),
        compiler_params=_params("arbitrary", "arbitrary"),
        name="moe_down",
    )(pe, pass_nb, n_pass, act, wd, wd, bd3, bd3)


def _combine_kernel(pos_ref, y_hbm, x_ref, gate_ref, g_ref, o_ref, ybuf_ref, sems):
    i = pl.program_id(0)
    n = pl.num_programs(0)
    tt, d = x_ref.shape
    half = d // 2
    slot = i % 2

    n_tok = n * tt

    def start_tile(step, slot_):
        for k in range(TOP_K):
            _gather_rows(y_hbm, pos_ref, k * n_tok + step * tt, tt, ybuf_ref.at[slot_, k],
                         sems.at[slot_])

    @pl.when(i == 0)
    def _():
        start_tile(0, 0)

    def wait_tile(slot_):
        for k in range(TOP_K):
            pltpu.make_async_copy(y_hbm.at[pl.ds(0, tt)], ybuf_ref.at[slot_, k],
                                  sems.at[slot_]).wait()

    wait_tile(slot)
    nxt = jnp.minimum(i + 1, n - 1)
    for k in range(TOP_K):
        for r0 in range(0, tt, GATHER_UNROLL):
            dst_tile = ybuf_ref.at[1 - slot, k].at[pl.ds(r0, GATHER_UNROLL)]
            for u in range(GATHER_UNROLL):
                pltpu.make_async_copy(
                    y_hbm.at[pl.ds(pos_ref[k * n_tok + nxt * tt + r0 + u], 1)],
                    dst_tile.at[pl.ds(u, 1)], sems.at[1 - slot]).start()

    gates = gate_ref[...]
    x = x_ref[...]
    acc_hi = x[:, :half]
    acc_lo = x[:, half:]
    for k in range(TOP_K):
        hi, lo = _unpack_bf16_pair(ybuf_ref[slot, k])
        acc_hi = acc_hi + gates[:, k:k + 1] * hi
        acc_lo = acc_lo + gates[:, k:k + 1] * lo
    ms = (jnp.sum(acc_hi * acc_hi, axis=-1, keepdims=True)
          + jnp.sum(acc_lo * acc_lo, axis=-1, keepdims=True)) * (1.0 / d)
    scale = lax.rsqrt(ms + RMS_EPS)
    o_ref[:, :half] = acc_hi * scale * g_ref[:, :half]
    o_ref[:, half:] = acc_lo * scale * g_ref[:, half:]

    @pl.when(i == n - 1)
    def _():
        wait_tile(1 - slot)


def _combine(pos_flat, y, x, gates, g, tt):
    t, d = x.shape
    assert tt % (GATHER_UNROLL * GATHER_WINDOWS) == 0
    grid_spec = pltpu.PrefetchScalarGridSpec(
        num_scalar_prefetch=1,
        grid=(t // tt,),
        in_specs=[pl.BlockSpec(memory_space=pl.ANY),
                  pl.BlockSpec((tt, d), lambda i, p: (i, 0)),
                  pl.BlockSpec((tt, TOP_K), lambda i, p: (i, 0)),
                  pl.BlockSpec((1, d), lambda i, p: (0, 0))],
        out_specs=pl.BlockSpec((tt, d), lambda i, p: (i, 0)),
        scratch_shapes=[pltpu.VMEM((2, TOP_K, tt, d // 2), U32),
                        pltpu.SemaphoreType.DMA((2,))],
    )
    return pl.pallas_call(
        _combine_kernel,
        grid_spec=grid_spec,
        out_shape=jax.ShapeDtypeStruct((t, d), F32),
        compiler_params=_params("arbitrary"),
        name="combine",
    )(pos_flat, y, x, gates, g.reshape(1, d))


class _Tiles(NamedTuple):
    norm_rows: int = 512
    inproj: tuple = (512, 1024)
    outproj: tuple = (1024, 512)
    router_rows: int = 512
    up_cols: int = 256
    down_cols: int = 512
    combine_rows: int = 256


TILES = _Tiles()


def _moe_and_final_norm(x_mid, norm2_g, w_router, b_router, w_gate, b_gate, w_up, b_up,
                        w_down, b_down, final_norm_g):
    t = x_mid.shape[0]
    n_pass_max = _num_passes(t * TOP_K)
    hp, idx, gates, rank, counts = _router(x_mid, norm2_g, w_router, b_router,
                                           tt=TILES.router_rows)
    pe, pass_nb, pass_valid, n_pass, pos_flat = _routing_tables(idx, rank, counts[0], n_pass_max)
    act = _moe_up(pe, pass_nb, pass_valid, n_pass, pos_flat, hp, w_gate, w_up, b_gate, b_up,
                  n_pass_max, tf=TILES.up_cols)
    y_rows = _moe_down(pe, pass_nb, n_pass, act, w_down, b_down, n_pass_max, td=TILES.down_cols)
    return _combine(pos_flat, y_rows, x_mid, gates, final_norm_g, tt=TILES.combine_rows)


def kernel(x, meta_tokens, norm1_g, w_in, lru_conv_w, lru_conv_b, lru_w_a, lru_b_a, lru_w_i, lru_b_i, lru_lambda, conf_conv_w, conf_conv_b, conf_norm_g, conf_norm_b, w_out, norm2_g, w_router, b_router, w_gate, b_gate, w_up, b_up, w_down, b_down, final_norm_g):
    b, s, d = x.shape
    t = b * s
    assert norm1_g.shape[0] == 1, "one layer"
    assert d == LRU_WIDTH + CONF_WIDTH == LRU_HEADS * HEAD_DIM + CONF_GROUPS * HEAD_DIM
    assert s % (LRU_CHUNKS * SUBLANES) == 0 and s % CONF_CHUNK == 0
    assert all(t % rows == 0 for rows in (TILES.norm_rows, TILES.inproj[0], TILES.outproj[0],
                                          TILES.router_rows, TILES.combine_rows))
    x2 = x.reshape(t, d)

    h = _rmsnorm(x2, norm1_g[0], tm=TILES.norm_rows)
    h_meta = _rmsnorm(meta_tokens.astype(x.dtype), norm1_g[0], tm=N_META)
    z, z_meta = _inproj(h, h_meta, w_in[0], *TILES.inproj)
    z3 = z.reshape(b, s, -1)

    y_lru = _lru_mixer(z3, z_meta, lru_conv_w[0], lru_conv_b[0], lru_w_a[0], lru_b_a[0],
                       lru_w_i[0], lru_b_i[0], lru_lambda[0])
    y_conf = _conf_mixer(z3, z_meta, conf_conv_w[0], conf_conv_b[0], conf_norm_g[0],
                         conf_norm_b[0])
    x_mid = _outproj(y_lru.reshape(t, -1), y_conf.reshape(t, -1), w_out[0], x2, *TILES.outproj)

    out = _moe_and_final_norm(x_mid, norm2_g[0], w_router[0], b_router[0], w_gate[0], b_gate[0],
                              w_up[0], b_up[0], w_down[0], b_down[0], final_norm_g)
    return out.reshape(b, s, d)
```

```python
from typing import NamedTuple

import jax
import jax.numpy as jnp
from jax import lax
from jax.experimental import pallas as pl
from jax.experimental.pallas import tpu as pltpu

N_META = 16
LRU_WIDTH = 2048
LRU_HEADS = 16
HEAD_DIM = 128
LRU_CONV_WIDTH = 4
LRU_C = 8.0
CONF_WIDTH = 2048
CONF_GROUPS = 16
CONF_KERNEL = 31
N_EXPERTS = 32
TOP_K = 4
SWIGLU_ALPHA = 1.702
SWIGLU_LIMIT = 7.0
RMS_EPS = 1e-5
LN_EPS = 1e-5
SQRT_FLOOR = 1e-30

SUBLANES = 8
V7X_VMEM_BYTES = 64 * 1024 * 1024
VMEM_LIMIT = V7X_VMEM_BYTES - 6 * 1024 * 1024

ROW_BLOCK = 128
MATMUL_ROWS = 1536
PASS_ROWS = 1536
BF16 = jnp.bfloat16
F32 = jnp.float32
U32 = jnp.uint32


def _params(*sem):
    return pltpu.CompilerParams(dimension_semantics=sem, vmem_limit_bytes=VMEM_LIMIT)


def _pack_bf16_pair(hi, lo):
    hi_bits = lax.bitcast_convert_type(hi.astype(BF16).astype(F32), U32)
    lo_bits = lax.bitcast_convert_type(lo.astype(BF16).astype(F32), U32)
    return hi_bits | lax.shift_right_logical(lo_bits, jnp.uint32(16))


def _sigmoid(x):
    return 0.5 * jnp.tanh(0.5 * x) + 0.5


def _unpack_bf16_pair(u):
    hi = lax.bitcast_convert_type(u & jnp.uint32(0xFFFF0000), F32)
    lo = lax.bitcast_convert_type(lax.shift_left(u, jnp.uint32(16)), F32)
    return hi, lo


def _rmsnorm_kernel(x_ref, g_ref, o_ref):
    x = x_ref[...]
    ms = jnp.mean(x * x, axis=-1, keepdims=True)
    o_ref[...] = (x * lax.rsqrt(ms + RMS_EPS) * g_ref[...]).astype(o_ref.dtype)


def _rmsnorm(x, g, tm):
    t, d = x.shape
    return pl.pallas_call(
        _rmsnorm_kernel,
        grid=(t // tm,),
        in_specs=[pl.BlockSpec((tm, d), lambda i: (i, 0)),
                  pl.BlockSpec((1, d), lambda i: (0, 0))],
        out_specs=pl.BlockSpec((tm, d), lambda i: (i, 0)),
        out_shape=jax.ShapeDtypeStruct((t, d), BF16),
        compiler_params=_params("parallel"),
        name="rmsnorm1",
    )(x, g.reshape(1, d))


def _inproj_kernel(a_ref, am_ref, w_ref, o_ref, om_ref, wb_ref):
    @pl.when(pl.program_id(1) == 0)
    def _():
        wb_ref[...] = w_ref[...].astype(BF16)
        om_ref[...] = jnp.dot(am_ref[...], wb_ref[...], preferred_element_type=F32)

    o_ref[...] = jnp.dot(a_ref[...], wb_ref[...], preferred_element_type=F32)


def _inproj(a, a_meta, w, tm, tn):
    t, k = a.shape
    n = w.shape[1]
    return pl.pallas_call(
        _inproj_kernel,
        grid=(n // tn, t // tm),
        in_specs=[pl.BlockSpec((tm, k), lambda j, i: (i, 0)),
                  pl.BlockSpec((N_META, k), lambda j, i: (0, 0)),
                  pl.BlockSpec((k, tn), lambda j, i: (0, j))],
        out_specs=[pl.BlockSpec((tm, tn), lambda j, i: (i, j)),
                   pl.BlockSpec((N_META, tn), lambda j, i: (0, j))],
        out_shape=[jax.ShapeDtypeStruct((t, n), F32),
                   jax.ShapeDtypeStruct((N_META, n), F32)],
        scratch_shapes=[pltpu.VMEM((k, tn), BF16)],
        compiler_params=_params("parallel", "arbitrary"),
        name="inproj",
    )(a, a_meta, w)


LRU_CHUNKS = SUBLANES
LRU_CHUNK_PAD = 4
LRU_SCAN_UNROLL = 8


def _gelu_tanh(x):
    return 0.5 * x * (1.0 + jnp.tanh(0.7978845608028654 * (x + 0.044715 * x * x * x)))


def _lru_kernel(xr_ref, gate_ref, xm_ref, cw_ref, cb_ref, wa_ref, ba_ref, wi_ref,
                bi_ref, lam_ref, o_ref, sf_ref, af_ref, bf_ref, ab_ref, bb_ref, hf_ref, hb_ref):
    s = xr_ref.shape[0]
    st = s + N_META
    pad = SUBLANES
    clen = s // LRU_CHUNKS
    pitch = clen + LRU_CHUNK_PAD
    zeros8 = jnp.zeros((pad, HEAD_DIM), F32)
    sf_ref[0:pad, :] = zeros8
    sf_ref[pad:pad + N_META, :] = xm_ref[...]
    sf_ref[pad + N_META:pad + st, :] = xr_ref[...]
    sf_ref[pad + st:pad + st + pad, :] = zeros8

    def gates(u, d):
        ub = u.astype(BF16)
        t_r = jnp.tanh(jnp.dot(ub, (0.5 * wa_ref[d]).astype(BF16), preferred_element_type=F32)
                       + 0.5 * ba_ref[d:d + 1, :])
        t_i = jnp.tanh(jnp.dot(ub, (0.5 * wi_ref[d]).astype(BF16), preferred_element_type=F32)
                       + 0.5 * bi_ref[d:d + 1, :])
        lam = lam_ref[d:d + 1, :]
        softplus_neg = jnp.maximum(-lam, 0.0) + jnp.log1p(jnp.exp(-jnp.abs(lam)))
        half_c = (-0.5 * LRU_C) * softplus_neg
        log_a = half_c * t_r + half_c
        a = jnp.exp(log_a)
        one_minus_a2 = -jnp.tanh(log_a) * (a * a + 1.0)
        root = one_minus_a2 * lax.rsqrt(jnp.maximum(one_minus_a2, SQRT_FLOOR))
        half_u = 0.5 * u
        b = root * (half_u * t_i + half_u)
        return a, b

    def conv(d, row0, n, sign):
        u = cb_ref[d:d + 1, :] + cw_ref[d, 3:4, :] * sf_ref[pl.ds(row0, n), :]
        for j in range(1, LRU_CONV_WIDTH):
            u = u + cw_ref[d, 3 - j:4 - j, :] * sf_ref[pl.ds(row0 - sign * j, n), :]
        return u

    a, b = gates(conv(0, pad, N_META, 1), 0)
    h_meta = b[0:1, :]
    for r in range(1, N_META):
        h_meta = a[r:r + 1, :] * h_meta + b[r:r + 1, :]

    base = pad + N_META
    for c in reversed(range(LRU_CHUNKS)):
        misaligned = (c * pitch) % SUBLANES
        lead = misaligned + SUBLANES if misaligned else 0
        win = c * pitch - lead
        n = clen + (2 * SUBLANES if misaligned else 0)
        t0 = c * clen - lead
        a, b = gates(conv(0, base + t0, n, 1), 0)
        af_ref[win:win + n, :] = a
        bf_ref[win:win + n, :] = b
        a, b = gates(conv(1, base + t0, n, -1), 1)
        ab_ref[win:win + n, :] = a
        bb_ref[win:win + n, :] = b

    def step_rows(t):
        return (pl.ds(t, LRU_CHUNKS, stride=pitch),
                pl.ds(clen - 1 - t, LRU_CHUNKS, stride=pitch))

    def two_steps(a_ref, b_ref, r0, r1):
        a0, a1 = a_ref[r0, :], a_ref[r1, :]
        b0 = b_ref[r0, :]
        return a0, b0, a1 * a0, a1 * b0 + b_ref[r1, :]

    def local_scan(i, carry):
        hf, pf, hb, pb = carry
        (rf0, rb0), (rf1, rb1) = step_rows(2 * i), step_rows(2 * i + 1)
        _, _, a2f, b2f = two_steps(af_ref, bf_ref, rf0, rf1)
        _, _, a2b, b2b = two_steps(ab_ref, bb_ref, rb0, rb1)
        return (a2f * hf + b2f, a2f * pf, a2b * hb + b2b, a2b * pb)

    zero = sf_ref[0:SUBLANES, :]
    one = zero + 1.0
    hf, pf, hb, pb = lax.fori_loop(0, clen // 2, local_scan, (zero, one, zero, one),
                                   unroll=LRU_SCAN_UNROLL // 2)

    row = lax.broadcasted_iota(jnp.int32, (LRU_CHUNKS, HEAD_DIM), 0)
    c = h_meta
    start_f = jnp.where(row == 0, c, zero)
    for k in range(1, LRU_CHUNKS):
        c = hf[k - 1:k, :] + pf[k - 1:k, :] * c
        start_f = jnp.where(row == k, c, start_f)
    c = zero[0:1, :]
    start_b = zero
    for k in range(LRU_CHUNKS - 2, -1, -1):
        c = hb[k + 1:k + 2, :] + pb[k + 1:k + 2, :] * c
        start_b = jnp.where(row == k, c, start_b)

    def final_scan(i, carry):
        hf, hb = carry
        (rf0, rb0), (rf1, rb1) = step_rows(2 * i), step_rows(2 * i + 1)
        a0f, b0f, a2f, b2f = two_steps(af_ref, bf_ref, rf0, rf1)
        a0b, b0b, a2b, b2b = two_steps(ab_ref, bb_ref, rb0, rb1)
        hf_ref[rf0, :] = a0f * hf + b0f
        hb_ref[rb0, :] = a0b * hb + b0b
        hf = a2f * hf + b2f
        hb = a2b * hb + b2b
        hf_ref[rf1, :] = hf
        hb_ref[rb1, :] = hb
        return hf, hb

    lax.fori_loop(0, clen // 2, final_scan, (start_f, start_b), unroll=LRU_SCAN_UNROLL // 2)
    for c in range(LRU_CHUNKS):
        hsum = hf_ref[pl.ds(c * pitch, clen), :] + hb_ref[pl.ds(c * pitch, clen), :]
        y = hsum * _gelu_tanh(gate_ref[c * clen:(c + 1) * clen, :])
        o_ref[c * clen:(c + 1) * clen, :] = y.astype(o_ref.dtype)


def _lru_mixer(z3, z_meta, cw, cb, wa, ba, wi, bi, lam):
    b, s, _ = z3.shape
    st = s + N_META
    hd = HEAD_DIM
    col = lambda off: (lambda bi_, h: (bi_, 0, off + h))
    return pl.pallas_call(
        _lru_kernel,
        grid=(b, LRU_HEADS),
        in_specs=[
            pl.BlockSpec((None, s, hd), col(0)),
            pl.BlockSpec((None, s, hd), col(LRU_HEADS)),
            pl.BlockSpec((N_META, hd), lambda bi_, h: (0, h)),
            pl.BlockSpec((2, LRU_CONV_WIDTH, hd), lambda bi_, h: (0, 0, h)),
            pl.BlockSpec((2, hd), lambda bi_, h: (0, h)),
            pl.BlockSpec((2, None, hd, hd), lambda bi_, h: (0, h, 0, 0)),
            pl.BlockSpec((2, hd), lambda bi_, h: (0, h)),
            pl.BlockSpec((2, None, hd, hd), lambda bi_, h: (0, h, 0, 0)),
            pl.BlockSpec((2, hd), lambda bi_, h: (0, h)),
            pl.BlockSpec((2, hd), lambda bi_, h: (0, h)),
        ],
        out_specs=pl.BlockSpec((None, s, hd), lambda bi_, h: (bi_, 0, h)),
        out_shape=jax.ShapeDtypeStruct((b, s, LRU_WIDTH), BF16),
        scratch_shapes=[
            pltpu.VMEM((st + 2 * SUBLANES, hd), F32),
        ] + [pltpu.VMEM((LRU_CHUNKS * (s // LRU_CHUNKS + LRU_CHUNK_PAD), hd), F32)] * 6,
        compiler_params=_params("parallel", "parallel"),
        name="lru_mixer",
    )(z3, z3, z_meta, cw, cb, wa, ba.reshape(2, LRU_WIDTH), wi, bi.reshape(2, LRU_WIDTH), lam)


CONF_CHUNK = 64
CONF_PARTIAL_SUMS = 2


def _conf_kernel(a_ref, b_ref, am_ref, bm_ref, cw_ref, cb_ref, g_ref, be_ref, o_ref,
                 cs_ref, sh_ref):
    s = a_ref.shape[0]
    st = s + N_META
    n_sh = sh_ref.shape[1]
    cs_ref[0:N_META, :] = am_ref[...] * _sigmoid(bm_ref[...])
    cs_ref[N_META:st, :] = a_ref[...] * _sigmoid(b_ref[...])
    cs_ref[st:st + N_META, :] = jnp.zeros((N_META, HEAD_DIM), F32)
    for r in range(1, SUBLANES):
        sh_ref[r - 1] = cs_ref[pl.ds(r, n_sh), :]

    n_chunks = s // CONF_CHUNK

    def conv_chunk(row0):
        parts = [None] * CONF_PARTIAL_SUMS
        for k in range(CONF_KERNEL):
            off = N_META - CONF_KERNEL // 2 + k
            r, q = off % SUBLANES, off // SUBLANES
            rows = pl.ds(row0 + q * SUBLANES, CONF_CHUNK)
            term = cw_ref[k:k + 1, :] * (cs_ref[rows, :] if r == 0 else sh_ref[r - 1, rows, :])
            p = k % CONF_PARTIAL_SUMS
            parts[p] = term if parts[p] is None else parts[p] + term
        return sum(parts[1:], parts[0]) + cb_ref[...]

    def step(j, carry):
        acc_prev, xc_prev = carry
        mean = jnp.mean(acc_prev, axis=-1, keepdims=True)
        var = jnp.mean(xc_prev * xc_prev, axis=-1, keepdims=True)
        acc = conv_chunk(pl.multiple_of(jnp.minimum(j, n_chunks - 1) * CONF_CHUNK, CONF_CHUNK))
        y = xc_prev * lax.rsqrt(var + LN_EPS) * g_ref[...] + be_ref[...]
        out_row = pl.multiple_of(jnp.maximum(j - 2, 0) * CONF_CHUNK, CONF_CHUNK)
        half_y = 0.5 * y
        o_ref[pl.ds(out_row, CONF_CHUNK), :] = (half_y * jnp.tanh(half_y) + half_y).astype(o_ref.dtype)
        return acc, acc_prev - mean

    warmup = cs_ref[pl.ds(0, CONF_CHUNK), :]
    lax.fori_loop(0, n_chunks + 2, step, (warmup, warmup))


def _conf_mixer(z3, z_meta, cw, cb, g, be):
    b, s, _ = z3.shape
    st = s + N_META
    hd = HEAD_DIM
    a_off = 2 * LRU_HEADS
    b_off = 2 * LRU_HEADS + CONF_GROUPS
    n_sh = st + N_META - SUBLANES
    vec = pl.BlockSpec((1, hd), lambda bi_, h: (0, h))
    return pl.pallas_call(
        _conf_kernel,
        grid=(b, CONF_GROUPS),
        in_specs=[
            pl.BlockSpec((None, s, hd), lambda bi_, h: (bi_, 0, a_off + h)),
            pl.BlockSpec((None, s, hd), lambda bi_, h: (bi_, 0, b_off + h)),
            pl.BlockSpec((N_META, hd), lambda bi_, h: (0, a_off + h)),
            pl.BlockSpec((N_META, hd), lambda bi_, h: (0, b_off + h)),
            pl.BlockSpec((CONF_KERNEL, hd), lambda bi_, h: (0, h)),
            vec, vec, vec,
        ],
        out_specs=pl.BlockSpec((None, s, hd), lambda bi_, h: (bi_, 0, h)),
        out_shape=jax.ShapeDtypeStruct((b, s, CONF_WIDTH), BF16),
        scratch_shapes=[
            pltpu.VMEM((st + N_META, hd), F32),
            pltpu.VMEM((SUBLANES - 1, n_sh, hd), F32),
        ],
        compiler_params=_params("parallel", "parallel"),
        name="conf_mixer",
    )(z3, z3, z_meta, z_meta, cw, cb.reshape(1, -1), g.reshape(1, -1), be.reshape(1, -1))


def _outproj_kernel(ya_ref, yb_ref, wa_ref, wb_ref, x_ref, o_ref):
    acc = jnp.dot(ya_ref[...], wa_ref[...].astype(BF16), preferred_element_type=F32)
    acc = acc + jnp.dot(yb_ref[...], wb_ref[...].astype(BF16), preferred_element_type=F32)
    o_ref[...] = x_ref[...] + acc


def _outproj(ya, yb, w, x, tm, tn):
    t, k = ya.shape
    n = w.shape[1]
    return pl.pallas_call(
        _outproj_kernel,
        grid=(t // tm, n // tn),
        in_specs=[pl.BlockSpec((tm, k), lambda i, j: (i, 0)),
                  pl.BlockSpec((tm, k), lambda i, j: (i, 0)),
                  pl.BlockSpec((k, tn), lambda i, j: (0, j)),
                  pl.BlockSpec((k, tn), lambda i, j: (1, j)),
                  pl.BlockSpec((tm, tn), lambda i, j: (i, j))],
        out_specs=pl.BlockSpec((tm, tn), lambda i, j: (i, j)),
        out_shape=jax.ShapeDtypeStruct((t, n), F32),
        compiler_params=_params("parallel", "parallel"),
        name="outproj",
    )(ya, yb, w, w, x)


def _router_kernel(x_ref, g_ref, wr_ref, br_ref, hp_ref, idx_ref, gate_ref, rank_ref,
                   cnt_ref, carry_ref):
    tt, d = x_ref.shape
    half = d // 2

    @pl.when(pl.program_id(0) == 0)
    def _():
        carry_ref[...] = jnp.zeros_like(carry_ref)

    x = x_ref[...]
    ms = jnp.mean(x * x, axis=-1, keepdims=True)
    h = x * lax.rsqrt(ms + RMS_EPS) * g_ref[...]
    hp_ref[...] = _pack_bf16_pair(h[:, :half], h[:, half:])

    h_hi = h.astype(BF16)
    h_lo = (h - h_hi.astype(F32)).astype(BF16)
    w = wr_ref[...]
    w_hi = w.astype(BF16)
    w_lo = (w - w_hi.astype(F32)).astype(BF16)
    hi_terms = jnp.dot(h_hi, jnp.concatenate([w_hi, w_lo], axis=1), preferred_element_type=F32)
    logits = (hi_terms[:, :N_EXPERTS] + hi_terms[:, N_EXPERTS:]
              + jnp.dot(h_lo, w_hi, preferred_element_type=F32)) + br_ref[...]

    lane = lax.broadcasted_iota(jnp.int32, (tt, N_EXPERTS), 1)
    lane_k = lax.broadcasted_iota(jnp.int32, (tt, TOP_K), 1)
    work = logits
    vals, sels = [], []
    idx_out = jnp.zeros((tt, TOP_K), jnp.int32)
    for k in range(TOP_K):
        m = jnp.max(work, axis=1, keepdims=True)
        am = jnp.min(jnp.where(work == m, lane, N_EXPERTS), axis=1, keepdims=True)
        sel = lane == am
        vals.append(m)
        sels.append(sel)
        idx_out = jnp.where(lane_k == k, am, idx_out)
        work = jnp.where(sel, -jnp.inf, work)
    idx_ref[...] = idx_out

    exps = [jnp.exp(v - vals[0]) for v in vals]
    denom = exps[0] + exps[1] + exps[2] + exps[3]
    gate_out = jnp.zeros((tt, TOP_K), F32)
    for k in range(TOP_K):
        gate_out = jnp.where(lane_k == k, exps[k] / denom, gate_out)
    gate_ref[...] = gate_out

    onehot = jnp.zeros((tt, N_EXPERTS), F32)
    for sel in sels:
        onehot = onehot + sel.astype(F32)
    r_i = lax.broadcasted_iota(jnp.int32, (tt, tt), 0)
    c_i = lax.broadcasted_iota(jnp.int32, (tt, tt), 1)
    tri = (c_i < r_i).astype(BF16)
    before = jnp.dot(tri, onehot.astype(BF16), preferred_element_type=F32) + carry_ref[...]
    rank_out = jnp.zeros((tt, TOP_K), jnp.int32)
    for k, sel in enumerate(sels):
        rk = jnp.sum(jnp.where(sel, before, 0.0), axis=1, keepdims=True).astype(jnp.int32)
        rank_out = jnp.where(lane_k == k, rk, rank_out)
    rank_ref[...] = rank_out
    carry_ref[...] = carry_ref[...] + jnp.sum(onehot, axis=0, keepdims=True)
    cnt_ref[...] = carry_ref[...].astype(jnp.int32)


def _router(x, g, wr, br, tt):
    t, d = x.shape
    small = lambda dt: jax.ShapeDtypeStruct((t, TOP_K), dt)
    return pl.pallas_call(
        _router_kernel,
        grid=(t // tt,),
        in_specs=[pl.BlockSpec((tt, d), lambda i: (i, 0)),
                  pl.BlockSpec((1, d), lambda i: (0, 0)),
                  pl.BlockSpec((d, N_EXPERTS), lambda i: (0, 0)),
                  pl.BlockSpec((1, N_EXPERTS), lambda i: (0, 0))],
        out_specs=[pl.BlockSpec((tt, d // 2), lambda i: (i, 0)),
                   pl.BlockSpec((tt, TOP_K), lambda i: (i, 0)),
                   pl.BlockSpec((tt, TOP_K), lambda i: (i, 0)),
                   pl.BlockSpec((tt, TOP_K), lambda i: (i, 0)),
                   pl.BlockSpec((1, N_EXPERTS), lambda i: (0, 0))],
        out_shape=[jax.ShapeDtypeStruct((t, d // 2), U32),
                   small(jnp.int32), small(F32), small(jnp.int32),
                   jax.ShapeDtypeStruct((1, N_EXPERTS), jnp.int32)],
        scratch_shapes=[pltpu.VMEM((1, N_EXPERTS), F32)],
        compiler_params=_params("arbitrary"),
        name="router",
    )(x, g.reshape(1, d), wr, br.reshape(1, N_EXPERTS))


def _num_passes(n_assign):
    return N_EXPERTS + n_assign // PASS_ROWS


def _per_expert(table, experts):
    onehot = experts[..., None] == jnp.arange(N_EXPERTS, dtype=experts.dtype)
    return jnp.sum(jnp.where(onehot, table, 0), axis=-1)


def _routing_tables(idx, rank, counts, n_pass_max):
    padded = (counts + ROW_BLOCK - 1) // ROW_BLOCK * ROW_BLOCK
    passes_e = (padded + PASS_ROWS - 1) // PASS_ROWS
    pass_end = jnp.cumsum(passes_e)
    pass_start = pass_end - passes_e
    n_pass = pass_end[-1]
    pos = (_per_expert(pass_start, idx) + rank // PASS_ROWS) * PASS_ROWS + rank % PASS_ROWS
    p_ids = jnp.arange(n_pass_max, dtype=jnp.int32)
    live = p_ids < n_pass
    pe = jnp.sum(jnp.minimum(p_ids, n_pass - 1)[:, None] >= pass_end[None, :], axis=1)
    pe = jnp.minimum(pe, N_EXPERTS - 1).astype(jnp.int32)
    done = (p_ids - _per_expert(pass_start, pe)) * PASS_ROWS
    rows = jnp.clip(_per_expert(padded, pe) - done, 0, PASS_ROWS)
    pass_nb = jnp.where(live, rows // ROW_BLOCK, 0).astype(jnp.int32)
    pass_valid = jnp.where(live, jnp.clip(_per_expert(counts, pe) - done, 0, PASS_ROWS),
                           0).astype(jnp.int32)
    return (pe, pass_nb, pass_valid, n_pass.reshape(1).astype(jnp.int32),
            pos.T.reshape(-1).astype(jnp.int32))


GATHER_UNROLL = SUBLANES
GATHER_WINDOWS = 4


def _gather_rows(src_hbm, idx_ref, idx_base, n_rows, dst_ref, sem):
    per_trip = GATHER_UNROLL * GATHER_WINDOWS

    def issue(g, _):
        for w in range(GATHER_WINDOWS):
            r0 = pl.multiple_of(g * per_trip + w * GATHER_UNROLL, GATHER_UNROLL)
            dst_tile = dst_ref.at[pl.ds(r0, GATHER_UNROLL)]
            for u in range(GATHER_UNROLL):
                pltpu.make_async_copy(src_hbm.at[pl.ds(idx_ref[idx_base + r0 + u], 1)],
                                      dst_tile.at[pl.ds(u, 1)], sem).start()
        return 0

    lax.fori_loop(0, n_rows // per_trip, issue, 0)


def _for_row_blocks(n_blocks, body):
    per_big = MATMUL_ROWS // ROW_BLOCK
    n_big = n_blocks // per_big

    def big(i, _):
        body(pl.multiple_of(i * MATMUL_ROWS, MATMUL_ROWS), MATMUL_ROWS)
        return 0

    lax.fori_loop(0, n_big, big, 0)
    rest = n_blocks - n_big * per_big
    start = pl.multiple_of(n_big * MATMUL_ROWS, MATMUL_ROWS)
    for m in range(1, per_big):
        @pl.when(rest == m)
        def _(m=m):
            body(start, m * ROW_BLOCK)


def _wait_row_blocks(src_hbm, dst_ref, n_blocks, rows, sem):
    def drain(i, _):
        pltpu.make_async_copy(src_hbm.at[pl.ds(0, rows)], dst_ref.at[pl.ds(0, rows)], sem).wait()
        return 0

    lax.fori_loop(0, n_blocks, drain, 0)


def _build_token_table(pos_ref, nb_ref, valid_ref, n_pass, cap, tok_ref):
    def pad_pass(q, _):
        def pad_row(r, _):
            tok_ref[q * cap + r] = 0
            return 0

        lax.fori_loop(valid_ref[q], nb_ref[q] * ROW_BLOCK, pad_row, 0)
        return 0

    lax.fori_loop(0, n_pass, pad_pass, 0)
    n_tok = pos_ref.shape[0] // TOP_K
    for k in range(TOP_K):
        def place(g, _, k=k):
            for u in range(GATHER_UNROLL):
                t = g * GATHER_UNROLL + u
                tok_ref[pos_ref[k * n_tok + t]] = t
            return 0

        lax.fori_loop(0, n_tok // GATHER_UNROLL, place, 0)


def _moe_up_kernel(pe_ref, nb_ref, valid_ref, npass_ref, pos_ref, hp_hbm, wg_ref, wu_ref, bg_ref,
                   bu_ref, o_ref, gbuf_ref, xb_ref, tok_ref, sem):
    p = pl.program_id(0)
    f = pl.program_id(1)
    cap, half = gbuf_ref.shape
    n_pass = npass_ref[0]
    live = p < n_pass
    nb = nb_ref[p]

    @pl.when(jnp.logical_and(live, f == 0))
    def _():
        @pl.when(p == 0)
        def _():
            _build_token_table(pos_ref, nb_ref, valid_ref, n_pass, cap, tok_ref)
            _gather_rows(hp_hbm, tok_ref, 0, nb * ROW_BLOCK, gbuf_ref, sem)

        _wait_row_blocks(hp_hbm, gbuf_ref, nb, ROW_BLOCK, sem)

        def unpack(rb, _):
            rows = pl.ds(pl.multiple_of(rb * ROW_BLOCK, ROW_BLOCK), ROW_BLOCK)
            hi, lo = _unpack_bf16_pair(gbuf_ref[rows, :])
            xb_ref[rows, :half] = hi.astype(BF16)
            xb_ref[rows, half:] = lo.astype(BF16)
            return 0

        lax.fori_loop(0, nb, unpack, 0)

        @pl.when(p + 1 < n_pass)
        def _():
            _gather_rows(hp_hbm, tok_ref, (p + 1) * cap, nb_ref[p + 1] * ROW_BLOCK, gbuf_ref, sem)

    @pl.when(live)
    def _():
        def block(row0, m):
            rows = pl.ds(row0, m)
            x = xb_ref[rows, :]
            hg = jnp.dot(x, wg_ref[...].astype(BF16), preferred_element_type=F32) + bg_ref[...]
            hu = jnp.dot(x, wu_ref[...].astype(BF16), preferred_element_type=F32) + bu_ref[...]
            hg = jnp.minimum(hg, SWIGLU_LIMIT)
            hu = jnp.clip(hu, -SWIGLU_LIMIT, SWIGLU_LIMIT)
            act = hg * _sigmoid(SWIGLU_ALPHA * hg) * (hu + 1.0)
            o_ref[rows, :] = act.astype(o_ref.dtype)

        _for_row_blocks(nb, block)


def _moe_up(pe, pass_nb, pass_valid, n_pass, pos_flat, hp, wg, wu, bg, bu, n_pass_max, tf):
    e, d, f = wg.shape
    nf = f // tf
    cap = PASS_ROWS
    assert ROW_BLOCK % (GATHER_UNROLL * GATHER_WINDOWS) == 0 and cap % ROW_BLOCK == 0

    def w_map(p, j, pe_, nb_, valid_, np_, pos_):
        return (pe_[p], 0, jnp.where(p < np_[0], j, nf - 1))

    def o_map(p, j, pe_, nb_, valid_, np_, pos_):
        ok = p < np_[0]
        return (jnp.where(ok, p, np_[0] - 1), jnp.where(ok, j, nf - 1))

    grid_spec = pltpu.PrefetchScalarGridSpec(
        num_scalar_prefetch=5,
        grid=(n_pass_max, nf),
        in_specs=[pl.BlockSpec(memory_space=pl.ANY),
                  pl.BlockSpec((None, d, tf), w_map),
                  pl.BlockSpec((None, d, tf), w_map),
                  pl.BlockSpec((None, 1, tf), w_map),
                  pl.BlockSpec((None, 1, tf), w_map)],
        out_specs=pl.BlockSpec((cap, tf), o_map),
        scratch_shapes=[pltpu.VMEM((cap, d // 2), U32),
                        pltpu.VMEM((cap, d), BF16),
                        pltpu.SMEM((n_pass_max * cap,), jnp.int32),
                        pltpu.SemaphoreType.DMA(())],
    )
    return pl.pallas_call(
        _moe_up_kernel,
        grid_spec=grid_spec,
        out_shape=jax.ShapeDtypeStruct((n_pass_max * cap, f), BF16),
        compiler_params=_params("arbitrary", "arbitrary"),
        name="moe_up",
    )(pe, pass_nb, pass_valid, n_pass, pos_flat, hp, wg, wu, bg.reshape(e, 1, f),
      bu.reshape(e, 1, f))


def _moe_down_kernel(pe_ref, nb_ref, npass_ref, a_ref, wh_ref, wl_ref, bh_ref, bl_ref, o_ref):
    p = pl.program_id(0)

    @pl.when(p < npass_ref[0])
    def _():
        def block(row0, m):
            rows = pl.ds(row0, m)
            a = a_ref[rows, :]
            hi = jnp.dot(a, wh_ref[...].astype(BF16), preferred_element_type=F32) + bh_ref[...]
            lo = jnp.dot(a, wl_ref[...].astype(BF16), preferred_element_type=F32) + bl_ref[...]
            o_ref[rows, :] = _pack_bf16_pair(hi, lo)

        _for_row_blocks(nb_ref[p], block)


def _moe_down(pe, pass_nb, n_pass, act, wd, bd, n_pass_max, td):
    e, f, d = wd.shape
    nd = d // 2 // td
    cap = PASS_ROWS

    def col(p, j, np_):
        return jnp.where(p < np_[0], j, nd - 1)

    def row(p, np_):
        return jnp.where(p < np_[0], p, np_[0] - 1)

    grid_spec = pltpu.PrefetchScalarGridSpec(
        num_scalar_prefetch=3,
        grid=(n_pass_max, nd),
        in_specs=[pl.BlockSpec((cap, f), lambda p, j, pe_, nb_, np_: (row(p, np_), 0)),
                  pl.BlockSpec((None, f, td), lambda p, j, pe_, nb_, np_: (pe_[p], 0, col(p, j, np_))),
                  pl.BlockSpec((None, f, td), lambda p, j, pe_, nb_, np_: (pe_[p], 0, nd + col(p, j, np_))),
                  pl.BlockSpec((None, 1, td), lambda p, j, pe_, nb_, np_: (pe_[p], 0, col(p, j, np_))),
                  pl.BlockSpec((None, 1, td), lambda p, j, pe_, nb_, np_: (pe_[p], 0, nd + col(p, j, np_)))],
        out_specs=pl.BlockSpec((cap, td), lambda p, j, pe_, nb_, np_: (row(p, np_), col(p, j, np_))),
    )
    bd3 = bd.reshape(e, 1, d)
    return pl.pallas_call(
        _moe_down_kernel,
        grid_spec=grid_spec,
        out_shape=jax.ShapeDtypeStruct((n_pass_max * cap, d // 2), U32),
        compiler_params=_params("arbitrary", "arbitrary"),
        name="moe_down",
    )(pe, pass_nb, n_pass, act, wd, wd, bd3, bd3)


def _combine_kernel(pos_ref, y_hbm, x_ref, gate_ref, g_ref, o_ref, ybuf_ref, sems):
    i = pl.program_id(0)
    n = pl.num_programs(0)
    tt, d = x_ref.shape
    half = d // 2
    slot = i % 2

    n_tok = n * tt

    def start_tile(step, slot_):
        for k in range(TOP_K):
            _gather_rows(y_hbm, pos_ref, k * n_tok + step * tt, tt, ybuf_ref.at[slot_, k],
                         sems.at[slot_])

    @pl.when(i == 0)
    def _():
        start_tile(0, 0)

    def wait_tile(slot_):
        for k in range(TOP_K):
            pltpu.make_async_copy(y_hbm.at[pl.ds(0, tt)], ybuf_ref.at[slot_, k],
                                  sems.at[slot_]).wait()

    wait_tile(slot)
    nxt = jnp.minimum(i + 1, n - 1)
    for k in range(TOP_K):
        for r0 in range(0, tt, GATHER_UNROLL):
            dst_tile = ybuf_ref.at[1 - slot, k].at[pl.ds(r0, GATHER_UNROLL)]
            for u in range(GATHER_UNROLL):
                pltpu.make_async_copy(
                    y_hbm.at[pl.ds(pos_ref[k * n_tok + nxt * tt + r0 + u], 1)],
                    dst_tile.at[pl.ds(u, 1)], sems.at[1 - slot]).start()

    gates = gate_ref[...]
    x = x_ref[...]
    acc_hi = x[:, :half]
    acc_lo = x[:, half:]
    for k in range(TOP_K):
        hi, lo = _unpack_bf16_pair(ybuf_ref[slot, k])
        acc_hi = acc_hi + gates[:, k:k + 1] * hi
        acc_lo = acc_lo + gates[:, k:k + 1] * lo
    ms = (jnp.sum(acc_hi * acc_hi, axis=-1, keepdims=True)
          + jnp.sum(acc_lo * acc_lo, axis=-1, keepdims=True)) * (1.0 / d)
    scale = lax.rsqrt(ms + RMS_EPS)
    o_ref[:, :half] = acc_hi * scale * g_ref[:, :half]
    o_ref[:, half:] = acc_lo * scale * g_ref[:, half:]

    @pl.when(i == n - 1)
    def _():
        wait_tile(1 - slot)


def _combine(pos_flat, y, x, gates, g, tt):
    t, d = x.shape
    assert tt % (GATHER_UNROLL * GATHER_WINDOWS) == 0
    grid_spec = pltpu.PrefetchScalarGridSpec(
        num_scalar_prefetch=1,
        grid=(t // tt,),
        in_specs=[pl.BlockSpec(memory_space=pl.ANY),
                  pl.BlockSpec((tt, d), lambda i, p: (i, 0)),
                  pl.BlockSpec((tt, TOP_K), lambda i, p: (i, 0)),
                  pl.BlockSpec((1, d), lambda i, p: (0, 0))],
        out_specs=pl.BlockSpec((tt, d), lambda i, p: (i, 0)),
        scratch_shapes=[pltpu.VMEM((2, TOP_K, tt, d // 2), U32),
                        pltpu.SemaphoreType.DMA((2,))],
    )
    return pl.pallas_call(
        _combine_kernel,
        grid_spec=grid_spec,
        out_shape=jax.ShapeDtypeStruct((t, d), F32),
        compiler_params=_params("arbitrary"),
        name="combine",
    )(pos_flat, y, x, gates, g.reshape(1, d))


class _Tiles(NamedTuple):
    norm_rows: int = 512
    inproj: tuple = (512, 1024)
    outproj: tuple = (1024, 512)
    router_rows: int = 512
    up_cols: int = 256
    down_cols: int = 512
    combine_rows: int = 256


TILES = _Tiles()


def _moe_and_final_norm(x_mid, norm2_g, w_router, b_router, w_gate, b_gate, w_up, b_up,
                        w_down, b_down, final_norm_g):
    t = x_mid.shape[0]
    n_pass_max = _num_passes(t * TOP_K)
    hp, idx, gates, rank, counts = _router(x_mid, norm2_g, w_router, b_router,
                                           tt=TILES.router_rows)
    pe, pass_nb, pass_valid, n_pass, pos_flat = _routing_tables(idx, rank, counts[0], n_pass_max)
    act = _moe_up(pe, pass_nb, pass_valid, n_pass, pos_flat, hp, w_gate, w_up, b_gate, b_up,
                  n_pass_max, tf=TILES.up_cols)
    y_rows = _moe_down(pe, pass_nb, n_pass, act, w_down, b_down, n_pass_max, td=TILES.down_cols)
    return _combine(pos_flat, y_rows, x_mid, gates, final_norm_g, tt=TILES.combine_rows)


def kernel(x, meta_tokens, norm1_g, w_in, lru_conv_w, lru_conv_b, lru_w_a, lru_b_a, lru_w_i, lru_b_i, lru_lambda, conf_conv_w, conf_conv_b, conf_norm_g, conf_norm_b, w_out, norm2_g, w_router, b_router, w_gate, b_gate, w_up, b_up, w_down, b_down, final_norm_g):
    b, s, d = x.shape
    t = b * s
    assert norm1_g.shape[0] == 1, "one layer"
    assert d == LRU_WIDTH + CONF_WIDTH == LRU_HEADS * HEAD_DIM + CONF_GROUPS * HEAD_DIM
    assert s % (LRU_CHUNKS * SUBLANES) == 0 and s % CONF_CHUNK == 0
    assert all(t % rows == 0 for rows in (TILES.norm_rows, TILES.inproj[0], TILES.outproj[0],
                                          TILES.router_rows, TILES.combine_rows))
    x2 = x.reshape(t, d)

    h = _rmsnorm(x2, norm1_g[0], tm=TILES.norm_rows)
    h_meta = _rmsnorm(meta_tokens.astype(x.dtype), norm1_g[0], tm=N_META)
    z, z_meta = _inproj(h, h_meta, w_in[0], *TILES.inproj)
    z3 = z.reshape(b, s, -1)

    y_lru = _lru_mixer(z3, z_meta, lru_conv_w[0], lru_conv_b[0], lru_w_a[0], lru_b_a[0],
                       lru_w_i[0], lru_b_i[0], lru_lambda[0])
    y_conf = _conf_mixer(z3, z_meta, conf_conv_w[0], conf_conv_b[0], conf_norm_g[0],
                         conf_norm_b[0])
    x_mid = _outproj(y_lru.reshape(t, -1), y_conf.reshape(t, -1), w_out[0], x2, *TILES.outproj)

    out = _moe_and_final_norm(x_mid, norm2_g[0], w_router[0], b_router[0], w_gate[0], b_gate[0],
                              w_up[0], b_up[0], w_down[0], b_down[0], final_norm_g)
    return out.reshape(b, s, d)
```
